```python
import math
import functools
import jax
import jax.numpy as jnp
from jax import lax
import numpy as np

D_MODEL = 1024
BATCH = 4
SEQ = 4096
DEPTH = 2
DEC_BATCH = 32
DEC_SEQ = 8
PAST_LEN = 8192
PAGE_SIZE = 128

HEAD_DIM = 64
ROT_DIM = HEAD_DIM // 4
ROPE_THETA = 500000.0
NORM_EPS = 1e-6
NEG_INF = -1e30
Q_BLOCK = 128
SEL_Q_BLOCK = 32

NSA_HEADS = D_MODEL // (2 * HEAD_DIM)
NSA_KV_HEADS = 2
NSA_GROUP = NSA_HEADS // NSA_KV_HEADS
NSA_BLOCK = 64
NSA_N_SEL = 16
NSA_WINDOW = 512
NSA_Q_W = NSA_HEADS * HEAD_DIM
NSA_KV_W = 2 * NSA_KV_HEADS * HEAD_DIM

GM_GROUPS = 8
GM_WIDTH = D_MODEL // 2
GM_GROUP_W = GM_WIDTH // GM_GROUPS
GM_CHUNK = 128

AB_SPLITS = (NSA_Q_W, 3 * NSA_HEADS, NSA_KV_W, NSA_KV_W, NSA_KV_W, GM_WIDTH, GM_WIDTH)
AB_IN_W = NSA_Q_W + 3 * NSA_HEADS + 3 * NSA_KV_W + 2 * GM_WIDTH
AB_OUT_W = NSA_Q_W + GM_WIDTH

DIL_CFG = ((128, 1), (512, 4), (2048, 16))
DIL_HEADS = 8
DIL_KV_HEADS = 2
DIL_GROUP = DIL_HEADS // DIL_KV_HEADS
DIL_Q_W = DIL_HEADS * HEAD_DIM
DIL_KV_W = DIL_KV_HEADS * HEAD_DIM
C_IN_W = len(DIL_CFG) * (DIL_Q_W + 2 * DIL_KV_W)
C_OUT_W = DIL_Q_W

N_EXPERTS = 32
TOP_K = 4
D_FF = D_MODEL
SWIGLU_ALPHA = 1.702
SWIGLU_LIMIT = 7.0
MOE_BLOCK = 128

N_AB_LAYERS = (DEPTH + 1) // 2
N_C_LAYERS = DEPTH // 2

kernel_name = 'hybrid_nsa_gmlp_dilated_moe_step'


def _rms(x, g):
    xf = x.astype(jnp.float32)
    y = xf * lax.rsqrt(jnp.mean(xf * xf, axis=-1, keepdims=True) + NORM_EPS)
    return (y * g.astype(jnp.float32)).astype(x.dtype)


def _layernorm(x, g, b):
    xf = x.astype(jnp.float32)
    mu = jnp.mean(xf, axis=-1, keepdims=True)
    var = jnp.mean(jnp.square(xf - mu), axis=-1, keepdims=True)
    y = (xf - mu) * lax.rsqrt(var + NORM_EPS)
    return (y * g.astype(jnp.float32) + b.astype(jnp.float32)).astype(x.dtype)


def _rope(x, pos):
    half = ROT_DIM // 2
    inv = jnp.exp(-math.log(ROPE_THETA) * jnp.arange(half, dtype=jnp.float32) * (2.0 / ROT_DIM))
    ang = pos.astype(jnp.float32)[:, None] * inv[None, :]
    cos = jnp.cos(ang)[:, None, :]
    sin = jnp.sin(ang)[:, None, :]
    xr = x[..., :ROT_DIM].astype(jnp.float32)
    x1, x2 = xr[..., :half], xr[..., half:]
    rot = jnp.concatenate([x1 * cos - x2 * sin, x2 * cos + x1 * sin], axis=-1).astype(x.dtype)
    return jnp.concatenate([rot, x[..., ROT_DIM:]], axis=-1)


def _split_last(x, sizes):
    return jnp.split(x, np.cumsum(sizes)[:-1].tolist(), axis=-1)


def _qblock(t, cap):
    return t if t <= cap else cap


def _to_blocks(x, axis, qb):
    nb = x.shape[axis] // qb
    x = x.reshape(x.shape[:axis] + (nb, qb) + x.shape[axis + 1:])
    return jnp.moveaxis(x, axis, 0)


def _from_blocks(y):
    y = jnp.moveaxis(y, 0, 1)
    return y.reshape((y.shape[0], y.shape[1] * y.shape[2]) + y.shape[3:])


def _empty_kv(x, n_kv):
    return jnp.zeros((x.shape[0], 0, 2, n_kv, HEAD_DIM), x.dtype)


def _paged_past(cache, page_table):
    pages = cache[page_table]
    return pages.reshape((page_table.shape[0], page_table.shape[1] * cache.shape[1]) + cache.shape[2:])


def _band_attend(qg, kv, window):
    T = qg.shape[1]
    off = kv.shape[1] - T
    scale = HEAD_DIM ** -0.5
    kv_pad = jnp.pad(kv, ((0, 0), (window, 0), (0, 0), (0, 0), (0, 0)))
    qb_n = _qblock(T, Q_BLOCK)
    span = qb_n + window
    rel_q = jnp.arange(qb_n)
    rel_k = jnp.arange(span)
    dist = window + rel_q[:, None] - rel_k[None, :]
    in_band = (dist >= 0) & (dist <= window)

    def band_block(args):
        qb, i0 = args
        start = off + i0
        kvb = lax.dynamic_slice_in_dim(kv_pad, start, span, axis=1)
        sc = jnp.einsum('bqkgd,bjkd->bkgqj', qb, kvb[:, :, 0], preferred_element_type=jnp.float32) * scale
        m = in_band & ((start + rel_k) >= window)[None, :]
        pr = jax.nn.softmax(jnp.where(m, sc, NEG_INF), axis=-1)
        return jnp.einsum('bkgqj,bjkd->bqkgd', pr.astype(kvb.dtype), kvb[:, :, 1])

    i0s = jnp.arange(T // qb_n) * qb_n
    return _from_blocks(lax.map(band_block, (_to_blocks(qg, 1, qb_n), i0s)))


def _dilated_attend(qg, kv, window, dilation):
    T = qg.shape[1]
    off = kv.shape[1] - T
    scale = HEAD_DIM ** -0.5
    n_keys = window // dilation + 1
    kv_pad = jnp.pad(kv, ((0, 0), (window, 0), (0, 0), (0, 0), (0, 0)))
    qb_n = _qblock(T, Q_BLOCK)
    steps = dilation * jnp.arange(n_keys)

    def dil_block(args):
        qb, i0 = args
        own = window + off + i0 + jnp.arange(qb_n)
        idx = own[:, None] - steps[None, :]
        kvb = jnp.take(kv_pad, idx, axis=1)
        sc = jnp.einsum('bqkgd,bqjkd->bkgqj', qb, kvb[:, :, :, 0], preferred_element_type=jnp.float32) * scale
        sc = jnp.where(idx >= window, sc, NEG_INF)
        lse = jax.nn.logsumexp(sc, axis=-1)
        pr = jnp.exp(sc - lse[..., None])
        o = jnp.einsum('bkgqj,bqjkd->bqkgd', pr.astype(kvb.dtype), kvb[:, :, :, 1])
        return o, jnp.transpose(lse, (0, 3, 1, 2))

    i0s = jnp.arange(T // qb_n) * qb_n
    o, lse = lax.map(dil_block, (_to_blocks(qg, 1, qb_n), i0s))
    return _from_blocks(o), _from_blocks(lse)


def _nsa(q, q_rot, pos, kvc_all, kvs_all, kvw_all, gates, pe, w_phi, kc_gain):
    B, T = q.shape[:2]
    L = kvc_all.shape[1]
    scale = HEAD_DIM ** -0.5
    qg = q.reshape(B, T, NSA_KV_HEADS, NSA_GROUP, HEAD_DIM)
    qr = q_rot.reshape(B, T, NSA_KV_HEADS, NSA_GROUP, HEAD_DIM)
    n_cb = L // NSA_BLOCK
    blocks = kvc_all[:, :n_cb * NSA_BLOCK].reshape(B, n_cb, NSA_BLOCK, 2, NSA_KV_HEADS, HEAD_DIM)
    summ = jnp.mean(blocks + pe[None, None, :, :, None, :], axis=2)
    summ = jnp.einsum('bnckd,cde->bncke', summ, w_phi)
    kc = _rms(summ[:, :, 0], kc_gain)
    vc = summ[:, :, 1]
    sc = jnp.einsum('btkgd,bnkd->bkgtn', qg, kc, preferred_element_type=jnp.float32) * scale
    cur = pos // NSA_BLOCK
    vis = jnp.arange(n_cb)[None, :] < cur[:, None]
    p = jax.nn.softmax(jnp.where(vis, sc, NEG_INF), axis=-1) * vis
    o_cmp = jnp.einsum('bkgtn,bnkd->btkgd', p.astype(vc.dtype), vc)
    imp = jnp.where(vis, p.sum(axis=2), -1.0)
    _, top = lax.top_k(imp, min(NSA_N_SEL - 1, n_cb))
    cur_b = cur[None, None, :, None]
    blk = jnp.concatenate([top, jnp.broadcast_to(cur_b, (B, NSA_KV_HEADS, T, 1)).astype(top.dtype)], axis=-1)
    ok = jnp.concatenate([top < cur_b, jnp.ones((B, NSA_KV_HEADS, T, 1), bool)], axis=-1)
    n_ball = -(-L // NSA_BLOCK)
    kvs_pad = jnp.pad(kvs_all, ((0, 0), (0, n_ball * NSA_BLOCK - L), (0, 0), (0, 0), (0, 0)))
    kvs_blk = kvs_pad.reshape(B, n_ball, NSA_BLOCK, 2, NSA_KV_HEADS, HEAD_DIM).transpose(0, 4, 1, 2, 3, 5)
    b_idx = jnp.arange(B)[:, None, None, None]
    h_idx = jnp.arange(NSA_KV_HEADS)[None, :, None, None]

    def sel_block(args):
        qb, pb, bb, okb = args
        kv = kvs_blk[b_idx, h_idx, bb]
        s = jnp.einsum('bqkgd,bkqsjd->bkgqsj', qb, kv[..., 0, :], preferred_element_type=jnp.float32) * scale
        kpos = bb[..., None] * NSA_BLOCK + jnp.arange(NSA_BLOCK)
        m = okb[..., None] & (kpos <= pb[None, None, :, None, None])
        s = jnp.where(m[:, :, None], s, NEG_INF)
        shp = s.shape
        pr = jax.nn.softmax(s.reshape(shp[:4] + (-1,)), axis=-1).reshape(shp)
        return jnp.einsum('bkgqsj,bkqsjd->bqkgd', pr.astype(kv.dtype), kv[..., 1, :])

    qb_n = _qblock(T, SEL_Q_BLOCK)
    o_slc = _from_blocks(lax.map(sel_block, (_to_blocks(qr, 1, qb_n), _to_blocks(pos, 0, qb_n),
                                             _to_blocks(blk, 2, qb_n), _to_blocks(ok, 2, qb_n))))
    o_win = _band_attend(qr, kvw_all, NSA_WINDOW)
    return gates[..., 0:1] * o_cmp + gates[..., 1:2] * o_slc + gates[..., 2:3] * o_win


def _chunk_mix(v, ws, bs):
    B, T, W = v.shape
    tp = -(-T // GM_CHUNK) * GM_CHUNK
    vp = jnp.pad(v, ((0, 0), (0, tp - T), (0, 0))).reshape(B, tp // GM_CHUNK, GM_CHUNK, GM_GROUPS, GM_GROUP_W)
    wm = jnp.where(jnp.tril(jnp.ones((GM_CHUNK, GM_CHUNK), bool)), ws, 0.0).astype(v.dtype)
    s = jnp.einsum('gij,bnjgc->bnigc', wm, vp) + jnp.transpose(bs)[None, None, :, :, None]
    return s.reshape(B, tp, W)[:, :T]


def _key_prep(kv, g, pos):
    B, T = kv.shape[:2]
    kv = kv.reshape(B, T, 2, NSA_KV_HEADS, HEAD_DIM)
    k = _rope(_rms(kv[:, :, 0], g), pos)
    return jnp.stack([k, kv[:, :, 1]], axis=2)


def _mixer_ab(h, start, pasts, weights):
    cmp_past, slc_past, win_past = pasts
    w_in, w_out, qk_gain, pe, w_phi, ln_g, ln_b, ws, bs = weights
    B, T, _ = h.shape
    pos = start + jnp.arange(T, dtype=jnp.int32)
    q, gate, kvc, kvs, kvw, u, v = _split_last(h @ w_in, AB_SPLITS)
    q = _rms(q.reshape(B, T, NSA_HEADS, HEAD_DIM), qk_gain[0])
    q_rot = _rope(q, pos)
    kvc = kvc.reshape(B, T, 2, NSA_KV_HEADS, HEAD_DIM)
    kvs = _key_prep(kvs, qk_gain[2], pos)
    kvw = _key_prep(kvw, qk_gain[3], pos)
    kvw_all = jnp.concatenate([win_past, kvw], axis=1)
    g = jax.nn.sigmoid(gate).reshape(B, T, NSA_KV_HEADS, NSA_GROUP, 3)
    o_a = _nsa(q, q_rot, pos, jnp.concatenate([cmp_past, kvc], axis=1), jnp.concatenate([slc_past, kvs], axis=1),
               kvw_all, g, pe, w_phi, qk_gain[1])
    u = jax.nn.gelu(u)
    v = _layernorm(jax.nn.gelu(v), ln_g, ln_b)
    o_b = u * _chunk_mix(v, ws, bs)
    y = jnp.concatenate([o_a.reshape(B, T, NSA_Q_W), o_b], axis=-1) @ w_out
    keep_w = min(NSA_WINDOW, kvw_all.shape[1])
    n_cur = (T - 1) % GM_CHUNK + 1
    return y, (kvc, kvs, kvw_all[:, kvw_all.shape[1] - keep_w:], v[:, T - n_cur:])


def _mixer_c(h, start, pasts, weights):
    w_in, w_out, qk_gain = weights
    B, T, _ = h.shape
    pos = start + jnp.arange(T, dtype=jnp.int32)
    parts = _split_last(h @ w_in, (DIL_Q_W, DIL_KV_W, DIL_KV_W) * len(DIL_CFG))
    outs, lses, new_bufs = [], [], []
    for g, (window, dilation) in enumerate(DIL_CFG):
        q, k, v = parts[3 * g], parts[3 * g + 1], parts[3 * g + 2]
        q = _rope(_rms(q.reshape(B, T, DIL_HEADS, HEAD_DIM), qk_gain[0]), pos)
        k = _rope(_rms(k.reshape(B, T, DIL_KV_HEADS, HEAD_DIM), qk_gain[1]), pos)
        kv_new = jnp.stack([k, v.reshape(B, T, DIL_KV_HEADS, HEAD_DIM)], axis=2)
        kv_all = jnp.concatenate([pasts[g], kv_new], axis=1)
        o, lse = _dilated_attend(q.reshape(B, T, DIL_KV_HEADS, DIL_GROUP, HEAD_DIM), kv_all, window, dilation)
        outs.append(o)
        lses.append(lse)
        keep = min(window, kv_all.shape[1])
        new_bufs.append(kv_all[:, kv_all.shape[1] - keep:])
    alpha = jax.nn.softmax(jnp.stack(lses, axis=0), axis=0)
    o = jnp.sum(alpha[..., None].astype(outs[0].dtype) * jnp.stack(outs, axis=0), axis=0)
    return o.reshape(B, T, C_OUT_W) @ w_out, tuple(new_bufs)


def _moe(x, w_router, b_router, w_up, b_up, w_down, b_down):
    B, T, D = x.shape
    n_tok = B * T
    xf = x.reshape(n_tok, D)
    logits = (xf @ w_router + b_router).astype(jnp.float32)
    top_val, top_exp = lax.top_k(logits, TOP_K)
    gate = jax.nn.softmax(top_val, axis=-1)
    n_asg = n_tok * TOP_K
    flat_e = top_exp.reshape(n_asg)
    flat_tok = jnp.arange(n_asg, dtype=jnp.int32) // TOP_K
    order = jnp.argsort(flat_e)
    sorted_e = flat_e[order]
    counts = jnp.bincount(flat_e, length=N_EXPERTS)
    padded = (counts + MOE_BLOCK - 1) // MOE_BLOCK * MOE_BLOCK
    pad_end = jnp.cumsum(padded)
    pad_start = pad_end - padded
    grp_start = jnp.cumsum(counts) - counts
    dest = pad_start[sorted_e] + jnp.arange(n_asg) - grp_start[sorted_e]
    n_blocks = -(-(n_asg + N_EXPERTS * (MOE_BLOCK - 1)) // MOE_BLOCK)
    n_slots = n_blocks * MOE_BLOCK
    slot_tok = jnp.full((n_slots,), n_tok, jnp.int32).at[dest].set(flat_tok[order])
    slot_gate = jnp.zeros((n_slots,), jnp.float32).at[dest].set(gate.reshape(n_asg)[order])
    blk_exp = jnp.minimum(jnp.searchsorted(pad_end, jnp.arange(n_blocks) * MOE_BLOCK, side='right'), N_EXPERTS - 1)
    x_pad = jnp.concatenate([xf, jnp.zeros((1, D), xf.dtype)], axis=0)

    def expert_block(args):
        tok, g, e = args
        hu = x_pad[tok] @ w_up[e] + b_up[e]
        glu, lin = jnp.split(hu, 2, axis=-1)
        glu = jnp.minimum(glu, SWIGLU_LIMIT)
        lin = jnp.clip(lin, -SWIGLU_LIMIT, SWIGLU_LIMIT)
        act = glu * jax.nn.sigmoid(SWIGLU_ALPHA * glu) * (lin + 1.0)
        return (act @ w_down[e] + b_down[e]) * g[:, None].astype(xf.dtype)

    out = lax.map(expert_block, (slot_tok.reshape(n_blocks, MOE_BLOCK), slot_gate.reshape(n_blocks, MOE_BLOCK), blk_exp))
    y = jnp.zeros((n_tok + 1, D), xf.dtype).at[slot_tok].add(out.reshape(n_slots, D))
    return y[:n_tok].reshape(B, T, D)


def _adaln(c, w, b):
    mod = jax.nn.silu(c) @ w + b
    return [m[:, None, :] for m in jnp.split(mod, 6, axis=-1)]


def _layer(x, c, mixer, norm_m, norm_f, w_ada_l, b_ada_l, moe_w):
    sh_m, sc_m, g_m, sh_f, sc_f, g_f = _adaln(c, w_ada_l, b_ada_l)
    y, state = mixer(_rms(x, norm_m) * (1.0 + sc_m) + sh_m)
    x = x + g_m * y
    x = x + g_f * _moe(_rms(x, norm_f) * (1.0 + sc_f) + sh_f, *moe_w)
    return x, state


def _stack(states, j):
    return jnp.stack([s[j] for s in states], axis=0)


def setup_inputs(seed: int = 0) -> dict:
    key = jax.random.key(seed)
    keys = iter(jax.random.split(key, 48))
    f32 = jnp.float32

    def nrm(shape, scale=1.0):
        return scale * jax.random.normal(next(keys), shape, f32)

    n_pages = PAST_LEN // PAGE_SIZE
    n_phys = (DEC_BATCH * n_pages * 5) // 4
    page_table = jax.random.permutation(next(keys), n_phys)[:DEC_BATCH * n_pages].reshape(DEC_BATCH, n_pages).astype(jnp.int32)
    na, nc = N_AB_LAYERS, N_C_LAYERS
    kv_a = (2, NSA_KV_HEADS, HEAD_DIM)
    kv_c = (2, DIL_KV_HEADS, HEAD_DIM)
    return {
        'x_prompt': nrm((BATCH, SEQ, D_MODEL)),
        'x_sample': nrm((DEC_BATCH, DEC_SEQ, D_MODEL)),
        'cache_cmp_kv': nrm((na, n_phys, PAGE_SIZE) + kv_a),
        'cache_slc_kv': nrm((na, n_phys, PAGE_SIZE) + kv_a),
        'state_win_kv': nrm((na, DEC_BATCH, min(NSA_WINDOW, PAST_LEN)) + kv_a),
        'state_dil0_kv': nrm((nc, DEC_BATCH, min(DIL_CFG[0][0], PAST_LEN)) + kv_c),
        'state_dil1_kv': nrm((nc, DEC_BATCH, min(DIL_CFG[1][0], PAST_LEN)) + kv_c),
        'state_dil2_kv': nrm((nc, DEC_BATCH, min(DIL_CFG[2][0], PAST_LEN)) + kv_c),
        'page_table': page_table,
        'c_prompt': nrm((BATCH, D_MODEL)),
        'c_sample': nrm((DEC_BATCH, D_MODEL)),
        'norm_mix': 1.0 + nrm((DEPTH, D_MODEL), 0.1),
        'norm_ffn': 1.0 + nrm((DEPTH, D_MODEL), 0.1),
        'w_ada': nrm((DEPTH, D_MODEL, 6 * D_MODEL), 0.3 * D_MODEL ** -0.5),
        'b_ada': nrm((DEPTH, 6 * D_MODEL), 0.02),
        'w_in_ab': nrm((na, D_MODEL, AB_IN_W), D_MODEL ** -0.5),
        'w_out_ab': nrm((na, AB_OUT_W, D_MODEL), AB_OUT_W ** -0.5),
        'nsa_qk_gain': 1.0 + nrm((na, 4, HEAD_DIM), 0.1),
        'nsa_pe': nrm((na, NSA_BLOCK, 2, HEAD_DIM), 0.1),
        'nsa_w_phi': nrm((na, 2, HEAD_DIM, HEAD_DIM), HEAD_DIM ** -0.5),
        'gm_ln_g': 1.0 + nrm((na, GM_WIDTH), 0.1),
        'gm_ln_b': nrm((na, GM_WIDTH), 0.02),
        'gm_ws': nrm((na, GM_GROUPS, GM_CHUNK, GM_CHUNK), GM_CHUNK ** -0.5),
        'gm_bs': 1.0 + nrm((na, GM_GROUPS, GM_CHUNK), 0.1),
        'w_in_c': nrm((nc, D_MODEL, C_IN_W), D_MODEL ** -0.5),
        'w_out_c': nrm((nc, C_OUT_W, D_MODEL), C_OUT_W ** -0.5),
        'dil_qk_gain': 1.0 + nrm((nc, 2, HEAD_DIM), 0.1),
        'w_router': nrm((DEPTH, D_MODEL, N_EXPERTS), D_MODEL ** -0.5),
        'b_router': nrm((DEPTH, N_EXPERTS), 0.01),
        'w_up': nrm((DEPTH, N_EXPERTS, D_MODEL, 2 * D_FF), D_MODEL ** -0.5),
        'b_up': nrm((DEPTH, N_EXPERTS, 2 * D_FF), 0.01),
        'w_down': nrm((DEPTH, N_EXPERTS, D_FF, D_MODEL), D_FF ** -0.5),
        'b_down': nrm((DEPTH, N_EXPERTS, D_MODEL), 0.01),
    }


def reference(x_prompt, x_sample, cache_cmp_kv, cache_slc_kv, state_win_kv, state_dil0_kv, state_dil1_kv,
              state_dil2_kv, page_table, c_prompt, c_sample, norm_mix, norm_ffn, w_ada, b_ada, w_in_ab, w_out_ab,
              nsa_qk_gain, nsa_pe, nsa_w_phi, gm_ln_g, gm_ln_b, gm_ws, gm_bs, w_in_c, w_out_c, dil_qk_gain,
              w_router, b_router, w_up, b_up, w_down, b_down):
    past_len = page_table.shape[1] * PAGE_SIZE
    x_p, x_s = x_prompt, x_sample
    ab_p, ab_s, dil_p, dil_s = [], [], [], []
    for layer in range(DEPTH):
        i = layer // 2
        moe_w = (w_router[layer], b_router[layer], w_up[layer], b_up[layer], w_down[layer], b_down[layer])
        shared = (norm_mix[layer], norm_ffn[layer], w_ada[layer], b_ada[layer], moe_w)
        if layer % 2 == 0:
            wts = (w_in_ab[i], w_out_ab[i], nsa_qk_gain[i], nsa_pe[i], nsa_w_phi[i], gm_ln_g[i], gm_ln_b[i], gm_ws[i], gm_bs[i])
            empty = _empty_kv(x_p, NSA_KV_HEADS)
            mix_p = functools.partial(_mixer_ab, start=0, pasts=(empty, empty, empty), weights=wts)
            pasts_s = (_paged_past(cache_cmp_kv[i], page_table), _paged_past(cache_slc_kv[i], page_table), state_win_kv[i])
            mix_s = functools.partial(_mixer_ab, start=past_len, pasts=pasts_s, weights=wts)
            x_p, st_p = _layer(x_p, c_prompt, mix_p, *shared)
            x_s, st_s = _layer(x_s, c_sample, mix_s, *shared)
            ab_p.append(st_p)
            ab_s.append(st_s)
        else:
            wts = (w_in_c[i], w_out_c[i], dil_qk_gain[i])
            empty = _empty_kv(x_p, DIL_KV_HEADS)
            mix_p = functools.partial(_mixer_c, start=0, pasts=(empty, empty, empty), weights=wts)
            mix_s = functools.partial(_mixer_c, start=past_len,
                                      pasts=(state_dil0_kv[i], state_dil1_kv[i], state_dil2_kv[i]), weights=wts)
            x_p, st_p = _layer(x_p, c_prompt, mix_p, *shared)
            x_s, st_s = _layer(x_s, c_sample, mix_s, *shared)
            dil_p.append(st_p)
            dil_s.append(st_s)
    cmp_p, cmp_s = _stack(ab_p, 0), _stack(ab_s, 0)
    slc_p, slc_s = _stack(ab_p, 1), _stack(ab_s, 1)
    win_p, win_s = _stack(ab_p, 2), _stack(ab_s, 2)
    gmv_p, gmv_s = _stack(ab_p, 3), _stack(ab_s, 3)
    d0_p, d0_s = _stack(dil_p, 0), _stack(dil_s, 0)
    d1_p, d1_s = _stack(dil_p, 1), _stack(dil_s, 1)
    d2_p, d2_s = _stack(dil_p, 2), _stack(dil_s, 2)
    return (x_p, x_s, cmp_p, cmp_s, slc_p, slc_s, win_p, win_s, gmv_p, gmv_s, d0_p, d0_s, d1_p, d1_s, d2_p, d2_s)
```

```python
import functools
import math

import jax
import jax.numpy as jnp
from jax import lax
import numpy as np
from jax.experimental import pallas as pl
from jax.experimental.pallas import tpu as pltpu

F32 = jnp.float32
BF16 = jnp.bfloat16

D_MODEL = 1024
HEAD_DIM = 64
ROT_DIM = HEAD_DIM // 4
ROPE_THETA = 500000.0
NORM_EPS = 1e-6
NEG_INF = -1e30
Q_BLOCK = 128
SEL_Q_BLOCK = 32
PAGE_SIZE = 128

NSA_HEADS = D_MODEL // (2 * HEAD_DIM)
NSA_KV_HEADS = 2
NSA_GROUP = NSA_HEADS // NSA_KV_HEADS
NSA_BLOCK = 64
NSA_N_SEL = 16
NSA_WINDOW = 512
NSA_Q_W = NSA_HEADS * HEAD_DIM
NSA_KV_W = 2 * NSA_KV_HEADS * HEAD_DIM

GM_GROUPS = 8
GM_WIDTH = D_MODEL // 2
GM_GROUP_W = GM_WIDTH // GM_GROUPS
GM_CHUNK = 128

AB_SPLITS = (NSA_Q_W, 3 * NSA_HEADS, NSA_KV_W, NSA_KV_W, NSA_KV_W, GM_WIDTH, GM_WIDTH)
AB_SPLITS_GATE_LAST = (AB_SPLITS[0],) + AB_SPLITS[2:] + (AB_SPLITS[1],)

DIL_CFG = ((128, 1), (512, 4), (2048, 16))
DIL_HEADS = 8
DIL_KV_HEADS = 2
DIL_GROUP = DIL_HEADS // DIL_KV_HEADS
DIL_Q_W = DIL_HEADS * HEAD_DIM
DIL_KV_W = DIL_KV_HEADS * HEAD_DIM
C_OUT_W = DIL_Q_W

N_EXPERTS = 32
TOP_K = 4
D_FF = D_MODEL
SWIGLU_ALPHA = 1.702
SWIGLU_LIMIT = 7.0

VMEM_LIMIT_BYTES = 56 * 1024 * 1024
ROW_TILE = 512
MOE_TILE = 512


def _row_tile(t):
    return t if t <= ROW_TILE else ROW_TILE


def _mod_spec(mod, tm):
    if mod.shape[1] == 1:
        return pl.BlockSpec((1, 1, mod.shape[2]), lambda b, i: (b, 0, 0))
    return pl.BlockSpec((1, tm, mod.shape[2]), lambda b, i: (b, i, 0))


def _split_bf16(v):
    hi = v.astype(BF16)
    lo = (v - hi.astype(F32)).astype(BF16)
    return hi, lo


def _mod_norm(x, g, sc, sh):
    h = x * lax.rsqrt(jnp.mean(x * x, axis=-1, keepdims=True) + NORM_EPS) * g
    return h * (1.0 + sc) + sh


def _mod_norm_proj_kernel(x_ref, g_ref, sc_ref, sh_ref, w_ref, *o_refs, splits):
    h = _mod_norm(x_ref[0], g_ref[...], sc_ref[0], sh_ref[0])
    y = jnp.dot(h.astype(BF16), w_ref[...], preferred_element_type=F32)
    off = 0
    for o_ref, n in zip(o_refs, splits):
        o_ref[0] = y[:, off:off + n]
        off += n


def _mod_norm_proj(x, norm_g, scale, shift, w_bf16, splits):
    B, T, D = x.shape
    tm = _row_tile(T)
    N = w_bf16.shape[1]
    return pl.pallas_call(
        functools.partial(_mod_norm_proj_kernel, splits=splits),
        out_shape=[jax.ShapeDtypeStruct((B, T, n), F32) for n in splits],
        grid=(B, T // tm),
        in_specs=[
            pl.BlockSpec((1, tm, D), lambda b, i: (b, i, 0)),
            pl.BlockSpec((1, D), lambda b, i: (0, 0)),
            _mod_spec(scale, tm),
            _mod_spec(shift, tm),
            pl.BlockSpec((D, N), lambda b, i: (0, 0)),
        ],
        out_specs=[pl.BlockSpec((1, tm, n), lambda b, i: (b, i, 0)) for n in splits],
        compiler_params=pltpu.CompilerParams(
            dimension_semantics=("parallel", "parallel"), vmem_limit_bytes=VMEM_LIMIT_BYTES),
        name="mod_norm_proj",
    )(x, norm_g.reshape(1, D), scale, shift, w_bf16)


def _proj_residual_kernel(*refs, n_parts):
    a_refs = refs[:n_parts]
    w_refs = refs[n_parts:2 * n_parts]
    x_ref, g_ref, o_ref = refs[2 * n_parts:]
    y = None
    for a_ref, w_ref in zip(a_refs, w_refs):
        p = jnp.dot(a_ref[0].astype(BF16), w_ref[...], preferred_element_type=F32)
        y = p if y is None else y + p
    o_ref[0] = x_ref[0] + g_ref[0] * y


def _proj_residual(a_parts, w_parts_bf16, x, gate):
    B, T, D = x.shape
    tm = _row_tile(T)
    n_parts = len(a_parts)
    in_specs = [pl.BlockSpec((1, tm, a.shape[2]), lambda b, i: (b, i, 0)) for a in a_parts]
    in_specs += [pl.BlockSpec(w.shape, lambda b, i: (0, 0)) for w in w_parts_bf16]
    in_specs += [pl.BlockSpec((1, tm, D), lambda b, i: (b, i, 0)), _mod_spec(gate, tm)]
    return pl.pallas_call(
        functools.partial(_proj_residual_kernel, n_parts=n_parts),
        out_shape=jax.ShapeDtypeStruct((B, T, D), F32),
        grid=(B, T // tm),
        in_specs=in_specs,
        out_specs=pl.BlockSpec((1, tm, D), lambda b, i: (b, i, 0)),
        compiler_params=pltpu.CompilerParams(
            dimension_semantics=("parallel", "parallel"), vmem_limit_bytes=VMEM_LIMIT_BYTES),
        name="proj_residual",
    )(*a_parts, *w_parts_bf16, x, gate)


def _mod_norm_router_kernel(x_ref, g_ref, sc_ref, sh_ref, whi_ref, wlo_ref, b_ref, h_ref, l_ref):
    h = _mod_norm(x_ref[0], g_ref[...], sc_ref[0], sh_ref[0])
    h_hi, h_lo = _split_bf16(h)
    h_ref[0] = h_hi
    w_hi = whi_ref[...]
    logits = (jnp.dot(h_hi, w_hi, preferred_element_type=F32)
              + jnp.dot(h_lo, w_hi, preferred_element_type=F32)
              + jnp.dot(h_hi, wlo_ref[...], preferred_element_type=F32))
    l_ref[0] = logits + b_ref[...]


def _mod_norm_router(x, norm_g, scale, shift, w_router, b_router):
    B, T, D = x.shape
    tm = _row_tile(T)
    E = w_router.shape[1]
    w_hi, w_lo = _split_bf16(w_router)
    return pl.pallas_call(
        _mod_norm_router_kernel,
        out_shape=[jax.ShapeDtypeStruct((B, T, D), BF16), jax.ShapeDtypeStruct((B, T, E), F32)],
        grid=(B, T // tm),
        in_specs=[
            pl.BlockSpec((1, tm, D), lambda b, i: (b, i, 0)),
            pl.BlockSpec((1, D), lambda b, i: (0, 0)),
            _mod_spec(scale, tm),
            _mod_spec(shift, tm),
            pl.BlockSpec((D, E), lambda b, i: (0, 0)),
            pl.BlockSpec((D, E), lambda b, i: (0, 0)),
            pl.BlockSpec((1, E), lambda b, i: (0, 0)),
        ],
        out_specs=[pl.BlockSpec((1, tm, D), lambda b, i: (b, i, 0)),
                   pl.BlockSpec((1, tm, E), lambda b, i: (b, i, 0))],
        compiler_params=pltpu.CompilerParams(
            dimension_semantics=("parallel", "parallel"), vmem_limit_bytes=VMEM_LIMIT_BYTES),
        name="mod_norm_router",
    )(x, norm_g.reshape(1, D), scale, shift, w_hi, w_lo, b_router.reshape(1, E))


def _moe_ffn_kernel(te_ref, tx_ref, tv_ref, x_ref, wu_ref, bu_ref, wd_ref, bd_ref, g_ref, o_ref):
    i = pl.program_id(0)

    @pl.when(tv_ref[i] != 0)
    def _():
        hu = jnp.dot(x_ref[...], wu_ref[0].astype(BF16), preferred_element_type=F32) + bu_ref[0]
        glu = jnp.minimum(hu[:, :D_FF], SWIGLU_LIMIT)
        lin = jnp.clip(hu[:, D_FF:], -SWIGLU_LIMIT, SWIGLU_LIMIT)
        act = glu * jax.nn.sigmoid(SWIGLU_ALPHA * glu) * (lin + 1.0)
        y = jnp.dot(act.astype(BF16), wd_ref[0].astype(BF16), preferred_element_type=F32) + bd_ref[0]
        o_ref[...] = y * g_ref[...]

    @pl.when(tv_ref[i] == 0)
    def _():
        o_ref[...] = jnp.zeros_like(o_ref)


def _moe_ffn(tile_exp, tile_x, tile_valid, x_sorted, w_up, b_up, w_down, b_down, slot_gate):
    n_slots, D = x_sorted.shape
    n_tiles = n_slots // MOE_TILE
    E, _, F2 = w_up.shape
    grid_spec = pltpu.PrefetchScalarGridSpec(
        num_scalar_prefetch=3,
        grid=(n_tiles,),
        in_specs=[
            pl.BlockSpec((MOE_TILE, D), lambda i, te, tx, tv: (tx[i], 0)),
            pl.BlockSpec((1, D, F2), lambda i, te, tx, tv: (te[i], 0, 0)),
            pl.BlockSpec((1, 1, F2), lambda i, te, tx, tv: (te[i], 0, 0)),
            pl.BlockSpec((1, F2 // 2, D), lambda i, te, tx, tv: (te[i], 0, 0)),
            pl.BlockSpec((1, 1, D), lambda i, te, tx, tv: (te[i], 0, 0)),
            pl.BlockSpec((MOE_TILE, 1), lambda i, te, tx, tv: (tx[i], 0)),
        ],
        out_specs=pl.BlockSpec((MOE_TILE, D), lambda i, te, tx, tv: (i, 0)),
    )
    return pl.pallas_call(
        _moe_ffn_kernel,
        out_shape=jax.ShapeDtypeStruct((n_slots, D), F32),
        grid_spec=grid_spec,
        compiler_params=pltpu.CompilerParams(
            dimension_semantics=("arbitrary",), vmem_limit_bytes=VMEM_LIMIT_BYTES),
        name="moe_ffn",
    )(tile_exp, tile_x, tile_valid, x_sorted, w_up, b_up.reshape(E, 1, F2), w_down,
      b_down.reshape(E, 1, D), slot_gate.reshape(n_slots, 1))


def _moe(h_tok, logits, w_up, b_up, w_down, b_down):
    n_tok, D = h_tok.shape
    top_val, top_exp = lax.top_k(logits, TOP_K)
    gate = jax.nn.softmax(top_val, axis=-1)
    n_asg = n_tok * TOP_K
    flat_e = top_exp.reshape(n_asg)
    flat_tok = jnp.arange(n_asg, dtype=jnp.int32) // TOP_K
    order = jnp.argsort(flat_e)
    sorted_e = flat_e[order]
    counts = jnp.bincount(flat_e, length=N_EXPERTS)
    padded = (counts + MOE_TILE - 1) // MOE_TILE * MOE_TILE
    pad_end = jnp.cumsum(padded)
    pad_start = pad_end - padded
    grp_start = jnp.cumsum(counts) - counts
    dest = (pad_start[sorted_e] + jnp.arange(n_asg) - grp_start[sorted_e]).astype(jnp.int32)
    n_tiles = -(-(n_asg + N_EXPERTS * (MOE_TILE - 1)) // MOE_TILE)
    n_slots = n_tiles * MOE_TILE
    slot_tok = jnp.zeros((n_slots,), jnp.int32).at[dest].set(flat_tok[order])
    slot_gate = jnp.zeros((n_slots,), F32).at[dest].set(gate.reshape(n_asg)[order])
    slot_of_asg = jnp.zeros((n_asg,), jnp.int32).at[order].set(dest)
    tile_start = jnp.arange(n_tiles, dtype=jnp.int32) * MOE_TILE
    n_used = (pad_end[-1] // MOE_TILE).astype(jnp.int32)
    tile_valid = (jnp.arange(n_tiles, dtype=jnp.int32) < n_used).astype(jnp.int32)
    tile_x = jnp.minimum(jnp.arange(n_tiles, dtype=jnp.int32), n_used - 1)
    tile_exp = jnp.minimum(jnp.searchsorted(pad_end, tile_x * MOE_TILE, side='right'),
                           N_EXPERTS - 1).astype(jnp.int32)
    del tile_start
    x_sorted = h_tok[slot_tok]
    out = _moe_ffn(tile_exp, tile_x, tile_valid, x_sorted, w_up, b_up, w_down, b_down, slot_gate)
    return out[slot_of_asg].reshape(n_tok, TOP_K, D).sum(axis=1)


def _rms(x, g):
    xf = x.astype(F32)
    y = xf * lax.rsqrt(jnp.mean(xf * xf, axis=-1, keepdims=True) + NORM_EPS)
    return (y * g.astype(F32)).astype(x.dtype)


def _layernorm(x, g, b):
    xf = x.astype(F32)
    mu = jnp.mean(xf, axis=-1, keepdims=True)
    var = jnp.mean(jnp.square(xf - mu), axis=-1, keepdims=True)
    y = (xf - mu) * lax.rsqrt(var + NORM_EPS)
    return (y * g.astype(F32) + b.astype(F32)).astype(x.dtype)


def _rope(x, pos):
    half = ROT_DIM // 2
    inv = jnp.exp(-math.log(ROPE_THETA) * jnp.arange(half, dtype=F32) * (2.0 / ROT_DIM))
    ang = pos.astype(F32)[:, None] * inv[None, :]
    cos = jnp.cos(ang)[:, None, :]
    sin = jnp.sin(ang)[:, None, :]
    xr = x[..., :ROT_DIM].astype(F32)
    x1, x2 = xr[..., :half], xr[..., half:]
    rot = jnp.concatenate([x1 * cos - x2 * sin, x2 * cos + x1 * sin], axis=-1).astype(x.dtype)
    return jnp.concatenate([rot, x[..., ROT_DIM:]], axis=-1)


def _qblock(t, cap):
    return t if t <= cap else cap


def _to_blocks(x, axis, qb):
    nb = x.shape[axis] // qb
    x = x.reshape(x.shape[:axis] + (nb, qb) + x.shape[axis + 1:])
    return jnp.moveaxis(x, axis, 0)


def _from_blocks(y):
    y = jnp.moveaxis(y, 0, 1)
    return y.reshape((y.shape[0], y.shape[1] * y.shape[2]) + y.shape[3:])


def _empty_kv(x, n_kv):
    return jnp.zeros((x.shape[0], 0, 2, n_kv, HEAD_DIM), x.dtype)


def _paged_past(cache, page_table):
    pages = cache[page_table]
    return pages.reshape((page_table.shape[0], page_table.shape[1] * cache.shape[1]) + cache.shape[2:])


def _band_attend(qg, kv, window):
    T = qg.shape[1]
    off = kv.shape[1] - T
    scale = HEAD_DIM ** -0.5
    kv_pad = jnp.pad(kv, ((0, 0), (window, 0), (0, 0), (0, 0), (0, 0)))
    qb_n = _qblock(T, Q_BLOCK)
    span = qb_n + window
    rel_q = jnp.arange(qb_n)
    rel_k = jnp.arange(span)
    dist = window + rel_q[:, None] - rel_k[None, :]
    in_band = (dist >= 0) & (dist <= window)

    def band_block(args):
        qb, i0 = args
        start = off + i0
        kvb = lax.dynamic_slice_in_dim(kv_pad, start, span, axis=1)
        sc = jnp.einsum('bqkgd,bjkd->bkgqj', qb, kvb[:, :, 0], preferred_element_type=F32) * scale
        m = in_band & ((start + rel_k) >= window)[None, :]
        pr = jax.nn.softmax(jnp.where(m, sc, NEG_INF), axis=-1)
        return jnp.einsum('bkgqj,bjkd->bqkgd', pr.astype(kvb.dtype), kvb[:, :, 1])

    i0s = jnp.arange(T // qb_n) * qb_n
    return _from_blocks(lax.map(band_block, (_to_blocks(qg, 1, qb_n), i0s)))


def _dilated_attend(qg, kv, window, dilation):
    T = qg.shape[1]
    off = kv.shape[1] - T
    scale = HEAD_DIM ** -0.5
    n_keys = window // dilation + 1
    kv_pad = jnp.pad(kv, ((0, 0), (window, 0), (0, 0), (0, 0), (0, 0)))
    qb_n = _qblock(T, Q_BLOCK)
    steps = dilation * jnp.arange(n_keys)

    def dil_block(args):
        qb, i0 = args
        own = window + off + i0 + jnp.arange(qb_n)
        idx = own[:, None] - steps[None, :]
        kvb = jnp.take(kv_pad, idx, axis=1)
        sc = jnp.einsum('bqkgd,bqjkd->bkgqj', qb, kvb[:, :, :, 0], preferred_element_type=F32) * scale
        sc = jnp.where(idx >= window, sc, NEG_INF)
        lse = jax.nn.logsumexp(sc, axis=-1)
        pr = jnp.exp(sc - lse[..., None])
        o = jnp.einsum('bkgqj,bqjkd->bqkgd', pr.astype(kvb.dtype), kvb[:, :, :, 1])
        return o, jnp.transpose(lse, (0, 3, 1, 2))

    i0s = jnp.arange(T // qb_n) * qb_n
    o, lse = lax.map(dil_block, (_to_blocks(qg, 1, qb_n), i0s))
    return _from_blocks(o), _from_blocks(lse)


def _nsa(q, q_rot, pos, kvc_all, kvs_all, kvw_all, gates, pe, w_phi, kc_gain):
    B, T = q.shape[:2]
    L = kvc_all.shape[1]
    scale = HEAD_DIM ** -0.5
    qg = q.reshape(B, T, NSA_KV_HEADS, NSA_GROUP, HEAD_DIM)
    qr = q_rot.reshape(B, T, NSA_KV_HEADS, NSA_GROUP, HEAD_DIM)
    n_cb = L // NSA_BLOCK
    blocks = kvc_all[:, :n_cb * NSA_BLOCK].reshape(B, n_cb, NSA_BLOCK, 2, NSA_KV_HEADS, HEAD_DIM)
    summ = jnp.mean(blocks + pe[None, None, :, :, None, :], axis=2)
    summ = jnp.einsum('bnckd,cde->bncke', summ, w_phi)
    kc = _rms(summ[:, :, 0], kc_gain)
    vc = summ[:, :, 1]
    sc = jnp.einsum('btkgd,bnkd->bkgtn', qg, kc, preferred_element_type=F32) * scale
    cur = pos // NSA_BLOCK
    vis = jnp.arange(n_cb)[None, :] < cur[:, None]
    p = jax.nn.softmax(jnp.where(vis, sc, NEG_INF), axis=-1) * vis
    o_cmp = jnp.einsum('bkgtn,bnkd->btkgd', p.astype(vc.dtype), vc)
    imp = jnp.where(vis, p.sum(axis=2), -1.0)
    _, top = lax.top_k(imp, min(NSA_N_SEL - 1, n_cb))
    cur_b = cur[None, None, :, None]
    blk = jnp.concatenate([top, jnp.broadcast_to(cur_b, (B, NSA_KV_HEADS, T, 1)).astype(top.dtype)], axis=-1)
    ok = jnp.concatenate([top < cur_b, jnp.ones((B, NSA_KV_HEADS, T, 1), bool)], axis=-1)
    n_ball = -(-L // NSA_BLOCK)
    kvs_pad = jnp.pad(kvs_all, ((0, 0), (0, n_ball * NSA_BLOCK - L), (0, 0), (0, 0), (0, 0)))
    kvs_blk = kvs_pad.reshape(B, n_ball, NSA_BLOCK, 2, NSA_KV_HEADS, HEAD_DIM).transpose(0, 4, 1, 2, 3, 5)
    b_idx = jnp.arange(B)[:, None, None, None]
    h_idx = jnp.arange(NSA_KV_HEADS)[None, :, None, None]

    def sel_block(args):
        qb, pb, bb, okb = args
        flat = kvs_blk.reshape((B * NSA_KV_HEADS * n_ball, NSA_BLOCK * 2 * HEAD_DIM))
        rows = (b_idx * NSA_KV_HEADS + h_idx) * n_ball + bb
        kv = jnp.take(flat, rows, axis=0).reshape(rows.shape + (NSA_BLOCK, 2, HEAD_DIM))
        s = jnp.einsum('bqkgd,bkqsjd->bkgqsj', qb, kv[..., 0, :], preferred_element_type=F32) * scale
        kpos = bb[..., None] * NSA_BLOCK + jnp.arange(NSA_BLOCK)
        m = okb[..., None] & (kpos <= pb[None, None, :, None, None])
        s = jnp.where(m[:, :, None], s, NEG_INF)
        shp = s.shape
        pr = jax.nn.softmax(s.reshape(shp[:4] + (-1,)), axis=-1).reshape(shp)
        return jnp.einsum('bkgqsj,bkqsjd->bqkgd', pr.astype(kv.dtype), kv[..., 1, :])

    qb_n = _qblock(T, SEL_Q_BLOCK)
    o_slc = _from_blocks(lax.map(sel_block, (_to_blocks(qr, 1, qb_n), _to_blocks(pos, 0, qb_n),
                                             _to_blocks(blk, 2, qb_n), _to_blocks(ok, 2, qb_n))))
    o_win = _band_attend(qr, kvw_all, NSA_WINDOW)
    return gates[..., 0:1] * o_cmp + gates[..., 1:2] * o_slc + gates[..., 2:3] * o_win


def _chunk_mix(v, ws, bs):
    B, T, W = v.shape
    tp = -(-T // GM_CHUNK) * GM_CHUNK
    vp = jnp.pad(v, ((0, 0), (0, tp - T), (0, 0))).reshape(B, tp // GM_CHUNK, GM_CHUNK, GM_GROUPS, GM_GROUP_W)
    wm = jnp.where(jnp.tril(jnp.ones((GM_CHUNK, GM_CHUNK), bool)), ws, 0.0).astype(v.dtype)
    s = jnp.einsum('gij,bnjgc->bnigc', wm, vp) + jnp.transpose(bs)[None, None, :, :, None]
    return s.reshape(B, tp, W)[:, :T]


def _key_prep(kv, g, pos):
    B, T = kv.shape[:2]
    kv = kv.reshape(B, T, 2, NSA_KV_HEADS, HEAD_DIM)
    k = _rope(_rms(kv[:, :, 0], g), pos)
    return jnp.stack([k, kv[:, :, 1]], axis=2)


def _mixer_ab(parts, start, pasts, weights):
    cmp_past, slc_past, win_past = pasts
    qk_gain, pe, w_phi, ln_g, ln_b, ws, bs = weights
    q, kvc, kvs, kvw, u, v, gate = parts
    B, T, _ = q.shape
    pos = start + jnp.arange(T, dtype=jnp.int32)
    q = _rms(q.reshape(B, T, NSA_HEADS, HEAD_DIM), qk_gain[0])
    q_rot = _rope(q, pos)
    kvc = kvc.reshape(B, T, 2, NSA_KV_HEADS, HEAD_DIM)
    kvs = _key_prep(kvs, qk_gain[2], pos)
    kvw = _key_prep(kvw, qk_gain[3], pos)
    kvw_all = jnp.concatenate([win_past, kvw], axis=1)
    g = jax.nn.sigmoid(gate).reshape(B, T, NSA_KV_HEADS, NSA_GROUP, 3)
    o_a = _nsa(q, q_rot, pos, jnp.concatenate([cmp_past, kvc], axis=1), jnp.concatenate([slc_past, kvs], axis=1),
               kvw_all, g, pe, w_phi, qk_gain[1])
    u = jax.nn.gelu(u)
    v = _layernorm(jax.nn.gelu(v), ln_g, ln_b)
    o_b = u * _chunk_mix(v, ws, bs)
    keep_w = min(NSA_WINDOW, kvw_all.shape[1])
    n_cur = (T - 1) % GM_CHUNK + 1
    return (o_a.reshape(B, T, NSA_Q_W), o_b), (kvc, kvs, kvw_all[:, kvw_all.shape[1] - keep_w:], v[:, T - n_cur:])


def _mixer_c(parts, start, pasts, qk_gain):
    B, T, _ = parts[0].shape
    pos = start + jnp.arange(T, dtype=jnp.int32)
    outs, lses, new_bufs = [], [], []
    for g, (window, dilation) in enumerate(DIL_CFG):
        q, k, v = parts[3 * g], parts[3 * g + 1], parts[3 * g + 2]
        q = _rope(_rms(q.reshape(B, T, DIL_HEADS, HEAD_DIM), qk_gain[0]), pos)
        k = _rope(_rms(k.reshape(B, T, DIL_KV_HEADS, HEAD_DIM), qk_gain[1]), pos)
        kv_new = jnp.stack([k, v.reshape(B, T, DIL_KV_HEADS, HEAD_DIM)], axis=2)
        kv_all = jnp.concatenate([pasts[g], kv_new], axis=1)
        o, lse = _dilated_attend(q.reshape(B, T, DIL_KV_HEADS, DIL_GROUP, HEAD_DIM), kv_all, window, dilation)
        outs.append(o)
        lses.append(lse)
        keep = min(window, kv_all.shape[1])
        new_bufs.append(kv_all[:, kv_all.shape[1] - keep:])
    alpha = jax.nn.softmax(jnp.stack(lses, axis=0), axis=0)
    o = jnp.sum(alpha[..., None].astype(outs[0].dtype) * jnp.stack(outs, axis=0), axis=0)
    return (o.reshape(B, T, C_OUT_W),), tuple(new_bufs)


def _adaln(c, w, b):
    mod = jax.nn.silu(c) @ w + b
    return [m[:, None, :] for m in jnp.split(mod, 6, axis=-1)]


def _expand_mod(m, t):
    B, _, D = m.shape
    return jnp.broadcast_to(m, (B, t, D)).reshape(1, B * t, D)


def _layer(groups, mixers, norm_m, norm_f, w_in_bf16, in_splits, w_out_parts, moe_w):
    w_router, b_router, w_up, b_up, w_down, b_down = moe_w
    xs, states, hs, lgs = [], [], [], []
    for (x, mods), mixer in zip(groups, mixers):
        sh_m, sc_m, g_m, sh_f, sc_f, g_f = mods
        parts = _mod_norm_proj(x, norm_m, sc_m, sh_m, w_in_bf16, in_splits)
        B, T, D = x.shape
        o_parts, state = mixer(parts)
        x = _proj_residual([o.reshape(B, T, -1) for o in o_parts], w_out_parts, x, g_m)
        h, lg = _mod_norm_router(x, norm_f, sc_f, sh_f, w_router, b_router)
        xs.append(x)
        states.append(state)
        hs.append(h.reshape(B * T, D))
        lgs.append(lg.reshape(B * T, N_EXPERTS))
    n_rows = [h.shape[0] for h in hs]
    moe = _moe(jnp.concatenate(hs, axis=0), jnp.concatenate(lgs, axis=0), w_up, b_up, w_down, b_down)
    outs = []
    off = 0
    for x, (_, mods), n in zip(xs, groups, n_rows):
        outs.append(x + mods[5] * moe[off:off + n].reshape(x.shape))
        off += n
    return outs, states


def _stack(states, j):
    return jnp.stack([s[j] for s in states], axis=0)


def kernel(x_prompt, x_sample, cache_cmp_kv, cache_slc_kv, state_win_kv, state_dil0_kv, state_dil1_kv,
           state_dil2_kv, page_table, c_prompt, c_sample, norm_mix, norm_ffn, w_ada, b_ada, w_in_ab, w_out_ab,
           nsa_qk_gain, nsa_pe, nsa_w_phi, gm_ln_g, gm_ln_b, gm_ws, gm_bs, w_in_c, w_out_c, dil_qk_gain,
           w_router, b_router, w_up, b_up, w_down, b_down):
    depth = norm_mix.shape[0]
    past_len = page_table.shape[1] * PAGE_SIZE
    Bs, Ts, D = x_sample.shape
    x_p = x_prompt
    x_s = x_sample.reshape(1, Bs * Ts, D)
    ab_p, ab_s, dil_p, dil_s = [], [], [], []
    for layer in range(depth):
        i = layer // 2
        moe_w = (w_router[layer], b_router[layer], w_up[layer], b_up[layer], w_down[layer], b_down[layer])
        mods_p = _adaln(c_prompt, w_ada[layer], b_ada[layer])
        mods_s = [_expand_mod(m, Ts) for m in _adaln(c_sample, w_ada[layer], b_ada[layer])]
        groups = [(x_p, mods_p), (x_s, mods_s)]
        if layer % 2 == 0:
            wts = (nsa_qk_gain[i], nsa_pe[i], nsa_w_phi[i], gm_ln_g[i], gm_ln_b[i], gm_ws[i], gm_bs[i])
            empty = _empty_kv(x_prompt, NSA_KV_HEADS)
            pasts_s = (_paged_past(cache_cmp_kv[i], page_table), _paged_past(cache_slc_kv[i], page_table),
                       state_win_kv[i])

            def mix_p(parts, wts=wts, empty=empty):
                return _mixer_ab(parts, 0, (empty, empty, empty), wts)

            def mix_s(parts, wts=wts, pasts_s=pasts_s):
                parts = [p.reshape(Bs, Ts, -1) for p in parts]
                return _mixer_ab(parts, past_len, pasts_s, wts)

            w_out = w_out_ab[i].astype(BF16)
            g0, g1 = AB_SPLITS[0], AB_SPLITS[0] + AB_SPLITS[1]
            w_in = w_in_ab[i].astype(BF16)
            w_in = jnp.concatenate([w_in[:, :g0], w_in[:, g1:], w_in[:, g0:g1]], axis=1)
            (x_p, x_s), (st_p, st_s) = _layer(
                groups, (mix_p, mix_s), norm_mix[layer], norm_ffn[layer], w_in, AB_SPLITS_GATE_LAST,
                (w_out[:NSA_Q_W], w_out[NSA_Q_W:]), moe_w)
            ab_p.append(st_p)
            ab_s.append(st_s)
        else:
            empty = _empty_kv(x_prompt, DIL_KV_HEADS)
            pasts_s = (state_dil0_kv[i], state_dil1_kv[i], state_dil2_kv[i])

            def mix_p(parts, empty=empty, gain=dil_qk_gain[i]):
                return _mixer_c(parts, 0, (empty, empty, empty), gain)

            def mix_s(parts, pasts_s=pasts_s, gain=dil_qk_gain[i]):
                parts = [p.reshape(Bs, Ts, -1) for p in parts]
                return _mixer_c(parts, past_len, pasts_s, gain)

            (x_p, x_s), (st_p, st_s) = _layer(
                groups, (mix_p, mix_s), norm_mix[layer], norm_ffn[layer], w_in_c[i].astype(BF16),
                (DIL_Q_W, DIL_KV_W, DIL_KV_W) * len(DIL_CFG), (w_out_c[i].astype(BF16),), moe_w)
            dil_p.append(st_p)
            dil_s.append(st_s)
    x_s = x_s.reshape(Bs, Ts, D)
    cmp_p, cmp_s = _stack(ab_p, 0), _stack(ab_s, 0)
    slc_p, slc_s = _stack(ab_p, 1), _stack(ab_s, 1)
    win_p, win_s = _stack(ab_p, 2), _stack(ab_s, 2)
    gmv_p, gmv_s = _stack(ab_p, 3), _stack(ab_s, 3)
    d0_p, d0_s = _stack(dil_p, 0), _stack(dil_s, 0)
    d1_p, d1_s = _stack(dil_p, 1), _stack(dil_s, 1)
    d2_p, d2_s = _stack(dil_p, 2), _stack(dil_s, 2)
    return (x_p, x_s, cmp_p, cmp_s, slc_p, slc_s, win_p, win_s, gmv_p, gmv_s, d0_p, d0_s, d1_p, d1_s, d2_p, d2_s)
```

```python
import functools
import math

import jax
import jax.numpy as jnp
from jax import lax
import numpy as np
from jax.experimental import pallas as pl
from jax.experimental.pallas import tpu as pltpu

F32 = jnp.float32
BF16 = jnp.bfloat16

D_MODEL = 1024
HEAD_DIM = 64
ROT_DIM = HEAD_DIM // 4
ROPE_THETA = 500000.0
NORM_EPS = 1e-6
NEG_INF = -1e30
Q_BLOCK = 128
SEL_Q_BLOCK = 32
PAGE_SIZE = 128

NSA_HEADS = D_MODEL // (2 * HEAD_DIM)
NSA_KV_HEADS = 2
NSA_GROUP = NSA_HEADS // NSA_KV_HEADS
NSA_BLOCK = 64
NSA_N_SEL = 16
NSA_WINDOW = 512
NSA_Q_W = NSA_HEADS * HEAD_DIM
NSA_KV_W = 2 * NSA_KV_HEADS * HEAD_DIM

GM_GROUPS = 8
GM_WIDTH = D_MODEL // 2
GM_GROUP_W = GM_WIDTH // GM_GROUPS
GM_CHUNK = 128

AB_SPLITS = (NSA_Q_W, 3 * NSA_HEADS, NSA_KV_W, NSA_KV_W, NSA_KV_W, GM_WIDTH, GM_WIDTH)
AB_SPLITS_GATE_LAST = (AB_SPLITS[0],) + AB_SPLITS[2:] + (AB_SPLITS[1],)

DIL_CFG = ((128, 1), (512, 4), (2048, 16))
DIL_HEADS = 8
DIL_KV_HEADS = 2
DIL_GROUP = DIL_HEADS // DIL_KV_HEADS
DIL_Q_W = DIL_HEADS * HEAD_DIM
DIL_KV_W = DIL_KV_HEADS * HEAD_DIM
C_OUT_W = DIL_Q_W

N_EXPERTS = 32
TOP_K = 4
D_FF = D_MODEL
SWIGLU_ALPHA = 1.702
SWIGLU_LIMIT = 7.0

VMEM_LIMIT_BYTES = 56 * 1024 * 1024
ROW_TILE = 512
MOE_TILE = 512


def _row_tile(t):
    return t if t <= ROW_TILE else ROW_TILE


def _mod_spec(mod, tm):
    if mod.shape[1] == 1:
        return pl.BlockSpec((1, 1, mod.shape[2]), lambda b, i: (b, 0, 0))
    return pl.BlockSpec((1, tm, mod.shape[2]), lambda b, i: (b, i, 0))


def _split_bf16(v):
    hi = v.astype(BF16)
    lo = (v - hi.astype(F32)).astype(BF16)
    return hi, lo


def _mod_norm(x, g, sc, sh):
    h = x * lax.rsqrt(jnp.mean(x * x, axis=-1, keepdims=True) + NORM_EPS) * g
    return h * (1.0 + sc) + sh


def _mod_norm_proj_kernel(x_ref, g_ref, sc_ref, sh_ref, w_ref, *o_refs, splits):
    h = _mod_norm(x_ref[0], g_ref[...], sc_ref[0], sh_ref[0])
    y = jnp.dot(h.astype(BF16), w_ref[...], preferred_element_type=F32)
    off = 0
    for o_ref, n in zip(o_refs, splits):
        o_ref[0] = y[:, off:off + n]
        off += n


def _mod_norm_proj(x, norm_g, scale, shift, w_bf16, splits):
    B, T, D = x.shape
    tm = _row_tile(T)
    N = w_bf16.shape[1]
    return pl.pallas_call(
        functools.partial(_mod_norm_proj_kernel, splits=splits),
        out_shape=[jax.ShapeDtypeStruct((B, T, n), F32) for n in splits],
        grid=(B, T // tm),
        in_specs=[
            pl.BlockSpec((1, tm, D), lambda b, i: (b, i, 0)),
            pl.BlockSpec((1, D), lambda b, i: (0, 0)),
            _mod_spec(scale, tm),
            _mod_spec(shift, tm),
            pl.BlockSpec((D, N), lambda b, i: (0, 0)),
        ],
        out_specs=[pl.BlockSpec((1, tm, n), lambda b, i: (b, i, 0)) for n in splits],
        compiler_params=pltpu.CompilerParams(
            dimension_semantics=("parallel", "parallel"), vmem_limit_bytes=VMEM_LIMIT_BYTES),
        name="mod_norm_proj",
    )(x, norm_g.reshape(1, D), scale, shift, w_bf16)


def _proj_residual_kernel(*refs, n_parts):
    a_refs = refs[:n_parts]
    w_refs = refs[n_parts:2 * n_parts]
    x_ref, g_ref, o_ref = refs[2 * n_parts:]
    y = None
    for a_ref, w_ref in zip(a_refs, w_refs):
        p = jnp.dot(a_ref[0].astype(BF16), w_ref[...], preferred_element_type=F32)
        y = p if y is None else y + p
    o_ref[0] = x_ref[0] + g_ref[0] * y


def _proj_residual(a_parts, w_parts_bf16, x, gate):
    B, T, D = x.shape
    tm = _row_tile(T)
    n_parts = len(a_parts)
    in_specs = [pl.BlockSpec((1, tm, a.shape[2]), lambda b, i: (b, i, 0)) for a in a_parts]
    in_specs += [pl.BlockSpec(w.shape, lambda b, i: (0, 0)) for w in w_parts_bf16]
    in_specs += [pl.BlockSpec((1, tm, D), lambda b, i: (b, i, 0)), _mod_spec(gate, tm)]
    return pl.pallas_call(
        functools.partial(_proj_residual_kernel, n_parts=n_parts),
        out_shape=jax.ShapeDtypeStruct((B, T, D), F32),
        grid=(B, T // tm),
        in_specs=in_specs,
        out_specs=pl.BlockSpec((1, tm, D), lambda b, i: (b, i, 0)),
        compiler_params=pltpu.CompilerParams(
            dimension_semantics=("parallel", "parallel"), vmem_limit_bytes=VMEM_LIMIT_BYTES),
        name="proj_residual",
    )(*a_parts, *w_parts_bf16, x, gate)


def _mod_norm_router_kernel(x_ref, g_ref, sc_ref, sh_ref, whi_ref, wlo_ref, b_ref, h_ref, l_ref):
    h = _mod_norm(x_ref[0], g_ref[...], sc_ref[0], sh_ref[0])
    h_hi, h_lo = _split_bf16(h)
    h_ref[0] = h_hi
    w_hi = whi_ref[...]
    logits = (jnp.dot(h_hi, w_hi, preferred_element_type=F32)
              + jnp.dot(h_lo, w_hi, preferred_element_type=F32)
              + jnp.dot(h_hi, wlo_ref[...], preferred_element_type=F32))
    l_ref[0] = logits + b_ref[...]


def _mod_norm_router(x, norm_g, scale, shift, w_router, b_router):
    B, T, D = x.shape
    tm = _row_tile(T)
    E = w_router.shape[1]
    w_hi, w_lo = _split_bf16(w_router)
    return pl.pallas_call(
        _mod_norm_router_kernel,
        out_shape=[jax.ShapeDtypeStruct((B, T, D), BF16), jax.ShapeDtypeStruct((B, T, E), F32)],
        grid=(B, T // tm),
        in_specs=[
            pl.BlockSpec((1, tm, D), lambda b, i: (b, i, 0)),
            pl.BlockSpec((1, D), lambda b, i: (0, 0)),
            _mod_spec(scale, tm),
            _mod_spec(shift, tm),
            pl.BlockSpec((D, E), lambda b, i: (0, 0)),
            pl.BlockSpec((D, E), lambda b, i: (0, 0)),
            pl.BlockSpec((1, E), lambda b, i: (0, 0)),
        ],
        out_specs=[pl.BlockSpec((1, tm, D), lambda b, i: (b, i, 0)),
                   pl.BlockSpec((1, tm, E), lambda b, i: (b, i, 0))],
        compiler_params=pltpu.CompilerParams(
            dimension_semantics=("parallel", "parallel"), vmem_limit_bytes=VMEM_LIMIT_BYTES),
        name="mod_norm_router",
    )(x, norm_g.reshape(1, D), scale, shift, w_hi, w_lo, b_router.reshape(1, E))


def _moe_ffn_kernel(te_ref, tx_ref, tv_ref, x_ref, wu_ref, bu_ref, wd_ref, bd_ref, g_ref, o_ref):
    i = pl.program_id(0)

    @pl.when(tv_ref[i] != 0)
    def _():
        hu = jnp.dot(x_ref[...], wu_ref[0].astype(BF16), preferred_element_type=F32) + bu_ref[0]
        glu = jnp.minimum(hu[:, :D_FF], SWIGLU_LIMIT)
        lin = jnp.clip(hu[:, D_FF:], -SWIGLU_LIMIT, SWIGLU_LIMIT)
        act = glu * jax.nn.sigmoid(SWIGLU_ALPHA * glu) * (lin + 1.0)
        y = jnp.dot(act.astype(BF16), wd_ref[0].astype(BF16), preferred_element_type=F32) + bd_ref[0]
        o_ref[...] = y * g_ref[...]

    @pl.when(tv_ref[i] == 0)
    def _():
        o_ref[...] = jnp.zeros_like(o_ref)


def _moe_ffn(tile_exp, tile_x, tile_valid, x_sorted, w_up, b_up, w_down, b_down, slot_gate):
    n_slots, D = x_sorted.shape
    n_tiles = n_slots // MOE_TILE
    E, _, F2 = w_up.shape
    grid_spec = pltpu.PrefetchScalarGridSpec(
        num_scalar_prefetch=3,
        grid=(n_tiles,),
        in_specs=[
            pl.BlockSpec((MOE_TILE, D), lambda i, te, tx, tv: (tx[i], 0)),
            pl.BlockSpec((1, D, F2), lambda i, te, tx, tv: (te[i], 0, 0)),
            pl.BlockSpec((1, 1, F2), lambda i, te, tx, tv: (te[i], 0, 0)),
            pl.BlockSpec((1, F2 // 2, D), lambda i, te, tx, tv: (te[i], 0, 0)),
            pl.BlockSpec((1, 1, D), lambda i, te, tx, tv: (te[i], 0, 0)),
            pl.BlockSpec((MOE_TILE, 1), lambda i, te, tx, tv: (tx[i], 0)),
        ],
        out_specs=pl.BlockSpec((MOE_TILE, D), lambda i, te, tx, tv: (i, 0)),
    )
    return pl.pallas_call(
        _moe_ffn_kernel,
        out_shape=jax.ShapeDtypeStruct((n_slots, D), F32),
        grid_spec=grid_spec,
        compiler_params=pltpu.CompilerParams(
            dimension_semantics=("arbitrary",), vmem_limit_bytes=VMEM_LIMIT_BYTES),
        name="moe_ffn",
    )(tile_exp, tile_x, tile_valid, x_sorted, w_up, b_up.reshape(E, 1, F2), w_down,
      b_down.reshape(E, 1, D), slot_gate.reshape(n_slots, 1))


def _moe(h_tok, logits, w_up, b_up, w_down, b_down):
    n_tok, D = h_tok.shape
    top_val, top_exp = lax.top_k(logits, TOP_K)
    gate = jax.nn.softmax(top_val, axis=-1)
    n_asg = n_tok * TOP_K
    flat_e = top_exp.reshape(n_asg)
    flat_tok = jnp.arange(n_asg, dtype=jnp.int32) // TOP_K
    order = jnp.argsort(flat_e)
    sorted_e = flat_e[order]
    counts = jnp.bincount(flat_e, length=N_EXPERTS)
    padded = (counts + MOE_TILE - 1) // MOE_TILE * MOE_TILE
    pad_end = jnp.cumsum(padded)
    pad_start = pad_end - padded
    grp_start = jnp.cumsum(counts) - counts
    dest = (pad_start[sorted_e] + jnp.arange(n_asg) - grp_start[sorted_e]).astype(jnp.int32)
    n_tiles = -(-(n_asg + N_EXPERTS * (MOE_TILE - 1)) // MOE_TILE)
    n_slots = n_tiles * MOE_TILE
    slot_tok = jnp.zeros((n_slots,), jnp.int32).at[dest].set(flat_tok[order])
    slot_gate = jnp.zeros((n_slots,), F32).at[dest].set(gate.reshape(n_asg)[order])
    slot_of_asg = jnp.zeros((n_asg,), jnp.int32).at[order].set(dest)
    tile_start = jnp.arange(n_tiles, dtype=jnp.int32) * MOE_TILE
    n_used = (pad_end[-1] // MOE_TILE).astype(jnp.int32)
    tile_valid = (jnp.arange(n_tiles, dtype=jnp.int32) < n_used).astype(jnp.int32)
    tile_x = jnp.minimum(jnp.arange(n_tiles, dtype=jnp.int32), n_used - 1)
    tile_exp = jnp.minimum(jnp.searchsorted(pad_end, tile_x * MOE_TILE, side='right'),
                           N_EXPERTS - 1).astype(jnp.int32)
    del tile_start
    x_sorted = h_tok[slot_tok]
    out = _moe_ffn(tile_exp, tile_x, tile_valid, x_sorted, w_up, b_up, w_down, b_down, slot_gate)
    return out[slot_of_asg].reshape(n_tok, TOP_K, D).sum(axis=1)


N_QHEADS = 8
LANES = 128
HALF = HEAD_DIM


def _nt_dot(a, b):
    return lax.dot_general(a, b, (((1,), (1,)), ((), ())), preferred_element_type=F32)


def _band_mask(qpos, kpos, window, dilation):
    delta = qpos - kpos
    valid = (delta >= 0) & (delta <= window)
    if dilation > 1:
        valid = valid & ((delta & (dilation - 1)) == 0)
    return valid


def _softmax_pv(s, valid, v, tq):
    n = s.shape[-1]
    s = jnp.where(valid[None], s.reshape(N_QHEADS, tq, n), NEG_INF)
    m = jnp.max(s, axis=-1, keepdims=True)
    p = jnp.exp(s - m)
    l = jnp.sum(p, axis=-1, keepdims=True)
    o = jnp.dot(p.reshape(N_QHEADS * tq, n).astype(BF16), v, preferred_element_type=F32)
    return o.reshape(N_QHEADS, tq, LANES) / l, m + jnp.log(l)


def _store_heads(o_ref, o, lse):
    if lse is None:
        o_ref[0] = o
        return
    lane = lax.broadcasted_iota(jnp.int32, (1, LANES), 1)
    for h in range(N_QHEADS):
        own = (lane < HALF) if h < N_QHEADS // 2 else (lane >= HALF)
        o_ref[0, h] = jnp.where(own, o[h], lse[h])


def _band_self_kernel(q_ref, k_ref, v_ref, o_ref, *, window, tq, span, want_lse):
    T = k_ref.shape[1]
    q0 = pl.program_id(1) * tq
    if span == T:
        start = 0
        k = k_ref[0]
        v = v_ref[0]
    else:
        start = pl.multiple_of(jnp.maximum(q0 - window, 0), LANES)
        k = k_ref[0, pl.ds(start, span), :]
        v = v_ref[0, pl.ds(start, span), :]
    q = q_ref[0].reshape(N_QHEADS * tq, LANES).astype(BF16)
    qpos = q0 + lax.broadcasted_iota(jnp.int32, (tq, 1), 0)
    kpos = start + lax.broadcasted_iota(jnp.int32, (1, span), 1)
    o, lse = _softmax_pv(_nt_dot(q, k), _band_mask(qpos, kpos, window, 1), v, tq)
    _store_heads(o_ref, o, lse if want_lse else None)


def _band_self(q, k, v, window, want_lse):
    B, _, T, _ = q.shape
    tq = min(T, Q_BLOCK)
    span = min(T, window + tq)
    return pl.pallas_call(
        functools.partial(_band_self_kernel, window=window, tq=tq, span=span, want_lse=want_lse),
        out_shape=jax.ShapeDtypeStruct((B, N_QHEADS, T, LANES), F32),
        grid=(B, T // tq),
        in_specs=[
            pl.BlockSpec((1, N_QHEADS, tq, LANES), lambda b, i: (b, 0, i, 0)),
            pl.BlockSpec((1, T, LANES), lambda b, i: (b, 0, 0)),
            pl.BlockSpec((1, T, LANES), lambda b, i: (b, 0, 0)),
        ],
        out_specs=pl.BlockSpec((1, N_QHEADS, tq, LANES), lambda b, i: (b, 0, i, 0)),
        compiler_params=pltpu.CompilerParams(
            dimension_semantics=("parallel", "parallel"), vmem_limit_bytes=VMEM_LIMIT_BYTES),
        name="band_self",
    )(q, k, v)


def _band_tail_kernel(q_ref, kv_ref, o_ref, *, window, dilation, tq, want_lse):
    L = kv_ref.shape[1]
    kv = kv_ref[0]
    k = kv[:, :LANES].astype(BF16)
    v = kv[:, LANES:].astype(BF16)
    q = q_ref[0].reshape(N_QHEADS * tq, LANES).astype(BF16)
    qpos = (L - tq) + lax.broadcasted_iota(jnp.int32, (tq, 1), 0)
    kpos = lax.broadcasted_iota(jnp.int32, (1, L), 1)
    o, lse = _softmax_pv(_nt_dot(q, k), _band_mask(qpos, kpos, window, dilation), v, tq)
    _store_heads(o_ref, o, lse if want_lse else None)


def _band_tail(q, kv_all, window, dilation, want_lse):
    B, _, T, _ = q.shape
    L = kv_all.shape[1]
    return pl.pallas_call(
        functools.partial(_band_tail_kernel, window=window, dilation=dilation, tq=T, want_lse=want_lse),
        out_shape=jax.ShapeDtypeStruct((B, N_QHEADS, T, LANES), F32),
        grid=(B,),
        in_specs=[
            pl.BlockSpec((1, N_QHEADS, T, LANES), lambda b: (b, 0, 0, 0)),
            pl.BlockSpec((1, L, 2 * LANES), lambda b: (b, 0, 0)),
        ],
        out_specs=pl.BlockSpec((1, N_QHEADS, T, LANES), lambda b: (b, 0, 0, 0)),
        compiler_params=pltpu.CompilerParams(
            dimension_semantics=("parallel",), vmem_limit_bytes=VMEM_LIMIT_BYTES),
        name="band_tail",
    )(q, kv_all)


def _pad_heads(q):
    z = jnp.zeros_like(q[:, :, :N_QHEADS // 2])
    lo = jnp.concatenate([q[:, :, :N_QHEADS // 2], z], axis=-1)
    hi = jnp.concatenate([z, q[:, :, N_QHEADS // 2:]], axis=-1)
    return jnp.concatenate([lo, hi], axis=2).transpose(0, 2, 1, 3)


def _unpad_heads(o):
    B, _, T, _ = o.shape
    x = o.transpose(0, 2, 1, 3).reshape(B, T, N_QHEADS, 2, HALF)
    lower = (jnp.arange(N_QHEADS) < N_QHEADS // 2)[None, None, :, None]
    own = jnp.where(lower, x[..., 0, :], x[..., 1, :])
    other = jnp.where(lower, x[..., 1, :], x[..., 0, :])
    return own, other


def _block_mean_kernel(x_ref, o_ref):
    rows = x_ref.shape[1]
    x = x_ref[0].reshape(rows // NSA_BLOCK, NSA_BLOCK, x_ref.shape[2])
    o_ref[0] = jnp.sum(x, axis=1) * (1.0 / NSA_BLOCK)


def _block_mean(kvc):
    B, T, W = kvc.shape
    tm = _row_tile(T)
    return pl.pallas_call(
        _block_mean_kernel,
        out_shape=jax.ShapeDtypeStruct((B, T // NSA_BLOCK, W), F32),
        grid=(B, T // tm),
        in_specs=[pl.BlockSpec((1, tm, W), lambda b, i: (b, i, 0))],
        out_specs=pl.BlockSpec((1, tm // NSA_BLOCK, W), lambda b, i: (b, i, 0)),
        compiler_params=pltpu.CompilerParams(dimension_semantics=("parallel", "parallel")),
        name="block_mean",
    )(kvc)


PAGES_PER_STEP = 8


def _page_specs(n_cols):
    return [pl.BlockSpec((1, PAGE_SIZE, n_cols), functools.partial(
        lambda b, s, pt, pg: (pt[b, s * PAGES_PER_STEP + pg], 0, 0), pg=pg)) for pg in range(PAGES_PER_STEP)]


def _page_block_mean_kernel(pt_ref, *refs):
    page_refs, o_ref = refs[:-1], refs[-1]
    per_page = PAGE_SIZE // NSA_BLOCK
    for pg, page_ref in enumerate(page_refs):
        x = page_ref[0].reshape(per_page, NSA_BLOCK, page_ref.shape[2])
        o_ref[0, pg * per_page:(pg + 1) * per_page, :] = jnp.sum(x, axis=1) * (1.0 / NSA_BLOCK)


def _page_block_mean(cache, page_table):
    B, n_pages = page_table.shape
    W = cache.shape[2]
    per_step = PAGES_PER_STEP * PAGE_SIZE // NSA_BLOCK
    grid_spec = pltpu.PrefetchScalarGridSpec(
        num_scalar_prefetch=1,
        grid=(B, n_pages // PAGES_PER_STEP),
        in_specs=_page_specs(W),
        out_specs=pl.BlockSpec((1, per_step, W), lambda b, s, pt: (b, s, 0)),
    )
    return pl.pallas_call(
        _page_block_mean_kernel,
        out_shape=jax.ShapeDtypeStruct((B, n_pages * PAGE_SIZE // NSA_BLOCK, W), F32),
        grid_spec=grid_spec,
        compiler_params=pltpu.CompilerParams(dimension_semantics=("parallel", "arbitrary")),
        name="page_block_mean",
    )(page_table, *([cache] * PAGES_PER_STEP))


def _head_mean_sq(x, ones_bd):
    hi, lo = _split_bf16(x * x)
    return jnp.dot(hi, ones_bd, preferred_element_type=F32) + jnp.dot(lo, ones_bd, preferred_element_type=F32)


def _cmp_finish_kernel(mean_ref, pe_ref, wk_ref, wv_ref, gain_ref, bd_ref, kc_ref, vc_ref):
    s = mean_ref[0] + pe_ref[...]
    sk = jnp.dot(s[:, :LANES].astype(BF16), wk_ref[...], preferred_element_type=F32)
    vc_ref[0] = jnp.dot(s[:, LANES:].astype(BF16), wv_ref[...], preferred_element_type=F32)
    kc_ref[0] = sk * lax.rsqrt(_head_mean_sq(sk, bd_ref[...]) + NORM_EPS) * gain_ref[...]


def _block_diag2(w):
    z = jnp.zeros_like(w)
    return jnp.concatenate([jnp.concatenate([w, z], axis=1), jnp.concatenate([z, w], axis=1)], axis=0)


def _cmp_finish(mean, pe, w_phi, kc_gain):
    B, n_cb, W = mean.shape
    pe_mean = jnp.mean(pe, axis=0)
    pe_row = jnp.concatenate([pe_mean[0], pe_mean[0], pe_mean[1], pe_mean[1]])[None, :]
    wk = _block_diag2(w_phi[0]).astype(BF16)
    wv = _block_diag2(w_phi[1]).astype(BF16)
    bd = _block_diag2(jnp.full((HALF, HALF), 1.0 / HALF, F32)).astype(BF16)
    gain = jnp.concatenate([kc_gain, kc_gain])[None, :]
    full = lambda shape: pl.BlockSpec(shape, lambda b: (0,) * len(shape))
    return pl.pallas_call(
        _cmp_finish_kernel,
        out_shape=[jax.ShapeDtypeStruct((B, n_cb, LANES), F32)] * 2,
        grid=(B,),
        in_specs=[pl.BlockSpec((1, n_cb, W), lambda b: (b, 0, 0)), full((1, W)), full((LANES, LANES)),
                  full((LANES, LANES)), full((1, LANES)), full((LANES, LANES))],
        out_specs=[pl.BlockSpec((1, n_cb, LANES), lambda b: (b, 0, 0))] * 2,
        compiler_params=pltpu.CompilerParams(dimension_semantics=("parallel",)),
        name="cmp_finish",
    )(mean, pe_row, wk, wv, gain, bd)


def _cmp_select_kernel(q_ref, kc_ref, vc_ref, ocmp_ref, sel_ref, *, tq, pos0):
    n_cb = kc_ref.shape[1]
    q0 = pl.program_id(1) * tq
    qh, ql = _split_bf16(q_ref[0].reshape(N_QHEADS * tq, LANES))
    kh, kl = _split_bf16(kc_ref[0])
    s = (_nt_dot(qh, kh) + _nt_dot(ql, kh) + _nt_dot(qh, kl)) * (HEAD_DIM ** -0.5)
    cur = (pos0 + q0 + lax.broadcasted_iota(jnp.int32, (tq, 1), 0)) // NSA_BLOCK
    blk = lax.broadcasted_iota(jnp.int32, (1, n_cb), 1)
    vis = blk < cur
    s = jnp.where(vis[None], s.reshape(N_QHEADS, tq, n_cb), NEG_INF)
    p = jnp.exp(s - jnp.max(s, axis=-1, keepdims=True))
    p = jnp.where(vis[None], p / jnp.sum(p, axis=-1, keepdims=True), 0.0)
    o = jnp.dot(p.reshape(N_QHEADS * tq, n_cb).astype(BF16), vc_ref[0].astype(BF16), preferred_element_type=F32)
    ocmp_ref[0] = o.reshape(N_QHEADS, tq, LANES)
    for kvh in range(NSA_KV_HEADS):
        g0 = kvh * NSA_GROUP
        imp = jnp.where(vis, p[g0] + p[g0 + 1] + p[g0 + 2] + p[g0 + 3], -1.0)
        rank = jnp.zeros((tq, n_cb), jnp.int32)
        for i in range(n_cb):
            col = imp[:, i:i + 1]
            wins_tie = jnp.where(blk > i, 1, 0)
            rank = rank + jnp.where(col > imp, 1, jnp.where(col == imp, wins_tie, 0))
        chosen = jnp.where(vis, jnp.where(rank < NSA_N_SEL - 1, 1.0, 0.0), jnp.where(blk == cur, 1.0, 0.0))
        sel_ref[0, kvh] = chosen.astype(BF16)


def _cmp_select(qn, kc, vc, pos0):
    B, _, T, _ = qn.shape
    n_cb = kc.shape[1]
    tq = min(T, Q_BLOCK)
    return pl.pallas_call(
        functools.partial(_cmp_select_kernel, tq=tq, pos0=pos0),
        out_shape=[jax.ShapeDtypeStruct((B, N_QHEADS, T, LANES), F32),
                   jax.ShapeDtypeStruct((B, NSA_KV_HEADS, T, n_cb), BF16)],
        grid=(B, T // tq),
        in_specs=[
            pl.BlockSpec((1, N_QHEADS, tq, LANES), lambda b, i: (b, 0, i, 0)),
            pl.BlockSpec((1, n_cb, LANES), lambda b, i: (b, 0, 0)),
            pl.BlockSpec((1, n_cb, LANES), lambda b, i: (b, 0, 0)),
        ],
        out_specs=[pl.BlockSpec((1, N_QHEADS, tq, LANES), lambda b, i: (b, 0, i, 0)),
                   pl.BlockSpec((1, NSA_KV_HEADS, tq, n_cb), lambda b, i: (b, 0, i, 0))],
        compiler_params=pltpu.CompilerParams(
            dimension_semantics=("parallel", "parallel"), vmem_limit_bytes=VMEM_LIMIT_BYTES),
        name="cmp_select",
    )(qn, kc, vc)


def _selection_bias(sel, first_key, n_keys, causal):
    n_cb = sel.shape[-1]
    key_blk = (first_key + lax.broadcasted_iota(jnp.int32, (1, n_keys), 1)) // NSA_BLOCK
    expand = jnp.where(lax.broadcasted_iota(jnp.int32, (n_cb, 1), 0) == key_blk, 1.0, 0.0).astype(BF16)
    out = []
    for kvh in range(NSA_KV_HEADS):
        picked = jnp.dot(sel[kvh], expand, preferred_element_type=F32)
        bias = (picked - 1.0) * 1e30
        if causal is not None:
            bias = jnp.where(causal, bias, NEG_INF)
        out.append(jnp.broadcast_to(bias[None], (NSA_GROUP,) + bias.shape))
    return jnp.concatenate(out, axis=0)


def _online_softmax_step(s, v, m_ref, l_ref, acc_ref, tq):
    n = s.shape[-1]
    m_old = m_ref[...]
    m_new = jnp.maximum(m_old, jnp.max(s, axis=-1, keepdims=True))
    alpha = jnp.exp(m_old - m_new)
    p = jnp.exp(s - m_new[..., :1])
    l_ref[...] = alpha * l_ref[...] + jnp.sum(p, axis=-1, keepdims=True)
    pv = jnp.dot(p.reshape(N_QHEADS * tq, n).astype(BF16), v, preferred_element_type=F32)
    acc_ref[...] = alpha * acc_ref[...] + pv.reshape(N_QHEADS, tq, LANES)
    m_ref[...] = m_new


def _init_softmax_state(m_ref, l_ref, acc_ref):
    m_ref[...] = jnp.full(m_ref.shape, NEG_INF, F32)
    l_ref[...] = jnp.zeros(l_ref.shape, F32)
    acc_ref[...] = jnp.zeros(acc_ref.shape, F32)


def _gated_heads(gate_pre, o_cmp, o_slc, o_win):
    g = jax.nn.sigmoid(gate_pre)
    lane = lax.broadcasted_iota(jnp.int32, (1, LANES), 1)
    pairs = []
    for c in range(N_QHEADS // 2):
        mixed = []
        for h in (2 * c, 2 * c + 1):
            mixed.append(g[:, 3 * h:3 * h + 1] * o_cmp[h] + g[:, 3 * h + 1:3 * h + 2] * o_slc[h]
                         + g[:, 3 * h + 2:3 * h + 3] * o_win[h])
        if 2 * c < N_QHEADS // 2:
            pairs.append(jnp.where(lane < HALF, mixed[0], pltpu.roll(mixed[1], HALF, 1)))
        else:
            pairs.append(jnp.where(lane < HALF, pltpu.roll(mixed[0], HALF, 1), mixed[1]))
    return jnp.concatenate(pairs, axis=-1)


SLC_KEY_TILE = 512


def _nsa_prompt_kernel(q_ref, ks_ref, vs_ref, kw_ref, vw_ref, sel_ref, ocmp_ref, g_ref, o_ref,
                       m_ref, l_ref, acc_ref, *, tq, tk, span):
    T = ks_ref.shape[1]
    q0 = pl.program_id(1) * tq
    q = q_ref[0].reshape(N_QHEADS * tq, LANES)
    qpos = q0 + lax.broadcasted_iota(jnp.int32, (tq, 1), 0)
    sel = sel_ref[0]
    _init_softmax_state(m_ref, l_ref, acc_ref)

    def key_tile(j, carry):
        k0 = pl.multiple_of(j * tk, tk)
        kpos = k0 + lax.broadcasted_iota(jnp.int32, (1, tk), 1)
        s = _nt_dot(q, ks_ref[0, pl.ds(k0, tk), :]).reshape(N_QHEADS, tq, tk)
        s = s + _selection_bias(sel, k0, tk, kpos <= qpos)
        _online_softmax_step(s, vs_ref[0, pl.ds(k0, tk), :], m_ref, l_ref, acc_ref, tq)
        return carry

    lax.fori_loop(0, (q0 + tq + tk - 1) // tk, key_tile, 0)
    o_slc = acc_ref[...] / l_ref[...]

    if span == T:
        start = 0
        kw, vw = kw_ref[0], vw_ref[0]
    else:
        start = pl.multiple_of(jnp.maximum(q0 - NSA_WINDOW, 0), LANES)
        kw, vw = kw_ref[0, pl.ds(start, span), :], vw_ref[0, pl.ds(start, span), :]
    kpos = start + lax.broadcasted_iota(jnp.int32, (1, span), 1)
    o_win, _ = _softmax_pv(_nt_dot(q, kw), _band_mask(qpos, kpos, NSA_WINDOW, 1), vw, tq)
    o_ref[0] = _gated_heads(g_ref[0], ocmp_ref[0], o_slc, o_win)


def _nsa_prompt(qr, ks, vs, kw, vw, sel, ocmp, gate_pre):
    B, _, T, _ = qr.shape
    n_cb = sel.shape[-1]
    tq = min(T, Q_BLOCK)
    tk = min(T, SLC_KEY_TILE)
    span = min(T, NSA_WINDOW + tq)
    seq = pl.BlockSpec((1, T, LANES), lambda b, i: (b, 0, 0))
    heads = pl.BlockSpec((1, N_QHEADS, tq, LANES), lambda b, i: (b, 0, i, 0))
    stat = pltpu.VMEM((N_QHEADS, tq, LANES), F32)
    return pl.pallas_call(
        functools.partial(_nsa_prompt_kernel, tq=tq, tk=tk, span=span),
        out_shape=jax.ShapeDtypeStruct((B, T, NSA_Q_W), F32),
        grid=(B, T // tq),
        in_specs=[heads, seq, seq, seq, seq,
                  pl.BlockSpec((1, NSA_KV_HEADS, tq, n_cb), lambda b, i: (b, 0, i, 0)),
                  heads,
                  pl.BlockSpec((1, tq, gate_pre.shape[2]), lambda b, i: (b, i, 0))],
        out_specs=pl.BlockSpec((1, tq, NSA_Q_W), lambda b, i: (b, i, 0)),
        scratch_shapes=[stat, stat, stat],
        compiler_params=pltpu.CompilerParams(
            dimension_semantics=("parallel", "arbitrary"), vmem_limit_bytes=VMEM_LIMIT_BYTES),
        name="nsa_prompt",
    )(qr, ks, vs, kw, vw, sel, ocmp, gate_pre)


def _nsa_paged_kernel(pt_ref, q_ref, *refs, tq, pos0):
    page_refs = refs[:PAGES_PER_STEP]
    new_ref, sel_ref, ocmp_ref, owin_ref, g_ref, o_ref, m_ref, l_ref, acc_ref = refs[PAGES_PER_STEP:]
    step = pl.program_id(1)
    q = q_ref[0].reshape(N_QHEADS * tq, LANES).astype(BF16)
    sel = sel_ref[0]

    @pl.when(step == 0)
    def _():
        _init_softmax_state(m_ref, l_ref, acc_ref)

    def attend(kv, first_key, causal):
        s = _nt_dot(q, kv[:, :LANES].astype(BF16)).reshape(N_QHEADS, tq, PAGE_SIZE)
        if causal is None:
            s = s + _selection_bias(sel, first_key, PAGE_SIZE, None)
        else:
            s = jnp.where(causal[None], s, NEG_INF)
        _online_softmax_step(s, kv[:, LANES:].astype(BF16), m_ref, l_ref, acc_ref, tq)

    for pg, page_ref in enumerate(page_refs):
        attend(page_ref[0], (step * PAGES_PER_STEP + pg) * PAGE_SIZE, None)

    @pl.when(step == pl.num_programs(1) - 1)
    def _():
        qpos = lax.broadcasted_iota(jnp.int32, (tq, 1), 0)
        kpos = lax.broadcasted_iota(jnp.int32, (1, PAGE_SIZE), 1)
        attend(new_ref[0], pos0, kpos <= qpos)
        o_ref[0] = _gated_heads(g_ref[0], ocmp_ref[0], acc_ref[...] / l_ref[...], owin_ref[0])


def _nsa_paged(page_table, qr, cache, new_tile, sel, ocmp, owin, gate_pre, pos0):
    B, _, T, _ = qr.shape
    n_pages = page_table.shape[1]
    n_cb = sel.shape[-1]
    W = cache.shape[2]
    heads = pl.BlockSpec((1, N_QHEADS, T, LANES), lambda b, s, pt: (b, 0, 0, 0))
    stat = pltpu.VMEM((N_QHEADS, T, LANES), F32)
    grid_spec = pltpu.PrefetchScalarGridSpec(
        num_scalar_prefetch=1,
        grid=(B, n_pages // PAGES_PER_STEP),
        in_specs=[heads] + _page_specs(W) + [
            pl.BlockSpec((1, PAGE_SIZE, W), lambda b, s, pt: (b, 0, 0)),
            pl.BlockSpec((1, NSA_KV_HEADS, T, n_cb), lambda b, s, pt: (b, 0, 0, 0)),
            heads, heads,
            pl.BlockSpec((1, T, gate_pre.shape[2]), lambda b, s, pt: (b, 0, 0))],
        out_specs=pl.BlockSpec((1, T, NSA_Q_W), lambda b, s, pt: (b, 0, 0)),
        scratch_shapes=[stat, stat, stat],
    )
    return pl.pallas_call(
        functools.partial(_nsa_paged_kernel, tq=T, pos0=pos0),
        out_shape=jax.ShapeDtypeStruct((B, T, NSA_Q_W), F32),
        grid_spec=grid_spec,
        compiler_params=pltpu.CompilerParams(
            dimension_semantics=("parallel", "arbitrary"), vmem_limit_bytes=VMEM_LIMIT_BYTES),
        name="nsa_paged",
    )(page_table, qr, *([cache] * PAGES_PER_STEP), new_tile, sel, ocmp, owin, gate_pre)


def _rms(x, g):
    xf = x.astype(F32)
    y = xf * lax.rsqrt(jnp.mean(xf * xf, axis=-1, keepdims=True) + NORM_EPS)
    return (y * g.astype(F32)).astype(x.dtype)


def _layernorm(x, g, b):
    xf = x.astype(F32)
    mu = jnp.mean(xf, axis=-1, keepdims=True)
    var = jnp.mean(jnp.square(xf - mu), axis=-1, keepdims=True)
    y = (xf - mu) * lax.rsqrt(var + NORM_EPS)
    return (y * g.astype(F32) + b.astype(F32)).astype(x.dtype)


def _rope(x, pos):
    half = ROT_DIM // 2
    inv = jnp.exp(-math.log(ROPE_THETA) * jnp.arange(half, dtype=F32) * (2.0 / ROT_DIM))
    ang = pos.astype(F32)[:, None] * inv[None, :]
    cos = jnp.cos(ang)[:, None, :]
    sin = jnp.sin(ang)[:, None, :]
    xr = x[..., :ROT_DIM].astype(F32)
    x1, x2 = xr[..., :half], xr[..., half:]
    rot = jnp.concatenate([x1 * cos - x2 * sin, x2 * cos + x1 * sin], axis=-1).astype(x.dtype)
    return jnp.concatenate([rot, x[..., ROT_DIM:]], axis=-1)


def _chunk_mix(v, ws, bs):
    B, T, W = v.shape
    tp = -(-T // GM_CHUNK) * GM_CHUNK
    vp = jnp.pad(v, ((0, 0), (0, tp - T), (0, 0))).reshape(B, tp // GM_CHUNK, GM_CHUNK, GM_GROUPS, GM_GROUP_W)
    wm = jnp.where(jnp.tril(jnp.ones((GM_CHUNK, GM_CHUNK), bool)), ws, 0.0).astype(v.dtype)
    s = jnp.einsum('gij,bnjgc->bnigc', wm, vp) + jnp.transpose(bs)[None, None, :, :, None]
    return s.reshape(B, tp, W)[:, :T]


def _key_prep(kv, g, pos):
    B, T = kv.shape[:2]
    kv = kv.reshape(B, T, 2, NSA_KV_HEADS, HEAD_DIM)
    k = _rope(_rms(kv[:, :, 0], g), pos)
    return jnp.stack([k, kv[:, :, 1]], axis=2)


def _kv5(x):
    return x.reshape(x.shape[:2] + (2, NSA_KV_HEADS, HEAD_DIM))


def _mixer_ab(parts, start, pasts, weights, page_table):
    qk_gain, pe, w_phi, ln_g, ln_b, ws, bs = weights
    q, kvc, kvs, kvw, u, v, gate = parts
    B, T, _ = q.shape
    pos = start + jnp.arange(T, dtype=jnp.int32)
    scale = HEAD_DIM ** -0.5
    q = _rms(q.reshape(B, T, NSA_HEADS, HEAD_DIM), qk_gain[0])
    qn = _pad_heads(q)
    qr = _pad_heads(_rope(q, pos) * scale)
    kvs = _key_prep(kvs, qk_gain[2], pos).reshape(B, T, NSA_KV_W)
    kvw = _key_prep(kvw, qk_gain[3], pos).reshape(B, T, NSA_KV_W)
    if pasts is None:
        kc, vc = _cmp_finish(_block_mean(kvc), pe, w_phi, qk_gain[1])
        ocmp, sel = _cmp_select(qn, kc, vc, start)
        half = lambda x, i: x[..., i * LANES:(i + 1) * LANES].astype(BF16)
        o_a = _nsa_prompt(qr.astype(BF16), half(kvs, 0), half(kvs, 1), half(kvw, 0), half(kvw, 1), sel, ocmp, gate)
        kvw_all = kvw
    else:
        cache_cmp, cache_slc, win_past = pasts
        kc, vc = _cmp_finish(_page_block_mean(cache_cmp, page_table), pe, w_phi, qk_gain[1])
        ocmp, sel = _cmp_select(qn, kc, vc, start)
        kvw_all = jnp.concatenate([win_past, kvw], axis=1)
        owin = _band_tail(qr, kvw_all, NSA_WINDOW, 1, False)
        new_tile = jnp.pad(kvs, ((0, 0), (0, PAGE_SIZE - T), (0, 0)))
        o_a = _nsa_paged(page_table, qr, cache_slc, new_tile, sel, ocmp, owin, gate, start)
    u = jax.nn.gelu(u)
    v = _layernorm(jax.nn.gelu(v), ln_g, ln_b)
    o_b = u * _chunk_mix(v, ws, bs)
    keep_w = min(NSA_WINDOW, kvw_all.shape[1])
    n_cur = (T - 1) % GM_CHUNK + 1
    return (o_a, o_b), (_kv5(kvc), _kv5(kvs), _kv5(kvw_all[:, kvw_all.shape[1] - keep_w:]), v[:, T - n_cur:])


def _residue_major(x, d):
    B, T = x.shape[:2]
    x = x.reshape((B, T // d, d) + x.shape[2:])
    return jnp.swapaxes(x, 1, 2).reshape((B * d, T // d) + x.shape[3:])


def _time_major(x, d):
    Bd, Tm = x.shape[:2]
    x = x.reshape((Bd // d, d, Tm) + x.shape[2:])
    return jnp.swapaxes(x, 1, 2).reshape((Bd // d, Tm * d) + x.shape[3:])


def _mixer_c(parts, start, pasts, qk_gain):
    B, T, _ = parts[0].shape
    pos = start + jnp.arange(T, dtype=jnp.int32)
    scale = HEAD_DIM ** -0.5
    outs, lses, new_bufs = [], [], []
    for g, (window, dilation) in enumerate(DIL_CFG):
        q, k, v = parts[3 * g], parts[3 * g + 1], parts[3 * g + 2]
        q = _rope(_rms(q.reshape(B, T, DIL_HEADS, HEAD_DIM), qk_gain[0]), pos) * scale
        k = _rope(_rms(k.reshape(B, T, DIL_KV_HEADS, HEAD_DIM), qk_gain[1]), pos)
        kv_new = jnp.concatenate([k.reshape(B, T, DIL_KV_W), v], axis=-1)
        if pasts is None:
            kvd = _residue_major(kv_new, dilation).astype(BF16)
            qd = _pad_heads(_residue_major(q, dilation)).astype(BF16)
            o = _band_self(qd, kvd[..., :LANES], kvd[..., LANES:], window // dilation, True)
            own, other = _unpad_heads(o)
            own, other = _time_major(own, dilation), _time_major(other, dilation)
            kv_all = kv_new
        else:
            kv_all = jnp.concatenate([pasts[g], kv_new], axis=1)
            own, other = _unpad_heads(_band_tail(_pad_heads(q), kv_all, window, dilation, True))
        outs.append(own)
        lses.append(other[..., 0])
        keep = min(window, kv_all.shape[1])
        new_bufs.append(kv_all[:, kv_all.shape[1] - keep:].reshape(B, keep, 2, DIL_KV_HEADS, HEAD_DIM))
    alpha = jax.nn.softmax(jnp.stack(lses, axis=0), axis=0)
    o = jnp.sum(alpha[..., None] * jnp.stack(outs, axis=0), axis=0)
    return (o.reshape(B, T, C_OUT_W),), tuple(new_bufs)


def _adaln(c, w, b):
    mod = jax.nn.silu(c) @ w + b
    return [m[:, None, :] for m in jnp.split(mod, 6, axis=-1)]


def _expand_mod(m, t):
    B, _, D = m.shape
    return jnp.broadcast_to(m, (B, t, D)).reshape(1, B * t, D)


def _layer(groups, mixers, norm_m, norm_f, w_in_bf16, in_splits, w_out_parts, moe_w):
    w_router, b_router, w_up, b_up, w_down, b_down = moe_w
    xs, states, hs, lgs = [], [], [], []
    for (x, mods), mixer in zip(groups, mixers):
        sh_m, sc_m, g_m, sh_f, sc_f, g_f = mods
        parts = _mod_norm_proj(x, norm_m, sc_m, sh_m, w_in_bf16, in_splits)
        B, T, D = x.shape
        o_parts, state = mixer(parts)
        x = _proj_residual([o.reshape(B, T, -1) for o in o_parts], w_out_parts, x, g_m)
        h, lg = _mod_norm_router(x, norm_f, sc_f, sh_f, w_router, b_router)
        xs.append(x)
        states.append(state)
        hs.append(h.reshape(B * T, D))
        lgs.append(lg.reshape(B * T, N_EXPERTS))
    n_rows = [h.shape[0] for h in hs]
    moe = _moe(jnp.concatenate(hs, axis=0), jnp.concatenate(lgs, axis=0), w_up, b_up, w_down, b_down)
    outs = []
    off = 0
    for x, (_, mods), n in zip(xs, groups, n_rows):
        outs.append(x + mods[5] * moe[off:off + n].reshape(x.shape))
        off += n
    return outs, states


def _stack(states, j):
    return jnp.stack([s[j] for s in states], axis=0)


def kernel(x_prompt, x_sample, cache_cmp_kv, cache_slc_kv, state_win_kv, state_dil0_kv, state_dil1_kv,
           state_dil2_kv, page_table, c_prompt, c_sample, norm_mix, norm_ffn, w_ada, b_ada, w_in_ab, w_out_ab,
           nsa_qk_gain, nsa_pe, nsa_w_phi, gm_ln_g, gm_ln_b, gm_ws, gm_bs, w_in_c, w_out_c, dil_qk_gain,
           w_router, b_router, w_up, b_up, w_down, b_down):
    depth = norm_mix.shape[0]
    past_len = page_table.shape[1] * PAGE_SIZE
    Bs, Ts, D = x_sample.shape
    x_p = x_prompt
    x_s = x_sample.reshape(1, Bs * Ts, D)
    ab_p, ab_s, dil_p, dil_s = [], [], [], []
    for layer in range(depth):
        i = layer // 2
        moe_w = (w_router[layer], b_router[layer], w_up[layer], b_up[layer], w_down[layer], b_down[layer])
        mods_p = _adaln(c_prompt, w_ada[layer], b_ada[layer])
        mods_s = [_expand_mod(m, Ts) for m in _adaln(c_sample, w_ada[layer], b_ada[layer])]
        groups = [(x_p, mods_p), (x_s, mods_s)]
        if layer % 2 == 0:
            wts = (nsa_qk_gain[i], nsa_pe[i], nsa_w_phi[i], gm_ln_g[i], gm_ln_b[i], gm_ws[i], gm_bs[i])
            flat_kv = lambda a: a.reshape(a.shape[:2] + (NSA_KV_W,))
            pasts_s = (flat_kv(cache_cmp_kv[i]), flat_kv(cache_slc_kv[i]), flat_kv(state_win_kv[i]))

            def mix_p(parts, wts=wts):
                return _mixer_ab(parts, 0, None, wts, None)

            def mix_s(parts, wts=wts, pasts_s=pasts_s):
                parts = [p.reshape(Bs, Ts, -1) for p in parts]
                return _mixer_ab(parts, past_len, pasts_s, wts, page_table)

            w_out = w_out_ab[i].astype(BF16)
            g0, g1 = AB_SPLITS[0], AB_SPLITS[0] + AB_SPLITS[1]
            w_in = w_in_ab[i].astype(BF16)
            w_in = jnp.concatenate([w_in[:, :g0], w_in[:, g1:], w_in[:, g0:g1]], axis=1)
            (x_p, x_s), (st_p, st_s) = _layer(
                groups, (mix_p, mix_s), norm_mix[layer], norm_ffn[layer], w_in, AB_SPLITS_GATE_LAST,
                (w_out[:NSA_Q_W], w_out[NSA_Q_W:]), moe_w)
            ab_p.append(st_p)
            ab_s.append(st_s)
        else:
            flat_kv = lambda a: a.reshape(a.shape[:2] + (2 * DIL_KV_W,))
            pasts_s = (flat_kv(state_dil0_kv[i]), flat_kv(state_dil1_kv[i]), flat_kv(state_dil2_kv[i]))

            def mix_p(parts, gain=dil_qk_gain[i]):
                return _mixer_c(parts, 0, None, gain)

            def mix_s(parts, pasts_s=pasts_s, gain=dil_qk_gain[i]):
                parts = [p.reshape(Bs, Ts, -1) for p in parts]
                return _mixer_c(parts, past_len, pasts_s, gain)

            (x_p, x_s), (st_p, st_s) = _layer(
                groups, (mix_p, mix_s), norm_mix[layer], norm_ffn[layer], w_in_c[i].astype(BF16),
                (DIL_Q_W, DIL_KV_W, DIL_KV_W) * len(DIL_CFG), (w_out_c[i].astype(BF16),), moe_w)
            dil_p.append(st_p)
            dil_s.append(st_s)
    x_s = x_s.reshape(Bs, Ts, D)
    cmp_p, cmp_s = _stack(ab_p, 0), _stack(ab_s, 0)
    slc_p, slc_s = _stack(ab_p, 1), _stack(ab_s, 1)
    win_p, win_s = _stack(ab_p, 2), _stack(ab_s, 2)
    gmv_p, gmv_s = _stack(ab_p, 3), _stack(ab_s, 3)
    d0_p, d0_s = _stack(dil_p, 0), _stack(dil_s, 0)
    d1_p, d1_s = _stack(dil_p, 1), _stack(dil_s, 1)
    d2_p, d2_s = _stack(dil_p, 2), _stack(dil_s, 2)
    return (x_p, x_s, cmp_p, cmp_s, slc_p, slc_s, win_p, win_s, gmv_p, gmv_s, d0_p, d0_s, d1_p, d1_s, d2_p, d2_s)
```

```python
import functools
import math

import jax
import jax.numpy as jnp
from jax import lax
import numpy as np
from jax.experimental import pallas as pl
from jax.experimental.pallas import tpu as pltpu

F32 = jnp.float32
BF16 = jnp.bfloat16

D_MODEL = 1024
HEAD_DIM = 64
ROT_DIM = HEAD_DIM // 4
ROPE_THETA = 500000.0
NORM_EPS = 1e-6
NEG_INF = -1e30
Q_BLOCK = 128
SEL_Q_BLOCK = 32
PAGE_SIZE = 128

NSA_HEADS = D_MODEL // (2 * HEAD_DIM)
NSA_KV_HEADS = 2
NSA_GROUP = NSA_HEADS // NSA_KV_HEADS
NSA_BLOCK = 64
NSA_N_SEL = 16
NSA_WINDOW = 512
NSA_Q_W = NSA_HEADS * HEAD_DIM
NSA_KV_W = 2 * NSA_KV_HEADS * HEAD_DIM

GM_GROUPS = 8
GM_WIDTH = D_MODEL // 2
GM_GROUP_W = GM_WIDTH // GM_GROUPS
GM_CHUNK = 128

AB_SPLITS = (NSA_Q_W, 3 * NSA_HEADS, NSA_KV_W, NSA_KV_W, NSA_KV_W, GM_WIDTH, GM_WIDTH)
AB_SPLITS_GATE_LAST = (AB_SPLITS[0],) + AB_SPLITS[2:] + (AB_SPLITS[1],)

DIL_CFG = ((128, 1), (512, 4), (2048, 16))
DIL_HEADS = 8
DIL_KV_HEADS = 2
DIL_GROUP = DIL_HEADS // DIL_KV_HEADS
DIL_Q_W = DIL_HEADS * HEAD_DIM
DIL_KV_W = DIL_KV_HEADS * HEAD_DIM
C_OUT_W = DIL_Q_W

N_EXPERTS = 32
TOP_K = 4
D_FF = D_MODEL
SWIGLU_ALPHA = 1.702
SWIGLU_LIMIT = 7.0

VMEM_LIMIT_BYTES = 56 * 1024 * 1024
ROW_TILE = 512
MOE_TILE = 512


def _row_tile(t):
    return t if t <= ROW_TILE else ROW_TILE


def _mod_spec(mod, tm):
    if mod.shape[1] == 1:
        return pl.BlockSpec((1, 1, mod.shape[2]), lambda b, i: (b, 0, 0))
    return pl.BlockSpec((1, tm, mod.shape[2]), lambda b, i: (b, i, 0))


def _split_bf16(v):
    hi = v.astype(BF16)
    lo = (v - hi.astype(F32)).astype(BF16)
    return hi, lo


def _mod_norm(x, g, sc, sh):
    h = x * lax.rsqrt(jnp.mean(x * x, axis=-1, keepdims=True) + NORM_EPS) * g
    return h * (1.0 + sc) + sh


def _mod_norm_proj_kernel(x_ref, g_ref, sc_ref, sh_ref, w_ref, *o_refs, splits):
    h = _mod_norm(x_ref[0], g_ref[...], sc_ref[0], sh_ref[0])
    y = jnp.dot(h.astype(BF16), w_ref[...], preferred_element_type=F32)
    off = 0
    for o_ref, n in zip(o_refs, splits):
        o_ref[0] = y[:, off:off + n]
        off += n


def _mod_norm_proj(x, norm_g, scale, shift, w_bf16, splits):
    B, T, D = x.shape
    tm = _row_tile(T)
    N = w_bf16.shape[1]
    return pl.pallas_call(
        functools.partial(_mod_norm_proj_kernel, splits=splits),
        out_shape=[jax.ShapeDtypeStruct((B, T, n), F32) for n in splits],
        grid=(B, T // tm),
        in_specs=[
            pl.BlockSpec((1, tm, D), lambda b, i: (b, i, 0)),
            pl.BlockSpec((1, D), lambda b, i: (0, 0)),
            _mod_spec(scale, tm),
            _mod_spec(shift, tm),
            pl.BlockSpec((D, N), lambda b, i: (0, 0)),
        ],
        out_specs=[pl.BlockSpec((1, tm, n), lambda b, i: (b, i, 0)) for n in splits],
        compiler_params=pltpu.CompilerParams(
            dimension_semantics=("parallel", "parallel"), vmem_limit_bytes=VMEM_LIMIT_BYTES),
        name="mod_norm_proj",
    )(x, norm_g.reshape(1, D), scale, shift, w_bf16)


def _proj_residual_kernel(*refs, n_parts):
    a_refs = refs[:n_parts]
    w_refs = refs[n_parts:2 * n_parts]
    x_ref, g_ref, o_ref = refs[2 * n_parts:]
    y = None
    for a_ref, w_ref in zip(a_refs, w_refs):
        p = jnp.dot(a_ref[0].astype(BF16), w_ref[...], preferred_element_type=F32)
        y = p if y is None else y + p
    o_ref[0] = x_ref[0] + g_ref[0] * y


def _proj_residual(a_parts, w_parts_bf16, x, gate):
    B, T, D = x.shape
    tm = _row_tile(T)
    n_parts = len(a_parts)
    in_specs = [pl.BlockSpec((1, tm, a.shape[2]), lambda b, i: (b, i, 0)) for a in a_parts]
    in_specs += [pl.BlockSpec(w.shape, lambda b, i: (0, 0)) for w in w_parts_bf16]
    in_specs += [pl.BlockSpec((1, tm, D), lambda b, i: (b, i, 0)), _mod_spec(gate, tm)]
    return pl.pallas_call(
        functools.partial(_proj_residual_kernel, n_parts=n_parts),
        out_shape=jax.ShapeDtypeStruct((B, T, D), F32),
        grid=(B, T // tm),
        in_specs=in_specs,
        out_specs=pl.BlockSpec((1, tm, D), lambda b, i: (b, i, 0)),
        compiler_params=pltpu.CompilerParams(
            dimension_semantics=("parallel", "parallel"), vmem_limit_bytes=VMEM_LIMIT_BYTES),
        name="proj_residual",
    )(*a_parts, *w_parts_bf16, x, gate)


ROUTE_RANK_LANE = TOP_K


def _mod_norm_router_kernel(x_ref, g_ref, sc_ref, sh_ref, whi_ref, wlo_ref, b_ref, cnt_in_ref,
                            h_ref, ri_ref, rg_ref, cnt_ref, *, tm):
    @pl.when((pl.program_id(0) == 0) & (pl.program_id(1) == 0))
    def _():
        cnt_ref[...] = cnt_in_ref[...]

    h = _mod_norm(x_ref[0], g_ref[...], sc_ref[0], sh_ref[0])
    h_ref[0] = h
    h_hi, h_lo = _split_bf16(h)
    w_hi = whi_ref[...]
    logits = (jnp.dot(h_hi, w_hi, preferred_element_type=F32)
              + jnp.dot(h_lo, w_hi, preferred_element_type=F32)
              + jnp.dot(h_hi, wlo_ref[...], preferred_element_type=F32)) + b_ref[...]
    n_exp = logits.shape[-1]
    lane = lax.broadcasted_iota(jnp.int32, (1, n_exp), 1)
    work = logits
    vals, ids, hits = [], [], []
    for _ in range(TOP_K):
        m = jnp.max(work, axis=-1, keepdims=True)
        idx = jnp.min(jnp.where(work == m, lane, n_exp), axis=-1, keepdims=True)
        hit = lane == idx
        vals.append(m)
        ids.append(idx)
        hits.append(hit)
        work = jnp.where(hit, NEG_INF, work)
    exps = [jnp.exp(v - vals[0]) for v in vals]
    denom = exps[0] + exps[1] + exps[2] + exps[3]
    onehot = jnp.zeros(logits.shape, F32)
    for hit in hits:
        onehot = onehot + jnp.where(hit, 1.0, 0.0)
    row = lax.broadcasted_iota(jnp.int32, (tm, tm), 0)
    col = lax.broadcasted_iota(jnp.int32, (tm, tm), 1)
    earlier = jnp.where(row > col, 1.0, 0.0).astype(BF16)
    before = jnp.dot(earlier, onehot.astype(BF16), preferred_element_type=F32) + cnt_ref[...]
    cnt_ref[...] = cnt_ref[...] + jnp.sum(onehot, axis=0, keepdims=True)
    out_lane = lax.broadcasted_iota(jnp.int32, (1, LANES), 1)
    ri = jnp.zeros((tm, LANES), jnp.int32)
    rg = jnp.zeros((tm, LANES), F32)
    for k in range(TOP_K):
        rank = jnp.sum(jnp.where(hits[k], before, 0.0), axis=-1, keepdims=True).astype(jnp.int32)
        ri = jnp.where(out_lane == k, ids[k], ri)
        ri = jnp.where(out_lane == ROUTE_RANK_LANE + k, rank, ri)
        rg = jnp.where(out_lane == k, exps[k] / denom, rg)
    ri_ref[0] = ri
    rg_ref[0] = rg


def _mod_norm_router(x, norm_g, scale, shift, w_router, b_router, counts_in):
    B, T, D = x.shape
    tm = _row_tile(T)
    E = w_router.shape[1]
    w_hi, w_lo = _split_bf16(w_router)
    tok = lambda n: pl.BlockSpec((1, tm, n), lambda b, i: (b, i, 0))
    return pl.pallas_call(
        functools.partial(_mod_norm_router_kernel, tm=tm),
        out_shape=[jax.ShapeDtypeStruct((B, T, D), F32), jax.ShapeDtypeStruct((B, T, LANES), jnp.int32),
                   jax.ShapeDtypeStruct((B, T, LANES), F32), jax.ShapeDtypeStruct((1, E), F32)],
        grid=(B, T // tm),
        in_specs=[
            tok(D),
            pl.BlockSpec((1, D), lambda b, i: (0, 0)),
            _mod_spec(scale, tm),
            _mod_spec(shift, tm),
            pl.BlockSpec((D, E), lambda b, i: (0, 0)),
            pl.BlockSpec((D, E), lambda b, i: (0, 0)),
            pl.BlockSpec((1, E), lambda b, i: (0, 0)),
            pl.BlockSpec((1, E), lambda b, i: (0, 0)),
        ],
        out_specs=[tok(D), tok(LANES), tok(LANES), pl.BlockSpec((1, E), lambda b, i: (0, 0))],
        compiler_params=pltpu.CompilerParams(
            dimension_semantics=("arbitrary", "arbitrary"), vmem_limit_bytes=VMEM_LIMIT_BYTES),
        name="mod_norm_router",
    )(x, norm_g.reshape(1, D), scale, shift, w_hi, w_lo, b_router.reshape(1, E), counts_in)


def _moe_ffn_kernel(te_ref, tx_ref, tv_ref, x_ref, wu_ref, bu_ref, wd_ref, bd_ref, o_ref):
    i = pl.program_id(0)

    @pl.when(tv_ref[i] != 0)
    def _():
        hu = jnp.dot(x_ref[...].astype(BF16), wu_ref[0].astype(BF16), preferred_element_type=F32) + bu_ref[0]
        glu = jnp.minimum(hu[:, :D_FF], SWIGLU_LIMIT)
        lin = jnp.clip(hu[:, D_FF:], -SWIGLU_LIMIT, SWIGLU_LIMIT)
        act = glu * jax.nn.sigmoid(SWIGLU_ALPHA * glu) * (lin + 1.0)
        o_ref[...] = jnp.dot(act.astype(BF16), wd_ref[0].astype(BF16), preferred_element_type=F32) + bd_ref[0]

    @pl.when(tv_ref[i] == 0)
    def _():
        o_ref[...] = jnp.zeros_like(o_ref)


def _moe_ffn(tile_exp, tile_x, tile_valid, x_sorted, w_up, b_up, w_down, b_down):
    n_slots, D = x_sorted.shape
    n_tiles = n_slots // MOE_TILE
    E, _, F2 = w_up.shape
    grid_spec = pltpu.PrefetchScalarGridSpec(
        num_scalar_prefetch=3,
        grid=(n_tiles,),
        in_specs=[
            pl.BlockSpec((MOE_TILE, D), lambda i, te, tx, tv: (tx[i], 0)),
            pl.BlockSpec((1, D, F2), lambda i, te, tx, tv: (te[i], 0, 0)),
            pl.BlockSpec((1, 1, F2), lambda i, te, tx, tv: (te[i], 0, 0)),
            pl.BlockSpec((1, F2 // 2, D), lambda i, te, tx, tv: (te[i], 0, 0)),
            pl.BlockSpec((1, 1, D), lambda i, te, tx, tv: (te[i], 0, 0)),
        ],
        out_specs=pl.BlockSpec((MOE_TILE, D), lambda i, te, tx, tv: (i, 0)),
    )
    return pl.pallas_call(
        _moe_ffn_kernel,
        out_shape=jax.ShapeDtypeStruct((n_slots, D), F32),
        grid_spec=grid_spec,
        compiler_params=pltpu.CompilerParams(
            dimension_semantics=("arbitrary",), vmem_limit_bytes=VMEM_LIMIT_BYTES),
        name="moe_ffn",
    )(tile_exp, tile_x, tile_valid, x_sorted, w_up, b_up.reshape(E, 1, F2), w_down, b_down.reshape(E, 1, D))


DISPATCH_TILE = 256
COMBINE_TILE = 128


def _row_copy(src_ref, src_row, dst_ref, dst_row, sem):
    return pltpu.make_async_copy(src_ref.at[pl.ds(src_row, 1)], dst_ref.at[pl.ds(dst_row, 1)], sem)


def _moe_dispatch_kernel(dest_ref, h_ref, xs_in_ref, xs_ref, sem, *, tm):
    del xs_in_ref
    h_rows = h_ref.at[0]

    def issue(t, carry):
        for k in range(TOP_K):
            _row_copy(h_rows, t, xs_ref, dest_ref[0, 0, 0, t * TOP_K + k], sem).start()
        return carry

    def drain(t, carry):
        for k in range(TOP_K):
            _row_copy(h_rows, t, xs_ref, dest_ref[0, 0, 0, t * TOP_K + k], sem).wait()
        return carry

    lax.fori_loop(0, tm, issue, 0)
    lax.fori_loop(0, tm, drain, 0)


def _dest_spec(tm):
    return pl.BlockSpec((1, 1, 1, tm * TOP_K), lambda b, i: (b, i, 0, 0), memory_space=pltpu.SMEM)


def _moe_dispatch(dest, h, x_sorted):
    B, T, D = h.shape
    tm = min(T, DISPATCH_TILE)
    return pl.pallas_call(
        functools.partial(_moe_dispatch_kernel, tm=tm),
        out_shape=jax.ShapeDtypeStruct(x_sorted.shape, x_sorted.dtype),
        grid=(B, T // tm),
        in_specs=[_dest_spec(tm),
                  pl.BlockSpec((1, tm, D), lambda b, i: (b, i, 0)),
                  pl.BlockSpec(memory_space=pl.ANY)],
        out_specs=pl.BlockSpec(memory_space=pl.ANY),
        scratch_shapes=[pltpu.SemaphoreType.DMA(())],
        input_output_aliases={2: 0},
        compiler_params=pltpu.CompilerParams(dimension_semantics=("arbitrary", "arbitrary")),
        name="moe_dispatch",
    )(dest.reshape(B, T // tm, 1, tm * TOP_K), h, x_sorted)


def _moe_combine_kernel(dest_ref, ys_ref, rg_ref, x_ref, gf_ref, o_ref, buf, sem, *, tm):
    def issue(t, carry):
        for k in range(TOP_K):
            _row_copy(ys_ref, dest_ref[0, 0, 0, t * TOP_K + k], buf.at[k], t, sem).start()
        return carry

    def drain(t, carry):
        for k in range(TOP_K):
            _row_copy(ys_ref, dest_ref[0, 0, 0, t * TOP_K + k], buf.at[k], t, sem).wait()
        return carry

    lax.fori_loop(0, tm, issue, 0)
    lax.fori_loop(0, tm, drain, 0)
    gates = rg_ref[0]
    y = gates[:, 0:1] * buf[0]
    for k in range(1, TOP_K):
        y = y + gates[:, k:k + 1] * buf[k]
    o_ref[0] = x_ref[0] + gf_ref[0] * y


def _moe_combine(dest, y_sorted, route_g, x, gate_f):
    B, T, D = x.shape
    tm = min(T, COMBINE_TILE)
    tok = lambda n: pl.BlockSpec((1, tm, n), lambda b, i: (b, i, 0))
    return pl.pallas_call(
        functools.partial(_moe_combine_kernel, tm=tm),
        out_shape=jax.ShapeDtypeStruct((B, T, D), F32),
        grid=(B, T // tm),
        in_specs=[_dest_spec(tm), pl.BlockSpec(memory_space=pl.ANY), tok(LANES), tok(D), _mod_spec(gate_f, tm)],
        out_specs=tok(D),
        scratch_shapes=[pltpu.VMEM((TOP_K, tm, D), F32), pltpu.SemaphoreType.DMA(())],
        compiler_params=pltpu.CompilerParams(
            dimension_semantics=("parallel", "parallel"), vmem_limit_bytes=VMEM_LIMIT_BYTES),
        name="moe_combine",
    )(dest.reshape(B, T // tm, 1, tm * TOP_K), y_sorted, route_g, x, gate_f)


def _moe_plan(counts):
    counts = counts.astype(jnp.int32)
    padded = (counts + MOE_TILE - 1) // MOE_TILE * MOE_TILE
    pad_end = jnp.cumsum(padded)
    pad_start = pad_end - padded
    n_used = pad_end[-1] // MOE_TILE
    return pad_start, pad_end, n_used


def _moe_tiles(pad_end, n_used, n_tiles):
    tile = jnp.arange(n_tiles, dtype=jnp.int32)
    tile_valid = (tile < n_used).astype(jnp.int32)
    tile_x = jnp.minimum(tile, n_used - 1)
    tile_exp = jnp.sum((tile_x[:, None] * MOE_TILE >= pad_end[None, :]).astype(jnp.int32), axis=1)
    return jnp.minimum(tile_exp, N_EXPERTS - 1), tile_x, tile_valid


N_QHEADS = 8
LANES = 128
HALF = HEAD_DIM


def _nt_dot(a, b):
    return lax.dot_general(a, b, (((1,), (1,)), ((), ())), preferred_element_type=F32)


def _band_mask(qpos, kpos, window, dilation):
    delta = qpos - kpos
    valid = (delta >= 0) & (delta <= window)
    if dilation > 1:
        valid = valid & ((delta & (dilation - 1)) == 0)
    return valid


def _softmax_pv(s, valid, v, tq):
    n = s.shape[-1]
    s = jnp.where(valid[None], s.reshape(N_QHEADS, tq, n), NEG_INF)
    m = jnp.max(s, axis=-1, keepdims=True)
    p = jnp.exp(s - m)
    l = jnp.sum(p, axis=-1, keepdims=True)
    o = jnp.dot(p.reshape(N_QHEADS * tq, n).astype(BF16), v, preferred_element_type=F32)
    return o.reshape(N_QHEADS, tq, LANES) / l, m + jnp.log(l)


def _store_heads(o_ref, o, lse):
    if lse is None:
        o_ref[0] = o
        return
    lane = lax.broadcasted_iota(jnp.int32, (1, LANES), 1)
    for h in range(N_QHEADS):
        own = (lane < HALF) if h < N_QHEADS // 2 else (lane >= HALF)
        o_ref[0, h] = jnp.where(own, o[h], lse[h])


def _band_self_kernel(q_ref, k_ref, v_ref, o_ref, *, window, tq, span, want_lse):
    T = k_ref.shape[1]
    q0 = pl.program_id(1) * tq
    if span == T:
        start = 0
        k = k_ref[0]
        v = v_ref[0]
    else:
        start = pl.multiple_of(jnp.maximum(q0 - window, 0), LANES)
        k = k_ref[0, pl.ds(start, span), :]
        v = v_ref[0, pl.ds(start, span), :]
    q = q_ref[0].reshape(N_QHEADS * tq, LANES).astype(BF16)
    qpos = q0 + lax.broadcasted_iota(jnp.int32, (tq, 1), 0)
    kpos = start + lax.broadcasted_iota(jnp.int32, (1, span), 1)
    o, lse = _softmax_pv(_nt_dot(q, k), _band_mask(qpos, kpos, window, 1), v, tq)
    _store_heads(o_ref, o, lse if want_lse else None)


def _band_self(q, k, v, window, want_lse):
    B, _, T, _ = q.shape
    tq = min(T, Q_BLOCK)
    span = min(T, window + tq)
    return pl.pallas_call(
        functools.partial(_band_self_kernel, window=window, tq=tq, span=span, want_lse=want_lse),
        out_shape=jax.ShapeDtypeStruct((B, N_QHEADS, T, LANES), F32),
        grid=(B, T // tq),
        in_specs=[
            pl.BlockSpec((1, N_QHEADS, tq, LANES), lambda b, i: (b, 0, i, 0)),
            pl.BlockSpec((1, T, LANES), lambda b, i: (b, 0, 0)),
            pl.BlockSpec((1, T, LANES), lambda b, i: (b, 0, 0)),
        ],
        out_specs=pl.BlockSpec((1, N_QHEADS, tq, LANES), lambda b, i: (b, 0, i, 0)),
        compiler_params=pltpu.CompilerParams(
            dimension_semantics=("parallel", "parallel"), vmem_limit_bytes=VMEM_LIMIT_BYTES),
        name="band_self",
    )(q, k, v)


def _band_tail_kernel(q_ref, kv_ref, o_ref, *, window, dilation, tq, want_lse):
    L = kv_ref.shape[1]
    kv = kv_ref[0]
    k = kv[:, :LANES].astype(BF16)
    v = kv[:, LANES:].astype(BF16)
    q = q_ref[0].reshape(N_QHEADS * tq, LANES).astype(BF16)
    qpos = (L - tq) + lax.broadcasted_iota(jnp.int32, (tq, 1), 0)
    kpos = lax.broadcasted_iota(jnp.int32, (1, L), 1)
    o, lse = _softmax_pv(_nt_dot(q, k), _band_mask(qpos, kpos, window, dilation), v, tq)
    _store_heads(o_ref, o, lse if want_lse else None)


def _band_tail(q, kv_all, window, dilation, want_lse):
    B, _, T, _ = q.shape
    L = kv_all.shape[1]
    return pl.pallas_call(
        functools.partial(_band_tail_kernel, window=window, dilation=dilation, tq=T, want_lse=want_lse),
        out_shape=jax.ShapeDtypeStruct((B, N_QHEADS, T, LANES), F32),
        grid=(B,),
        in_specs=[
            pl.BlockSpec((1, N_QHEADS, T, LANES), lambda b: (b, 0, 0, 0)),
            pl.BlockSpec((1, L, 2 * LANES), lambda b: (b, 0, 0)),
        ],
        out_specs=pl.BlockSpec((1, N_QHEADS, T, LANES), lambda b: (b, 0, 0, 0)),
        compiler_params=pltpu.CompilerParams(
            dimension_semantics=("parallel",), vmem_limit_bytes=VMEM_LIMIT_BYTES),
        name="band_tail",
    )(q, kv_all)


def _pad_heads(q):
    z = jnp.zeros_like(q[:, :, :N_QHEADS // 2])
    lo = jnp.concatenate([q[:, :, :N_QHEADS // 2], z], axis=-1)
    hi = jnp.concatenate([z, q[:, :, N_QHEADS // 2:]], axis=-1)
    return jnp.concatenate([lo, hi], axis=2).transpose(0, 2, 1, 3)


def _unpad_heads(o):
    B, _, T, _ = o.shape
    x = o.transpose(0, 2, 1, 3).reshape(B, T, N_QHEADS, 2, HALF)
    lower = (jnp.arange(N_QHEADS) < N_QHEADS // 2)[None, None, :, None]
    own = jnp.where(lower, x[..., 0, :], x[..., 1, :])
    other = jnp.where(lower, x[..., 1, :], x[..., 0, :])
    return own, other


def _block_mean_kernel(x_ref, o_ref):
    rows = x_ref.shape[1]
    x = x_ref[0].reshape(rows // NSA_BLOCK, NSA_BLOCK, x_ref.shape[2])
    o_ref[0] = jnp.sum(x, axis=1) * (1.0 / NSA_BLOCK)


def _block_mean(kvc):
    B, T, W = kvc.shape
    tm = _row_tile(T)
    return pl.pallas_call(
        _block_mean_kernel,
        out_shape=jax.ShapeDtypeStruct((B, T // NSA_BLOCK, W), F32),
        grid=(B, T // tm),
        in_specs=[pl.BlockSpec((1, tm, W), lambda b, i: (b, i, 0))],
        out_specs=pl.BlockSpec((1, tm // NSA_BLOCK, W), lambda b, i: (b, i, 0)),
        compiler_params=pltpu.CompilerParams(dimension_semantics=("parallel", "parallel")),
        name="block_mean",
    )(kvc)


PAGES_PER_STEP = 8


def _page_specs(n_cols):
    return [pl.BlockSpec((1, PAGE_SIZE, n_cols), functools.partial(
        lambda b, s, pt, pg: (pt[b, s * PAGES_PER_STEP + pg], 0, 0), pg=pg)) for pg in range(PAGES_PER_STEP)]


def _page_block_mean_kernel(pt_ref, *refs):
    page_refs, o_ref = refs[:-1], refs[-1]
    per_page = PAGE_SIZE // NSA_BLOCK
    for pg, page_ref in enumerate(page_refs):
        x = page_ref[0].reshape(per_page, NSA_BLOCK, page_ref.shape[2])
        o_ref[0, pg * per_page:(pg + 1) * per_page, :] = jnp.sum(x, axis=1) * (1.0 / NSA_BLOCK)


def _page_block_mean(cache, page_table):
    B, n_pages = page_table.shape
    W = cache.shape[2]
    per_step = PAGES_PER_STEP * PAGE_SIZE // NSA_BLOCK
    grid_spec = pltpu.PrefetchScalarGridSpec(
        num_scalar_prefetch=1,
        grid=(B, n_pages // PAGES_PER_STEP),
        in_specs=_page_specs(W),
        out_specs=pl.BlockSpec((1, per_step, W), lambda b, s, pt: (b, s, 0)),
    )
    return pl.pallas_call(
        _page_block_mean_kernel,
        out_shape=jax.ShapeDtypeStruct((B, n_pages * PAGE_SIZE // NSA_BLOCK, W), F32),
        grid_spec=grid_spec,
        compiler_params=pltpu.CompilerParams(dimension_semantics=("parallel", "arbitrary")),
        name="page_block_mean",
    )(page_table, *([cache] * PAGES_PER_STEP))


def _head_mean_sq(x, ones_bd):
    hi, lo = _split_bf16(x * x)
    return jnp.dot(hi, ones_bd, preferred_element_type=F32) + jnp.dot(lo, ones_bd, preferred_element_type=F32)


def _cmp_finish_kernel(mean_ref, pe_ref, wk_ref, wv_ref, gain_ref, bd_ref, kc_ref, vc_ref):
    s = mean_ref[0] + pe_ref[...]
    sk = jnp.dot(s[:, :LANES].astype(BF16), wk_ref[...], preferred_element_type=F32)
    vc_ref[0] = jnp.dot(s[:, LANES:].astype(BF16), wv_ref[...], preferred_element_type=F32)
    kc_ref[0] = sk * lax.rsqrt(_head_mean_sq(sk, bd_ref[...]) + NORM_EPS) * gain_ref[...]


def _block_diag2(w):
    z = jnp.zeros_like(w)
    return jnp.concatenate([jnp.concatenate([w, z], axis=1), jnp.concatenate([z, w], axis=1)], axis=0)


def _cmp_finish(mean, pe, w_phi, kc_gain):
    B, n_cb, W = mean.shape
    pe_mean = jnp.mean(pe, axis=0)
    pe_row = jnp.concatenate([pe_mean[0], pe_mean[0], pe_mean[1], pe_mean[1]])[None, :]
    wk = _block_diag2(w_phi[0]).astype(BF16)
    wv = _block_diag2(w_phi[1]).astype(BF16)
    bd = _block_diag2(jnp.full((HALF, HALF), 1.0 / HALF, F32)).astype(BF16)
    gain = jnp.concatenate([kc_gain, kc_gain])[None, :]
    full = lambda shape: pl.BlockSpec(shape, lambda b: (0,) * len(shape))
    return pl.pallas_call(
        _cmp_finish_kernel,
        out_shape=[jax.ShapeDtypeStruct((B, n_cb, LANES), F32)] * 2,
        grid=(B,),
        in_specs=[pl.BlockSpec((1, n_cb, W), lambda b: (b, 0, 0)), full((1, W)), full((LANES, LANES)),
                  full((LANES, LANES)), full((1, LANES)), full((LANES, LANES))],
        out_specs=[pl.BlockSpec((1, n_cb, LANES), lambda b: (b, 0, 0))] * 2,
        compiler_params=pltpu.CompilerParams(dimension_semantics=("parallel",)),
        name="cmp_finish",
    )(mean, pe_row, wk, wv, gain, bd)


def _cmp_select_kernel(q_ref, kc_ref, vc_ref, ocmp_ref, sel_ref, *, tq, pos0):
    n_cb = kc_ref.shape[1]
    q0 = pl.program_id(1) * tq
    qh, ql = _split_bf16(q_ref[0].reshape(N_QHEADS * tq, LANES))
    kh, kl = _split_bf16(kc_ref[0])
    s = (_nt_dot(qh, kh) + _nt_dot(ql, kh) + _nt_dot(qh, kl)) * (HEAD_DIM ** -0.5)
    cur = (pos0 + q0 + lax.broadcasted_iota(jnp.int32, (tq, 1), 0)) // NSA_BLOCK
    blk = lax.broadcasted_iota(jnp.int32, (1, n_cb), 1)
    vis = blk < cur
    s = jnp.where(vis[None], s.reshape(N_QHEADS, tq, n_cb), NEG_INF)
    p = jnp.exp(s - jnp.max(s, axis=-1, keepdims=True))
    p = jnp.where(vis[None], p / jnp.sum(p, axis=-1, keepdims=True), 0.0)
    o = jnp.dot(p.reshape(N_QHEADS * tq, n_cb).astype(BF16), vc_ref[0].astype(BF16), preferred_element_type=F32)
    ocmp_ref[0] = o.reshape(N_QHEADS, tq, LANES)
    for kvh in range(NSA_KV_HEADS):
        g0 = kvh * NSA_GROUP
        imp = jnp.where(vis, p[g0] + p[g0 + 1] + p[g0 + 2] + p[g0 + 3], -1.0)
        rank = jnp.zeros((tq, n_cb), jnp.int32)
        for i in range(n_cb):
            col = imp[:, i:i + 1]
            wins_tie = jnp.where(blk > i, 1, 0)
            rank = rank + jnp.where(col > imp, 1, jnp.where(col == imp, wins_tie, 0))
        chosen = jnp.where(vis, jnp.where(rank < NSA_N_SEL - 1, 1.0, 0.0), jnp.where(blk == cur, 1.0, 0.0))
        sel_ref[0, kvh] = chosen.astype(BF16)


def _cmp_select(qn, kc, vc, pos0):
    B, _, T, _ = qn.shape
    n_cb = kc.shape[1]
    tq = min(T, Q_BLOCK)
    return pl.pallas_call(
        functools.partial(_cmp_select_kernel, tq=tq, pos0=pos0),
        out_shape=[jax.ShapeDtypeStruct((B, N_QHEADS, T, LANES), F32),
                   jax.ShapeDtypeStruct((B, NSA_KV_HEADS, T, n_cb), BF16)],
        grid=(B, T // tq),
        in_specs=[
            pl.BlockSpec((1, N_QHEADS, tq, LANES), lambda b, i: (b, 0, i, 0)),
            pl.BlockSpec((1, n_cb, LANES), lambda b, i: (b, 0, 0)),
            pl.BlockSpec((1, n_cb, LANES), lambda b, i: (b, 0, 0)),
        ],
        out_specs=[pl.BlockSpec((1, N_QHEADS, tq, LANES), lambda b, i: (b, 0, i, 0)),
                   pl.BlockSpec((1, NSA_KV_HEADS, tq, n_cb), lambda b, i: (b, 0, i, 0))],
        compiler_params=pltpu.CompilerParams(
            dimension_semantics=("parallel", "parallel"), vmem_limit_bytes=VMEM_LIMIT_BYTES),
        name="cmp_select",
    )(qn, kc, vc)


def _selection_bias(sel, first_key, n_keys, causal):
    n_cb = sel.shape[-1]
    key_blk = (first_key + lax.broadcasted_iota(jnp.int32, (1, n_keys), 1)) // NSA_BLOCK
    expand = jnp.where(lax.broadcasted_iota(jnp.int32, (n_cb, 1), 0) == key_blk, 1.0, 0.0).astype(BF16)
    out = []
    for kvh in range(NSA_KV_HEADS):
        picked = jnp.dot(sel[kvh], expand, preferred_element_type=F32)
        bias = (picked - 1.0) * 1e30
        if causal is not None:
            bias = jnp.where(causal, bias, NEG_INF)
        out.append(jnp.broadcast_to(bias[None], (NSA_GROUP,) + bias.shape))
    return jnp.concatenate(out, axis=0)


def _online_softmax_step(s, v, m_ref, l_ref, acc_ref, tq):
    n = s.shape[-1]
    m_old = m_ref[...]
    m_new = jnp.maximum(m_old, jnp.max(s, axis=-1, keepdims=True))
    alpha = jnp.exp(m_old - m_new)
    p = jnp.exp(s - m_new[..., :1])
    l_ref[...] = alpha * l_ref[...] + jnp.sum(p, axis=-1, keepdims=True)
    pv = jnp.dot(p.reshape(N_QHEADS * tq, n).astype(BF16), v, preferred_element_type=F32)
    acc_ref[...] = alpha * acc_ref[...] + pv.reshape(N_QHEADS, tq, LANES)
    m_ref[...] = m_new


def _init_softmax_state(m_ref, l_ref, acc_ref):
    m_ref[...] = jnp.full(m_ref.shape, NEG_INF, F32)
    l_ref[...] = jnp.zeros(l_ref.shape, F32)
    acc_ref[...] = jnp.zeros(acc_ref.shape, F32)


def _gated_heads(gate_pre, o_cmp, o_slc, o_win):
    g = jax.nn.sigmoid(gate_pre)
    lane = lax.broadcasted_iota(jnp.int32, (1, LANES), 1)
    pairs = []
    for c in range(N_QHEADS // 2):
        mixed = []
        for h in (2 * c, 2 * c + 1):
            mixed.append(g[:, 3 * h:3 * h + 1] * o_cmp[h] + g[:, 3 * h + 1:3 * h + 2] * o_slc[h]
                         + g[:, 3 * h + 2:3 * h + 3] * o_win[h])
        if 2 * c < N_QHEADS // 2:
            pairs.append(jnp.where(lane < HALF, mixed[0], pltpu.roll(mixed[1], HALF, 1)))
        else:
            pairs.append(jnp.where(lane < HALF, pltpu.roll(mixed[0], HALF, 1), mixed[1]))
    return jnp.concatenate(pairs, axis=-1)


SLC_KEY_TILE = 512


def _nsa_prompt_kernel(q_ref, ks_ref, vs_ref, kw_ref, vw_ref, sel_ref, ocmp_ref, g_ref, o_ref,
                       m_ref, l_ref, acc_ref, *, tq, tk, span):
    T = ks_ref.shape[1]
    q0 = pl.program_id(1) * tq
    q = q_ref[0].reshape(N_QHEADS * tq, LANES)
    qpos = q0 + lax.broadcasted_iota(jnp.int32, (tq, 1), 0)
    sel = sel_ref[0]
    _init_softmax_state(m_ref, l_ref, acc_ref)

    def key_tile(j, carry):
        k0 = pl.multiple_of(j * tk, tk)
        kpos = k0 + lax.broadcasted_iota(jnp.int32, (1, tk), 1)
        s = _nt_dot(q, ks_ref[0, pl.ds(k0, tk), :]).reshape(N_QHEADS, tq, tk)
        s = s + _selection_bias(sel, k0, tk, kpos <= qpos)
        _online_softmax_step(s, vs_ref[0, pl.ds(k0, tk), :], m_ref, l_ref, acc_ref, tq)
        return carry

    lax.fori_loop(0, (q0 + tq + tk - 1) // tk, key_tile, 0)
    o_slc = acc_ref[...] / l_ref[...]

    if span == T:
        start = 0
        kw, vw = kw_ref[0], vw_ref[0]
    else:
        start = pl.multiple_of(jnp.maximum(q0 - NSA_WINDOW, 0), LANES)
        kw, vw = kw_ref[0, pl.ds(start, span), :], vw_ref[0, pl.ds(start, span), :]
    kpos = start + lax.broadcasted_iota(jnp.int32, (1, span), 1)
    o_win, _ = _softmax_pv(_nt_dot(q, kw), _band_mask(qpos, kpos, NSA_WINDOW, 1), vw, tq)
    o_ref[0] = _gated_heads(g_ref[0], ocmp_ref[0], o_slc, o_win)


def _nsa_prompt(qr, ks, vs, kw, vw, sel, ocmp, gate_pre):
    B, _, T, _ = qr.shape
    n_cb = sel.shape[-1]
    tq = min(T, Q_BLOCK)
    tk = min(T, SLC_KEY_TILE)
    span = min(T, NSA_WINDOW + tq)
    seq = pl.BlockSpec((1, T, LANES), lambda b, i: (b, 0, 0))
    heads = pl.BlockSpec((1, N_QHEADS, tq, LANES), lambda b, i: (b, 0, i, 0))
    stat = pltpu.VMEM((N_QHEADS, tq, LANES), F32)
    return pl.pallas_call(
        functools.partial(_nsa_prompt_kernel, tq=tq, tk=tk, span=span),
        out_shape=jax.ShapeDtypeStruct((B, T, NSA_Q_W), F32),
        grid=(B, T // tq),
        in_specs=[heads, seq, seq, seq, seq,
                  pl.BlockSpec((1, NSA_KV_HEADS, tq, n_cb), lambda b, i: (b, 0, i, 0)),
                  heads,
                  pl.BlockSpec((1, tq, gate_pre.shape[2]), lambda b, i: (b, i, 0))],
        out_specs=pl.BlockSpec((1, tq, NSA_Q_W), lambda b, i: (b, i, 0)),
        scratch_shapes=[stat, stat, stat],
        compiler_params=pltpu.CompilerParams(
            dimension_semantics=("parallel", "arbitrary"), vmem_limit_bytes=VMEM_LIMIT_BYTES),
        name="nsa_prompt",
    )(qr, ks, vs, kw, vw, sel, ocmp, gate_pre)


def _nsa_paged_kernel(pt_ref, q_ref, *refs, tq, pos0):
    page_refs = refs[:PAGES_PER_STEP]
    new_ref, sel_ref, ocmp_ref, owin_ref, g_ref, o_ref, m_ref, l_ref, acc_ref = refs[PAGES_PER_STEP:]
    step = pl.program_id(1)
    q = q_ref[0].reshape(N_QHEADS * tq, LANES).astype(BF16)
    sel = sel_ref[0]

    @pl.when(step == 0)
    def _():
        _init_softmax_state(m_ref, l_ref, acc_ref)

    def attend(k, v, first_key, causal):
        n = k.shape[0]
        s = _nt_dot(q, k).reshape(N_QHEADS, tq, n)
        if causal is None:
            s = s + _selection_bias(sel, first_key, n, None)
        else:
            s = jnp.where(causal[None], s, NEG_INF)
        _online_softmax_step(s, v, m_ref, l_ref, acc_ref, tq)

    attend(jnp.concatenate([r[0, :, :LANES].astype(BF16) for r in page_refs], axis=0),
           jnp.concatenate([r[0, :, LANES:].astype(BF16) for r in page_refs], axis=0),
           step * (PAGES_PER_STEP * PAGE_SIZE), None)

    @pl.when(step == pl.num_programs(1) - 1)
    def _():
        qpos = lax.broadcasted_iota(jnp.int32, (tq, 1), 0)
        kpos = lax.broadcasted_iota(jnp.int32, (1, PAGE_SIZE), 1)
        new = new_ref[0]
        attend(new[:, :LANES].astype(BF16), new[:, LANES:].astype(BF16), pos0, kpos <= qpos)
        o_ref[0] = _gated_heads(g_ref[0], ocmp_ref[0], acc_ref[...] / l_ref[...], owin_ref[0])


def _nsa_paged(page_table, qr, cache, new_tile, sel, ocmp, owin, gate_pre, pos0):
    B, _, T, _ = qr.shape
    n_pages = page_table.shape[1]
    n_cb = sel.shape[-1]
    W = cache.shape[2]
    heads = pl.BlockSpec((1, N_QHEADS, T, LANES), lambda b, s, pt: (b, 0, 0, 0))
    stat = pltpu.VMEM((N_QHEADS, T, LANES), F32)
    grid_spec = pltpu.PrefetchScalarGridSpec(
        num_scalar_prefetch=1,
        grid=(B, n_pages // PAGES_PER_STEP),
        in_specs=[heads] + _page_specs(W) + [
            pl.BlockSpec((1, PAGE_SIZE, W), lambda b, s, pt: (b, 0, 0)),
            pl.BlockSpec((1, NSA_KV_HEADS, T, n_cb), lambda b, s, pt: (b, 0, 0, 0)),
            heads, heads,
            pl.BlockSpec((1, T, gate_pre.shape[2]), lambda b, s, pt: (b, 0, 0))],
        out_specs=pl.BlockSpec((1, T, NSA_Q_W), lambda b, s, pt: (b, 0, 0)),
        scratch_shapes=[stat, stat, stat],
    )
    return pl.pallas_call(
        functools.partial(_nsa_paged_kernel, tq=T, pos0=pos0),
        out_shape=jax.ShapeDtypeStruct((B, T, NSA_Q_W), F32),
        grid_spec=grid_spec,
        compiler_params=pltpu.CompilerParams(
            dimension_semantics=("parallel", "arbitrary"), vmem_limit_bytes=VMEM_LIMIT_BYTES),
        name="nsa_paged",
    )(page_table, qr, *([cache] * PAGES_PER_STEP), new_tile, sel, ocmp, owin, gate_pre)


def _rms(x, g):
    xf = x.astype(F32)
    y = xf * lax.rsqrt(jnp.mean(xf * xf, axis=-1, keepdims=True) + NORM_EPS)
    return (y * g.astype(F32)).astype(x.dtype)


def _layernorm(x, g, b):
    xf = x.astype(F32)
    mu = jnp.mean(xf, axis=-1, keepdims=True)
    var = jnp.mean(jnp.square(xf - mu), axis=-1, keepdims=True)
    y = (xf - mu) * lax.rsqrt(var + NORM_EPS)
    return (y * g.astype(F32) + b.astype(F32)).astype(x.dtype)


def _rope(x, pos):
    half = ROT_DIM // 2
    inv = jnp.exp(-math.log(ROPE_THETA) * jnp.arange(half, dtype=F32) * (2.0 / ROT_DIM))
    ang = pos.astype(F32)[:, None] * inv[None, :]
    cos = jnp.cos(ang)[:, None, :]
    sin = jnp.sin(ang)[:, None, :]
    xr = x[..., :ROT_DIM].astype(F32)
    x1, x2 = xr[..., :half], xr[..., half:]
    rot = jnp.concatenate([x1 * cos - x2 * sin, x2 * cos + x1 * sin], axis=-1).astype(x.dtype)
    return jnp.concatenate([rot, x[..., ROT_DIM:]], axis=-1)


def _chunk_mix(v, ws, bs):
    B, T, W = v.shape
    tp = -(-T // GM_CHUNK) * GM_CHUNK
    vp = jnp.pad(v, ((0, 0), (0, tp - T), (0, 0))).reshape(B, tp // GM_CHUNK, GM_CHUNK, GM_GROUPS, GM_GROUP_W)
    wm = jnp.where(jnp.tril(jnp.ones((GM_CHUNK, GM_CHUNK), bool)), ws, 0.0).astype(v.dtype)
    s = jnp.einsum('gij,bnjgc->bnigc', wm, vp) + jnp.transpose(bs)[None, None, :, :, None]
    return s.reshape(B, tp, W)[:, :T]


def _key_prep(kv, g, pos):
    B, T = kv.shape[:2]
    kv = kv.reshape(B, T, 2, NSA_KV_HEADS, HEAD_DIM)
    k = _rope(_rms(kv[:, :, 0], g), pos)
    return jnp.stack([k, kv[:, :, 1]], axis=2)


def _kv5(x):
    return x.reshape(x.shape[:2] + (2, NSA_KV_HEADS, HEAD_DIM))


def _mixer_ab(parts, start, pasts, weights, page_table):
    qk_gain, pe, w_phi, ln_g, ln_b, ws, bs = weights
    q, kvc, kvs, kvw, u, v, gate = parts
    B, T, _ = q.shape
    pos = start + jnp.arange(T, dtype=jnp.int32)
    scale = HEAD_DIM ** -0.5
    q = _rms(q.reshape(B, T, NSA_HEADS, HEAD_DIM), qk_gain[0])
    qn = _pad_heads(q)
    qr = _pad_heads(_rope(q, pos) * scale)
    kvs = _key_prep(kvs, qk_gain[2], pos).reshape(B, T, NSA_KV_W)
    kvw = _key_prep(kvw, qk_gain[3], pos).reshape(B, T, NSA_KV_W)
    if pasts is None:
        kc, vc = _cmp_finish(_block_mean(kvc), pe, w_phi, qk_gain[1])
        ocmp, sel = _cmp_select(qn, kc, vc, start)
        half = lambda x, i: x[..., i * LANES:(i + 1) * LANES].astype(BF16)
        o_a = _nsa_prompt(qr.astype(BF16), half(kvs, 0), half(kvs, 1), half(kvw, 0), half(kvw, 1), sel, ocmp, gate)
        kvw_all = kvw
    else:
        cache_cmp, cache_slc, win_past = pasts
        kc, vc = _cmp_finish(_page_block_mean(cache_cmp, page_table), pe, w_phi, qk_gain[1])
        ocmp, sel = _cmp_select(qn, kc, vc, start)
        kvw_all = jnp.concatenate([win_past, kvw], axis=1)
        owin = _band_tail(qr, kvw_all, NSA_WINDOW, 1, False)
        new_tile = jnp.pad(kvs, ((0, 0), (0, PAGE_SIZE - T), (0, 0)))
        o_a = _nsa_paged(page_table, qr, cache_slc, new_tile, sel, ocmp, owin, gate, start)
    u = jax.nn.gelu(u)
    v = _layernorm(jax.nn.gelu(v), ln_g, ln_b)
    o_b = u * _chunk_mix(v, ws, bs)
    keep_w = min(NSA_WINDOW, kvw_all.shape[1])
    n_cur = (T - 1) % GM_CHUNK + 1
    return (o_a, o_b), (_kv5(kvc), _kv5(kvs), _kv5(kvw_all[:, kvw_all.shape[1] - keep_w:]), v[:, T - n_cur:])


def _residue_major(x, d):
    B, T = x.shape[:2]
    x = x.reshape((B, T // d, d) + x.shape[2:])
    return jnp.swapaxes(x, 1, 2).reshape((B * d, T // d) + x.shape[3:])


def _time_major(x, d):
    Bd, Tm = x.shape[:2]
    x = x.reshape((Bd // d, d, Tm) + x.shape[2:])
    return jnp.swapaxes(x, 1, 2).reshape((Bd // d, Tm * d) + x.shape[3:])


def _mixer_c(parts, start, pasts, qk_gain):
    B, T, _ = parts[0].shape
    pos = start + jnp.arange(T, dtype=jnp.int32)
    scale = HEAD_DIM ** -0.5
    outs, lses, new_bufs = [], [], []
    for g, (window, dilation) in enumerate(DIL_CFG):
        q, k, v = parts[3 * g], parts[3 * g + 1], parts[3 * g + 2]
        q = _rope(_rms(q.reshape(B, T, DIL_HEADS, HEAD_DIM), qk_gain[0]), pos) * scale
        k = _rope(_rms(k.reshape(B, T, DIL_KV_HEADS, HEAD_DIM), qk_gain[1]), pos)
        kv_new = jnp.concatenate([k.reshape(B, T, DIL_KV_W), v], axis=-1)
        if pasts is None:
            kvd = _residue_major(kv_new, dilation).astype(BF16)
            qd = _pad_heads(_residue_major(q, dilation)).astype(BF16)
            o = _band_self(qd, kvd[..., :LANES], kvd[..., LANES:], window // dilation, True)
            own, other = _unpad_heads(o)
            own, other = _time_major(own, dilation), _time_major(other, dilation)
            kv_all = kv_new
        else:
            kv_all = jnp.concatenate([pasts[g], kv_new], axis=1)
            own, other = _unpad_heads(_band_tail(_pad_heads(q), kv_all, window, dilation, True))
        outs.append(own)
        lses.append(other[..., 0])
        keep = min(window, kv_all.shape[1])
        new_bufs.append(kv_all[:, kv_all.shape[1] - keep:].reshape(B, keep, 2, DIL_KV_HEADS, HEAD_DIM))
    alpha = jax.nn.softmax(jnp.stack(lses, axis=0), axis=0)
    o = jnp.sum(alpha[..., None] * jnp.stack(outs, axis=0), axis=0)
    return (o.reshape(B, T, C_OUT_W),), tuple(new_bufs)


def _adaln(c, w, b):
    mod = jax.nn.silu(c) @ w + b
    return [m[:, None, :] for m in jnp.split(mod, 6, axis=-1)]


def _expand_mod(m, t):
    B, _, D = m.shape
    return jnp.broadcast_to(m, (B, t, D)).reshape(1, B * t, D)


def _layer(groups, mixers, norm_m, norm_f, w_in_bf16, in_splits, w_out_parts, moe_w):
    w_router, b_router, w_up, b_up, w_down, b_down = moe_w
    xs, states, routed = [], [], []
    counts = jnp.zeros((1, N_EXPERTS), F32)
    for (x, mods), mixer in zip(groups, mixers):
        sh_m, sc_m, g_m, sh_f, sc_f, g_f = mods
        parts = _mod_norm_proj(x, norm_m, sc_m, sh_m, w_in_bf16, in_splits)
        B, T, D = x.shape
        o_parts, state = mixer(parts)
        x = _proj_residual([o.reshape(B, T, -1) for o in o_parts], w_out_parts, x, g_m)
        h, route_i, route_g, counts = _mod_norm_router(x, norm_f, sc_f, sh_f, w_router, b_router, counts)
        xs.append(x)
        states.append(state)
        routed.append((h, route_i, route_g))
    n_asg = sum(x.shape[0] * x.shape[1] for x in xs) * TOP_K
    n_tiles = -(-(n_asg + N_EXPERTS * (MOE_TILE - 1)) // MOE_TILE)
    pad_start, pad_end, n_used = _moe_plan(counts[0])
    x_sorted = jnp.zeros((n_tiles * MOE_TILE, xs[0].shape[2]), F32)
    dests = []
    for h, route_i, _ in routed:
        expert = route_i[..., :TOP_K]
        rank = route_i[..., ROUTE_RANK_LANE:ROUTE_RANK_LANE + TOP_K]
        first = jnp.sum(jnp.where(expert[..., None] == jnp.arange(N_EXPERTS), pad_start, 0), axis=-1)
        dests.append((first + rank).astype(jnp.int32))
        x_sorted = _moe_dispatch(dests[-1], h, x_sorted)
    y_sorted = _moe_ffn(*_moe_tiles(pad_end, n_used, n_tiles), x_sorted, w_up, b_up, w_down, b_down)
    outs = [_moe_combine(dest, y_sorted, route_g, x, mods[5])
            for dest, (_, _, route_g), x, (_, mods) in zip(dests, routed, xs, groups)]
    return outs, states


def _stack(states, j):
    return jnp.stack([s[j] for s in states], axis=0)


def kernel(x_prompt, x_sample, cache_cmp_kv, cache_slc_kv, state_win_kv, state_dil0_kv, state_dil1_kv,
           state_dil2_kv, page_table, c_prompt, c_sample, norm_mix, norm_ffn, w_ada, b_ada, w_in_ab, w_out_ab,
           nsa_qk_gain, nsa_pe, nsa_w_phi, gm_ln_g, gm_ln_b, gm_ws, gm_bs, w_in_c, w_out_c, dil_qk_gain,
           w_router, b_router, w_up, b_up, w_down, b_down):
    depth = norm_mix.shape[0]
    past_len = page_table.shape[1] * PAGE_SIZE
    Bs, Ts, D = x_sample.shape
    x_p = x_prompt
    x_s = x_sample.reshape(1, Bs * Ts, D)
    ab_p, ab_s, dil_p, dil_s = [], [], [], []
    for layer in range(depth):
        i = layer // 2
        moe_w = (w_router[layer], b_router[layer], w_up[layer], b_up[layer], w_down[layer], b_down[layer])
        mods_p = _adaln(c_prompt, w_ada[layer], b_ada[layer])
        mods_s = [_expand_mod(m, Ts) for m in _adaln(c_sample, w_ada[layer], b_ada[layer])]
        groups = [(x_p, mods_p), (x_s, mods_s)]
        if layer % 2 == 0:
            wts = (nsa_qk_gain[i], nsa_pe[i], nsa_w_phi[i], gm_ln_g[i], gm_ln_b[i], gm_ws[i], gm_bs[i])
            flat_kv = lambda a: a.reshape(a.shape[:2] + (NSA_KV_W,))
            pasts_s = (flat_kv(cache_cmp_kv[i]), flat_kv(cache_slc_kv[i]), flat_kv(state_win_kv[i]))

            def mix_p(parts, wts=wts):
                return _mixer_ab(parts, 0, None, wts, None)

            def mix_s(parts, wts=wts, pasts_s=pasts_s):
                parts = [p.reshape(Bs, Ts, -1) for p in parts]
                return _mixer_ab(parts, past_len, pasts_s, wts, page_table)

            w_out = w_out_ab[i].astype(BF16)
            g0, g1 = AB_SPLITS[0], AB_SPLITS[0] + AB_SPLITS[1]
            w_in = w_in_ab[i].astype(BF16)
            w_in = jnp.concatenate([w_in[:, :g0], w_in[:, g1:], w_in[:, g0:g1]], axis=1)
            (x_p, x_s), (st_p, st_s) = _layer(
                groups, (mix_p, mix_s), norm_mix[layer], norm_ffn[layer], w_in, AB_SPLITS_GATE_LAST,
                (w_out[:NSA_Q_W], w_out[NSA_Q_W:]), moe_w)
            ab_p.append(st_p)
            ab_s.append(st_s)
        else:
            flat_kv = lambda a: a.reshape(a.shape[:2] + (2 * DIL_KV_W,))
            pasts_s = (flat_kv(state_dil0_kv[i]), flat_kv(state_dil1_kv[i]), flat_kv(state_dil2_kv[i]))

            def mix_p(parts, gain=dil_qk_gain[i]):
                return _mixer_c(parts, 0, None, gain)

            def mix_s(parts, pasts_s=pasts_s, gain=dil_qk_gain[i]):
                parts = [p.reshape(Bs, Ts, -1) for p in parts]
                return _mixer_c(parts, past_len, pasts_s, gain)

            (x_p, x_s), (st_p, st_s) = _layer(
                groups, (mix_p, mix_s), norm_mix[layer], norm_ffn[layer], w_in_c[i].astype(BF16),
                (DIL_Q_W, DIL_KV_W, DIL_KV_W) * len(DIL_CFG), (w_out_c[i].astype(BF16),), moe_w)
            dil_p.append(st_p)
            dil_s.append(st_s)
    x_s = x_s.reshape(Bs, Ts, D)
    cmp_p, cmp_s = _stack(ab_p, 0), _stack(ab_s, 0)
    slc_p, slc_s = _stack(ab_p, 1), _stack(ab_s, 1)
    win_p, win_s = _stack(ab_p, 2), _stack(ab_s, 2)
    gmv_p, gmv_s = _stack(ab_p, 3), _stack(ab_s, 3)
    d0_p, d0_s = _stack(dil_p, 0), _stack(dil_s, 0)
    d1_p, d1_s = _stack(dil_p, 1), _stack(dil_s, 1)
    d2_p, d2_s = _stack(dil_p, 2), _stack(dil_s, 2)
    return (x_p, x_s, cmp_p, cmp_s, slc_p, slc_s, win_p, win_s, gmv_p, gmv_s, d0_p, d0_s, d1_p, d1_s, d2_p, d2_s)
```

```python
import functools
import math

import jax
import jax.numpy as jnp
from jax import lax
from jax.experimental import pallas as pl
from jax.experimental.pallas import tpu as pltpu

F32 = jnp.float32
BF16 = jnp.bfloat16

D_MODEL = 1024
HEAD_DIM = 64
ROT_DIM = HEAD_DIM // 4
ROPE_THETA = 500000.0
NORM_EPS = 1e-6
NEG_INF = -1e30
Q_BLOCK = 128
PAGE_SIZE = 128

NSA_HEADS = D_MODEL // (2 * HEAD_DIM)
NSA_KV_HEADS = 2
NSA_GROUP = NSA_HEADS // NSA_KV_HEADS
NSA_BLOCK = 64
NSA_N_SEL = 16
NSA_WINDOW = 512
NSA_Q_W = NSA_HEADS * HEAD_DIM
NSA_KV_W = 2 * NSA_KV_HEADS * HEAD_DIM

GM_GROUPS = 8
GM_WIDTH = D_MODEL // 2
GM_GROUP_W = GM_WIDTH // GM_GROUPS
GM_CHUNK = 128

AB_SPLITS = (NSA_Q_W, 3 * NSA_HEADS, NSA_KV_W, NSA_KV_W, NSA_KV_W, GM_WIDTH, GM_WIDTH)

DIL_CFG = ((128, 1), (512, 4), (2048, 16))
DIL_HEADS = 8
DIL_KV_HEADS = 2
DIL_Q_W = DIL_HEADS * HEAD_DIM
DIL_KV_W = DIL_KV_HEADS * HEAD_DIM

N_EXPERTS = 32
TOP_K = 4
D_FF = D_MODEL
SWIGLU_ALPHA = 1.702
SWIGLU_LIMIT = 7.0

VMEM_LIMIT_BYTES = 56 * 1024 * 1024
ROW_TILE = 512
MOE_TILE = 512


def _row_tile(t):
    return t if t <= ROW_TILE else ROW_TILE


def _mod_spec(mod, tm):
    if mod.shape[1] == 1:
        return pl.BlockSpec((1, 1, mod.shape[2]), lambda b, i: (b, 0, 0))
    return pl.BlockSpec((1, tm, mod.shape[2]), lambda b, i: (b, i, 0))


def _split_bf16(v):
    hi = v.astype(BF16)
    lo = (v - hi.astype(F32)).astype(BF16)
    return hi, lo


def _mod_norm(x, g, sc, sh):
    h = x * lax.rsqrt(jnp.mean(x * x, axis=-1, keepdims=True) + NORM_EPS) * g
    return h * (1.0 + sc) + sh


def _proj_residual_kernel(*refs, n_parts):
    a_refs = refs[:n_parts]
    w_refs = refs[n_parts:2 * n_parts]
    x_ref, g_ref, o_ref = refs[2 * n_parts:]
    y = None
    for a_ref, w_ref in zip(a_refs, w_refs):
        p = jnp.dot(a_ref[0].astype(BF16), w_ref[...], preferred_element_type=F32)
        y = p if y is None else y + p
    o_ref[0] = x_ref[0] + g_ref[0] * y


def _proj_residual(a_parts, w_parts_bf16, x, gate):
    B, T, D = x.shape
    tm = _row_tile(T)
    n_parts = len(a_parts)
    in_specs = [pl.BlockSpec((1, tm, a.shape[2]), lambda b, i: (b, i, 0)) for a in a_parts]
    in_specs += [pl.BlockSpec(w.shape, lambda b, i: (0, 0)) for w in w_parts_bf16]
    in_specs += [pl.BlockSpec((1, tm, D), lambda b, i: (b, i, 0)), _mod_spec(gate, tm)]
    return pl.pallas_call(
        functools.partial(_proj_residual_kernel, n_parts=n_parts),
        out_shape=jax.ShapeDtypeStruct((B, T, D), F32),
        grid=(B, T // tm),
        in_specs=in_specs,
        out_specs=pl.BlockSpec((1, tm, D), lambda b, i: (b, i, 0)),
        compiler_params=pltpu.CompilerParams(
            dimension_semantics=("parallel", "parallel"), vmem_limit_bytes=VMEM_LIMIT_BYTES),
        name="proj_residual",
    )(*a_parts, *w_parts_bf16, x, gate)


ROUTE_RANK_LANE = TOP_K


def _mod_norm_router_kernel(x_ref, g_ref, sc_ref, sh_ref, whi_ref, wlo_ref, b_ref, cnt_in_ref,
                            h_ref, ri_ref, rg_ref, cnt_ref, *, tm):
    @pl.when((pl.program_id(0) == 0) & (pl.program_id(1) == 0))
    def _():
        cnt_ref[...] = cnt_in_ref[...]

    h = _mod_norm(x_ref[0], g_ref[...], sc_ref[0], sh_ref[0])
    h_ref[0] = h
    h_hi, h_lo = _split_bf16(h)
    w_hi = whi_ref[...]
    logits = (jnp.dot(h_hi, w_hi, preferred_element_type=F32)
              + jnp.dot(h_lo, w_hi, preferred_element_type=F32)
              + jnp.dot(h_hi, wlo_ref[...], preferred_element_type=F32)) + b_ref[...]
    n_exp = logits.shape[-1]
    lane = lax.broadcasted_iota(jnp.int32, (1, n_exp), 1)
    work = logits
    vals, ids, hits = [], [], []
    for _ in range(TOP_K):
        m = jnp.max(work, axis=-1, keepdims=True)
        idx = jnp.min(jnp.where(work == m, lane, n_exp), axis=-1, keepdims=True)
        hit = lane == idx
        vals.append(m)
        ids.append(idx)
        hits.append(hit)
        work = jnp.where(hit, NEG_INF, work)
    exps = [jnp.exp(v - vals[0]) for v in vals]
    denom = exps[0] + exps[1] + exps[2] + exps[3]
    onehot = jnp.zeros(logits.shape, F32)
    for hit in hits:
        onehot = onehot + jnp.where(hit, 1.0, 0.0)
    row = lax.broadcasted_iota(jnp.int32, (tm, tm), 0)
    col = lax.broadcasted_iota(jnp.int32, (tm, tm), 1)
    earlier = jnp.where(row > col, 1.0, 0.0).astype(BF16)
    before = jnp.dot(earlier, onehot.astype(BF16), preferred_element_type=F32) + cnt_ref[...]
    cnt_ref[...] = cnt_ref[...] + jnp.sum(onehot, axis=0, keepdims=True)
    out_lane = lax.broadcasted_iota(jnp.int32, (1, LANES), 1)
    ri = jnp.zeros((tm, LANES), jnp.int32)
    rg = jnp.zeros((tm, LANES), F32)
    for k in range(TOP_K):
        rank = jnp.sum(jnp.where(hits[k], before, 0.0), axis=-1, keepdims=True).astype(jnp.int32)
        ri = jnp.where(out_lane == k, ids[k], ri)
        ri = jnp.where(out_lane == ROUTE_RANK_LANE + k, rank, ri)
        rg = jnp.where(out_lane == k, exps[k] / denom, rg)
    ri_ref[0] = ri
    rg_ref[0] = rg


def _mod_norm_router(x, norm_g, scale, shift, w_router, b_router, counts_in):
    B, T, D = x.shape
    tm = _row_tile(T)
    E = w_router.shape[1]
    w_hi, w_lo = _split_bf16(w_router)
    tok = lambda n: pl.BlockSpec((1, tm, n), lambda b, i: (b, i, 0))
    return pl.pallas_call(
        functools.partial(_mod_norm_router_kernel, tm=tm),
        out_shape=[jax.ShapeDtypeStruct((B, T, D), F32), jax.ShapeDtypeStruct((B, T, LANES), jnp.int32),
                   jax.ShapeDtypeStruct((B, T, LANES), F32), jax.ShapeDtypeStruct((1, E), F32)],
        grid=(B, T // tm),
        in_specs=[
            tok(D),
            pl.BlockSpec((1, D), lambda b, i: (0, 0)),
            _mod_spec(scale, tm),
            _mod_spec(shift, tm),
            pl.BlockSpec((D, E), lambda b, i: (0, 0)),
            pl.BlockSpec((D, E), lambda b, i: (0, 0)),
            pl.BlockSpec((1, E), lambda b, i: (0, 0)),
            pl.BlockSpec((1, E), lambda b, i: (0, 0)),
        ],
        out_specs=[tok(D), tok(LANES), tok(LANES), pl.BlockSpec((1, E), lambda b, i: (0, 0))],
        compiler_params=pltpu.CompilerParams(
            dimension_semantics=("arbitrary", "arbitrary"), vmem_limit_bytes=VMEM_LIMIT_BYTES),
        name="mod_norm_router",
    )(x, norm_g.reshape(1, D), scale, shift, w_hi, w_lo, b_router.reshape(1, E), counts_in)


def _moe_ffn_kernel(te_ref, tx_ref, tv_ref, x_ref, wu_ref, bu_ref, wd_ref, bd_ref, o_ref):
    i = pl.program_id(0)

    @pl.when(tv_ref[i] != 0)
    def _():
        hu = jnp.dot(x_ref[...].astype(BF16), wu_ref[0].astype(BF16), preferred_element_type=F32) + bu_ref[0]
        glu = jnp.minimum(hu[:, :D_FF], SWIGLU_LIMIT)
        lin = jnp.clip(hu[:, D_FF:], -SWIGLU_LIMIT, SWIGLU_LIMIT)
        act = glu * jax.nn.sigmoid(SWIGLU_ALPHA * glu) * (lin + 1.0)
        o_ref[...] = jnp.dot(act.astype(BF16), wd_ref[0].astype(BF16), preferred_element_type=F32) + bd_ref[0]

    @pl.when(tv_ref[i] == 0)
    def _():
        o_ref[...] = jnp.zeros_like(o_ref)


def _moe_ffn(tile_exp, tile_x, tile_valid, x_sorted, w_up, b_up, w_down, b_down):
    n_slots, D = x_sorted.shape
    n_tiles = n_slots // MOE_TILE
    E, _, F2 = w_up.shape
    grid_spec = pltpu.PrefetchScalarGridSpec(
        num_scalar_prefetch=3,
        grid=(n_tiles,),
        in_specs=[
            pl.BlockSpec((MOE_TILE, D), lambda i, te, tx, tv: (tx[i], 0)),
            pl.BlockSpec((1, D, F2), lambda i, te, tx, tv: (te[i], 0, 0)),
            pl.BlockSpec((1, 1, F2), lambda i, te, tx, tv: (te[i], 0, 0)),
            pl.BlockSpec((1, F2 // 2, D), lambda i, te, tx, tv: (te[i], 0, 0)),
            pl.BlockSpec((1, 1, D), lambda i, te, tx, tv: (te[i], 0, 0)),
        ],
        out_specs=pl.BlockSpec((MOE_TILE, D), lambda i, te, tx, tv: (i, 0)),
    )
    return pl.pallas_call(
        _moe_ffn_kernel,
        out_shape=jax.ShapeDtypeStruct((n_slots, D), F32),
        grid_spec=grid_spec,
        compiler_params=pltpu.CompilerParams(
            dimension_semantics=("arbitrary",), vmem_limit_bytes=VMEM_LIMIT_BYTES),
        name="moe_ffn",
    )(tile_exp, tile_x, tile_valid, x_sorted, w_up, b_up.reshape(E, 1, F2), w_down, b_down.reshape(E, 1, D))


DISPATCH_TILE = 256
COMBINE_TILE = 128


def _row_copy(src_ref, src_row, dst_ref, dst_row, sem):
    return pltpu.make_async_copy(src_ref.at[pl.ds(src_row, 1)], dst_ref.at[pl.ds(dst_row, 1)], sem)


def _moe_dispatch_kernel(dest_ref, h_ref, xs_in_ref, xs_ref, sem, *, tm):
    del xs_in_ref
    h_rows = h_ref.at[0]

    def issue(t, carry):
        for k in range(TOP_K):
            _row_copy(h_rows, t, xs_ref, dest_ref[0, 0, 0, t * TOP_K + k], sem).start()
        return carry

    def drain(t, carry):
        for k in range(TOP_K):
            _row_copy(h_rows, t, xs_ref, dest_ref[0, 0, 0, t * TOP_K + k], sem).wait()
        return carry

    lax.fori_loop(0, tm, issue, 0)
    lax.fori_loop(0, tm, drain, 0)


def _dest_spec(tm):
    return pl.BlockSpec((1, 1, 1, tm * TOP_K), lambda b, i: (b, i, 0, 0), memory_space=pltpu.SMEM)


def _moe_dispatch(dest, h, x_sorted):
    B, T, D = h.shape
    tm = min(T, DISPATCH_TILE)
    return pl.pallas_call(
        functools.partial(_moe_dispatch_kernel, tm=tm),
        out_shape=jax.ShapeDtypeStruct(x_sorted.shape, x_sorted.dtype),
        grid=(B, T // tm),
        in_specs=[_dest_spec(tm),
                  pl.BlockSpec((1, tm, D), lambda b, i: (b, i, 0)),
                  pl.BlockSpec(memory_space=pl.ANY)],
        out_specs=pl.BlockSpec(memory_space=pl.ANY),
        scratch_shapes=[pltpu.SemaphoreType.DMA(())],
        input_output_aliases={2: 0},
        compiler_params=pltpu.CompilerParams(dimension_semantics=("arbitrary", "arbitrary")),
        name="moe_dispatch",
    )(dest.reshape(B, T // tm, 1, tm * TOP_K), h, x_sorted)


def _moe_combine_kernel(dest_ref, ys_ref, rg_ref, x_ref, gf_ref, o_ref, buf, sem, *, tm):
    def issue(t, carry):
        for k in range(TOP_K):
            _row_copy(ys_ref, dest_ref[0, 0, 0, t * TOP_K + k], buf.at[k], t, sem).start()
        return carry

    def drain(t, carry):
        for k in range(TOP_K):
            _row_copy(ys_ref, dest_ref[0, 0, 0, t * TOP_K + k], buf.at[k], t, sem).wait()
        return carry

    lax.fori_loop(0, tm, issue, 0)
    lax.fori_loop(0, tm, drain, 0)
    gates = rg_ref[0]
    y = gates[:, 0:1] * buf[0]
    for k in range(1, TOP_K):
        y = y + gates[:, k:k + 1] * buf[k]
    o_ref[0] = x_ref[0] + gf_ref[0] * y


def _moe_combine(dest, y_sorted, route_g, x, gate_f):
    B, T, D = x.shape
    tm = min(T, COMBINE_TILE)
    tok = lambda n: pl.BlockSpec((1, tm, n), lambda b, i: (b, i, 0))
    return pl.pallas_call(
        functools.partial(_moe_combine_kernel, tm=tm),
        out_shape=jax.ShapeDtypeStruct((B, T, D), F32),
        grid=(B, T // tm),
        in_specs=[_dest_spec(tm), pl.BlockSpec(memory_space=pl.ANY), tok(LANES), tok(D), _mod_spec(gate_f, tm)],
        out_specs=tok(D),
        scratch_shapes=[pltpu.VMEM((TOP_K, tm, D), F32), pltpu.SemaphoreType.DMA(())],
        compiler_params=pltpu.CompilerParams(
            dimension_semantics=("parallel", "parallel"), vmem_limit_bytes=VMEM_LIMIT_BYTES),
        name="moe_combine",
    )(dest.reshape(B, T // tm, 1, tm * TOP_K), y_sorted, route_g, x, gate_f)


def _moe_plan(counts):
    counts = counts.astype(jnp.int32)
    padded = (counts + MOE_TILE - 1) // MOE_TILE * MOE_TILE
    pad_end = jnp.cumsum(padded)
    pad_start = pad_end - padded
    n_used = pad_end[-1] // MOE_TILE
    return pad_start, pad_end, n_used


def _moe_tiles(pad_end, n_used, n_tiles):
    tile = jnp.arange(n_tiles, dtype=jnp.int32)
    tile_valid = (tile < n_used).astype(jnp.int32)
    tile_x = jnp.minimum(tile, n_used - 1)
    tile_exp = jnp.sum((tile_x[:, None] * MOE_TILE >= pad_end[None, :]).astype(jnp.int32), axis=1)
    return jnp.minimum(tile_exp, N_EXPERTS - 1), tile_x, tile_valid


N_QHEADS = 8
LANES = 128
HALF = HEAD_DIM


def _nt_dot(a, b):
    return lax.dot_general(a, b, (((1,), (1,)), ((), ())), preferred_element_type=F32)


def _band_mask(qpos, kpos, window, dilation):
    delta = qpos - kpos
    valid = (delta >= 0) & (delta <= window)
    if dilation > 1:
        valid = valid & ((delta & (dilation - 1)) == 0)
    return valid


def _softmax_pv(s, valid, v, tq):
    n = s.shape[-1]
    s = jnp.where(valid[None], s.reshape(N_QHEADS, tq, n), NEG_INF)
    m = jnp.max(s, axis=-1, keepdims=True)
    p = jnp.exp(s - m)
    l = jnp.sum(p, axis=-1, keepdims=True)
    o = jnp.dot(p.reshape(N_QHEADS * tq, n).astype(BF16), v, preferred_element_type=F32)
    return o.reshape(N_QHEADS, tq, LANES) / l, m + jnp.log(l)


def _store_heads(o_ref, o, lse):
    if lse is None:
        o_ref[0] = o
        return
    lane = lax.broadcasted_iota(jnp.int32, (1, LANES), 1)
    for h in range(N_QHEADS):
        own = (lane < HALF) if h < N_QHEADS // 2 else (lane >= HALF)
        o_ref[0, h] = jnp.where(own, o[h], lse[h])


def _band_self_kernel(q_ref, k_ref, v_ref, o_ref, *, window, tq, span, want_lse):
    T = k_ref.shape[1]
    q0 = pl.program_id(1) * tq
    if span == T:
        start = 0
        k = k_ref[0]
        v = v_ref[0]
    else:
        start = pl.multiple_of(jnp.maximum(q0 - window, 0), LANES)
        k = k_ref[0, pl.ds(start, span), :]
        v = v_ref[0, pl.ds(start, span), :]
    q = q_ref[0].reshape(N_QHEADS * tq, LANES).astype(BF16)
    qpos = q0 + lax.broadcasted_iota(jnp.int32, (tq, 1), 0)
    kpos = start + lax.broadcasted_iota(jnp.int32, (1, span), 1)
    o, lse = _softmax_pv(_nt_dot(q, k), _band_mask(qpos, kpos, window, 1), v, tq)
    _store_heads(o_ref, o, lse if want_lse else None)


def _band_self(q, k, v, window, want_lse):
    B, _, T, _ = q.shape
    tq = min(T, Q_BLOCK)
    span = min(T, window + tq)
    return pl.pallas_call(
        functools.partial(_band_self_kernel, window=window, tq=tq, span=span, want_lse=want_lse),
        out_shape=jax.ShapeDtypeStruct((B, N_QHEADS, T, LANES), F32),
        grid=(B, T // tq),
        in_specs=[
            pl.BlockSpec((1, N_QHEADS, tq, LANES), lambda b, i: (b, 0, i, 0)),
            pl.BlockSpec((1, T, LANES), lambda b, i: (b, 0, 0)),
            pl.BlockSpec((1, T, LANES), lambda b, i: (b, 0, 0)),
        ],
        out_specs=pl.BlockSpec((1, N_QHEADS, tq, LANES), lambda b, i: (b, 0, i, 0)),
        compiler_params=pltpu.CompilerParams(
            dimension_semantics=("parallel", "parallel"), vmem_limit_bytes=VMEM_LIMIT_BYTES),
        name="band_self",
    )(q, k, v)


def _band_tail_kernel(q_ref, kv_ref, o_ref, *, window, dilation, tq, want_lse):
    L = kv_ref.shape[1]
    kv = kv_ref[0]
    k = kv[:, :LANES].astype(BF16)
    v = kv[:, LANES:].astype(BF16)
    q = q_ref[0].reshape(N_QHEADS * tq, LANES).astype(BF16)
    qpos = (L - tq) + lax.broadcasted_iota(jnp.int32, (tq, 1), 0)
    kpos = lax.broadcasted_iota(jnp.int32, (1, L), 1)
    o, lse = _softmax_pv(_nt_dot(q, k), _band_mask(qpos, kpos, window, dilation), v, tq)
    _store_heads(o_ref, o, lse if want_lse else None)


def _band_tail(q, kv_all, window, dilation, want_lse):
    B, _, T, _ = q.shape
    L = kv_all.shape[1]
    return pl.pallas_call(
        functools.partial(_band_tail_kernel, window=window, dilation=dilation, tq=T, want_lse=want_lse),
        out_shape=jax.ShapeDtypeStruct((B, N_QHEADS, T, LANES), F32),
        grid=(B,),
        in_specs=[
            pl.BlockSpec((1, N_QHEADS, T, LANES), lambda b: (b, 0, 0, 0)),
            pl.BlockSpec((1, L, 2 * LANES), lambda b: (b, 0, 0)),
        ],
        out_specs=pl.BlockSpec((1, N_QHEADS, T, LANES), lambda b: (b, 0, 0, 0)),
        compiler_params=pltpu.CompilerParams(
            dimension_semantics=("parallel",), vmem_limit_bytes=VMEM_LIMIT_BYTES),
        name="band_tail",
    )(q, kv_all)


def _block_mean_kernel(x_ref, o_ref):
    rows = x_ref.shape[1]
    x = x_ref[0].reshape(rows // NSA_BLOCK, NSA_BLOCK, x_ref.shape[2])
    o_ref[0] = jnp.sum(x, axis=1) * (1.0 / NSA_BLOCK)


def _block_mean(kvc):
    B, T, W = kvc.shape
    tm = _row_tile(T)
    return pl.pallas_call(
        _block_mean_kernel,
        out_shape=jax.ShapeDtypeStruct((B, T // NSA_BLOCK, W), F32),
        grid=(B, T // tm),
        in_specs=[pl.BlockSpec((1, tm, W), lambda b, i: (b, i, 0))],
        out_specs=pl.BlockSpec((1, tm // NSA_BLOCK, W), lambda b, i: (b, i, 0)),
        compiler_params=pltpu.CompilerParams(dimension_semantics=("parallel", "parallel")),
        name="block_mean",
    )(kvc)


PAGES_PER_STEP = 8


def _page_specs(n_cols):
    return [pl.BlockSpec((1, PAGE_SIZE, n_cols), functools.partial(
        lambda b, s, pt, pg: (pt[b, s * PAGES_PER_STEP + pg], 0, 0), pg=pg)) for pg in range(PAGES_PER_STEP)]


def _page_block_mean_kernel(pt_ref, *refs):
    page_refs, o_ref = refs[:-1], refs[-1]
    per_page = PAGE_SIZE // NSA_BLOCK
    for pg, page_ref in enumerate(page_refs):
        x = page_ref[0].reshape(per_page, NSA_BLOCK, page_ref.shape[2])
        o_ref[0, pg * per_page:(pg + 1) * per_page, :] = jnp.sum(x, axis=1) * (1.0 / NSA_BLOCK)


def _page_block_mean(cache, page_table):
    B, n_pages = page_table.shape
    W = cache.shape[2]
    per_step = PAGES_PER_STEP * PAGE_SIZE // NSA_BLOCK
    grid_spec = pltpu.PrefetchScalarGridSpec(
        num_scalar_prefetch=1,
        grid=(B, n_pages // PAGES_PER_STEP),
        in_specs=_page_specs(W),
        out_specs=pl.BlockSpec((1, per_step, W), lambda b, s, pt: (b, s, 0)),
    )
    return pl.pallas_call(
        _page_block_mean_kernel,
        out_shape=jax.ShapeDtypeStruct((B, n_pages * PAGE_SIZE // NSA_BLOCK, W), F32),
        grid_spec=grid_spec,
        compiler_params=pltpu.CompilerParams(dimension_semantics=("parallel", "arbitrary")),
        name="page_block_mean",
    )(page_table, *([cache] * PAGES_PER_STEP))


def _head_mean_sq(x, ones_bd):
    hi, lo = _split_bf16(x * x)
    return jnp.dot(hi, ones_bd, preferred_element_type=F32) + jnp.dot(lo, ones_bd, preferred_element_type=F32)


def _cmp_finish_kernel(mean_ref, pe_ref, wk_ref, wv_ref, gain_ref, bd_ref, kc_ref, vc_ref):
    s = mean_ref[0] + pe_ref[...]
    sk = jnp.dot(s[:, :LANES].astype(BF16), wk_ref[...], preferred_element_type=F32)
    vc_ref[0] = jnp.dot(s[:, LANES:].astype(BF16), wv_ref[...], preferred_element_type=F32)
    kc_ref[0] = sk * lax.rsqrt(_head_mean_sq(sk, bd_ref[...]) + NORM_EPS) * gain_ref[...]


def _block_diag2(w):
    z = jnp.zeros_like(w)
    return jnp.concatenate([jnp.concatenate([w, z], axis=1), jnp.concatenate([z, w], axis=1)], axis=0)


def _cmp_finish(mean, pe, w_phi, kc_gain):
    B, n_cb, W = mean.shape
    pe_mean = jnp.mean(pe, axis=0)
    pe_row = jnp.concatenate([pe_mean[0], pe_mean[0], pe_mean[1], pe_mean[1]])[None, :]
    wk = _block_diag2(w_phi[0]).astype(BF16)
    wv = _block_diag2(w_phi[1]).astype(BF16)
    bd = _block_diag2(jnp.full((HALF, HALF), 1.0 / HALF, F32)).astype(BF16)
    gain = jnp.concatenate([kc_gain, kc_gain])[None, :]
    full = lambda shape: pl.BlockSpec(shape, lambda b: (0,) * len(shape))
    return pl.pallas_call(
        _cmp_finish_kernel,
        out_shape=[jax.ShapeDtypeStruct((B, n_cb, LANES), F32)] * 2,
        grid=(B,),
        in_specs=[pl.BlockSpec((1, n_cb, W), lambda b: (b, 0, 0)), full((1, W)), full((LANES, LANES)),
                  full((LANES, LANES)), full((1, LANES)), full((LANES, LANES))],
        out_specs=[pl.BlockSpec((1, n_cb, LANES), lambda b: (b, 0, 0))] * 2,
        compiler_params=pltpu.CompilerParams(dimension_semantics=("parallel",)),
        name="cmp_finish",
    )(mean, pe_row, wk, wv, gain, bd)


def _cmp_select_kernel(q_ref, kc_ref, vc_ref, ocmp_ref, sel_ref, *, tq, pos0):
    n_cb = kc_ref.shape[1]
    q0 = pl.program_id(1) * tq
    qh, ql = _split_bf16(q_ref[0].reshape(N_QHEADS * tq, LANES))
    kh, kl = _split_bf16(kc_ref[0])
    s = (_nt_dot(qh, kh) + _nt_dot(ql, kh) + _nt_dot(qh, kl)) * (HEAD_DIM ** -0.5)
    cur = (pos0 + q0 + lax.broadcasted_iota(jnp.int32, (tq, 1), 0)) // NSA_BLOCK
    blk = lax.broadcasted_iota(jnp.int32, (1, n_cb), 1)
    vis = blk < cur
    s = jnp.where(vis[None], s.reshape(N_QHEADS, tq, n_cb), NEG_INF)
    p = jnp.exp(s - jnp.max(s, axis=-1, keepdims=True))
    p = jnp.where(vis[None], p / jnp.sum(p, axis=-1, keepdims=True), 0.0)
    o = jnp.dot(p.reshape(N_QHEADS * tq, n_cb).astype(BF16), vc_ref[0].astype(BF16), preferred_element_type=F32)
    ocmp_ref[0] = o.reshape(N_QHEADS, tq, LANES)
    for kvh in range(NSA_KV_HEADS):
        g0 = kvh * NSA_GROUP
        imp = jnp.where(vis, p[g0] + p[g0 + 1] + p[g0 + 2] + p[g0 + 3], -1.0)
        rank = jnp.zeros((tq, n_cb), jnp.int32)
        for i in range(n_cb):
            col = imp[:, i:i + 1]
            wins_tie = jnp.where(blk > i, 1, 0)
            rank = rank + jnp.where(col > imp, 1, jnp.where(col == imp, wins_tie, 0))
        chosen = jnp.where(vis, jnp.where(rank < NSA_N_SEL - 1, 1.0, 0.0), jnp.where(blk == cur, 1.0, 0.0))
        sel_ref[0, kvh] = chosen.astype(BF16)


def _cmp_select(qn, kc, vc, pos0):
    B, _, T, _ = qn.shape
    n_cb = kc.shape[1]
    tq = min(T, Q_BLOCK)
    return pl.pallas_call(
        functools.partial(_cmp_select_kernel, tq=tq, pos0=pos0),
        out_shape=[jax.ShapeDtypeStruct((B, N_QHEADS, T, LANES), F32),
                   jax.ShapeDtypeStruct((B, NSA_KV_HEADS, T, n_cb), BF16)],
        grid=(B, T // tq),
        in_specs=[
            pl.BlockSpec((1, N_QHEADS, tq, LANES), lambda b, i: (b, 0, i, 0)),
            pl.BlockSpec((1, n_cb, LANES), lambda b, i: (b, 0, 0)),
            pl.BlockSpec((1, n_cb, LANES), lambda b, i: (b, 0, 0)),
        ],
        out_specs=[pl.BlockSpec((1, N_QHEADS, tq, LANES), lambda b, i: (b, 0, i, 0)),
                   pl.BlockSpec((1, NSA_KV_HEADS, tq, n_cb), lambda b, i: (b, 0, i, 0))],
        compiler_params=pltpu.CompilerParams(
            dimension_semantics=("parallel", "parallel"), vmem_limit_bytes=VMEM_LIMIT_BYTES),
        name="cmp_select",
    )(qn, kc, vc)


def _selection_bias(sel, first_key, n_keys, causal):
    n_cb = sel.shape[-1]
    key_blk = (first_key + lax.broadcasted_iota(jnp.int32, (1, n_keys), 1)) // NSA_BLOCK
    expand = jnp.where(lax.broadcasted_iota(jnp.int32, (n_cb, 1), 0) == key_blk, 1.0, 0.0).astype(BF16)
    out = []
    for kvh in range(NSA_KV_HEADS):
        picked = jnp.dot(sel[kvh], expand, preferred_element_type=F32)
        bias = (picked - 1.0) * 1e30
        if causal is not None:
            bias = jnp.where(causal, bias, NEG_INF)
        out.append(jnp.broadcast_to(bias[None], (NSA_GROUP,) + bias.shape))
    return jnp.concatenate(out, axis=0)


def _online_softmax_step(s, v, m_ref, l_ref, acc_ref, tq):
    n = s.shape[-1]
    m_old = m_ref[...]
    m_new = jnp.maximum(m_old, jnp.max(s, axis=-1, keepdims=True))
    alpha = jnp.exp(m_old - m_new)
    p = jnp.exp(s - m_new[..., :1])
    l_ref[...] = alpha * l_ref[...] + jnp.sum(p, axis=-1, keepdims=True)
    pv = jnp.dot(p.reshape(N_QHEADS * tq, n).astype(BF16), v, preferred_element_type=F32)
    acc_ref[...] = alpha * acc_ref[...] + pv.reshape(N_QHEADS, tq, LANES)
    m_ref[...] = m_new


def _init_softmax_state(m_ref, l_ref, acc_ref):
    m_ref[...] = jnp.full(m_ref.shape, NEG_INF, F32)
    l_ref[...] = jnp.zeros(l_ref.shape, F32)
    acc_ref[...] = jnp.zeros(acc_ref.shape, F32)


def _gated_heads(gate_pre, o_cmp, o_slc, o_win):
    g = jax.nn.sigmoid(gate_pre)
    lane = lax.broadcasted_iota(jnp.int32, (1, LANES), 1)
    pairs = []
    for c in range(N_QHEADS // 2):
        mixed = []
        for h in (2 * c, 2 * c + 1):
            mixed.append(g[:, 3 * h:3 * h + 1] * o_cmp[h] + g[:, 3 * h + 1:3 * h + 2] * o_slc[h]
                         + g[:, 3 * h + 2:3 * h + 3] * o_win[h])
        if 2 * c < N_QHEADS // 2:
            pairs.append(jnp.where(lane < HALF, mixed[0], pltpu.roll(mixed[1], HALF, 1)))
        else:
            pairs.append(jnp.where(lane < HALF, pltpu.roll(mixed[0], HALF, 1), mixed[1]))
    return jnp.concatenate(pairs, axis=-1)


SLC_KEY_TILE = 512


def _nsa_prompt_kernel(q_ref, ks_ref, vs_ref, kw_ref, vw_ref, sel_ref, ocmp_ref, g_ref, o_ref,
                       m_ref, l_ref, acc_ref, *, tq, tk, span):
    T = ks_ref.shape[1]
    q0 = pl.program_id(1) * tq
    q = q_ref[0].reshape(N_QHEADS * tq, LANES)
    qpos = q0 + lax.broadcasted_iota(jnp.int32, (tq, 1), 0)
    sel = sel_ref[0]
    _init_softmax_state(m_ref, l_ref, acc_ref)

    def key_tile(j, carry):
        k0 = pl.multiple_of(j * tk, tk)
        kpos = k0 + lax.broadcasted_iota(jnp.int32, (1, tk), 1)
        s = _nt_dot(q, ks_ref[0, pl.ds(k0, tk), :]).reshape(N_QHEADS, tq, tk)
        s = s + _selection_bias(sel, k0, tk, kpos <= qpos)
        _online_softmax_step(s, vs_ref[0, pl.ds(k0, tk), :], m_ref, l_ref, acc_ref, tq)
        return carry

    lax.fori_loop(0, (q0 + tq + tk - 1) // tk, key_tile, 0)
    o_slc = acc_ref[...] / l_ref[...]

    if span == T:
        start = 0
        kw, vw = kw_ref[0], vw_ref[0]
    else:
        start = pl.multiple_of(jnp.maximum(q0 - NSA_WINDOW, 0), LANES)
        kw, vw = kw_ref[0, pl.ds(start, span), :], vw_ref[0, pl.ds(start, span), :]
    kpos = start + lax.broadcasted_iota(jnp.int32, (1, span), 1)
    o_win, _ = _softmax_pv(_nt_dot(q, kw), _band_mask(qpos, kpos, NSA_WINDOW, 1), vw, tq)
    o_ref[0] = _gated_heads(g_ref[0], ocmp_ref[0], o_slc, o_win)


def _nsa_prompt(qr, ks, vs, kw, vw, sel, ocmp, gate_pre):
    B, _, T, _ = qr.shape
    n_cb = sel.shape[-1]
    tq = min(T, Q_BLOCK)
    tk = min(T, SLC_KEY_TILE)
    span = min(T, NSA_WINDOW + tq)
    seq = pl.BlockSpec((1, T, LANES), lambda b, i: (b, 0, 0))
    heads = pl.BlockSpec((1, N_QHEADS, tq, LANES), lambda b, i: (b, 0, i, 0))
    stat = pltpu.VMEM((N_QHEADS, tq, LANES), F32)
    return pl.pallas_call(
        functools.partial(_nsa_prompt_kernel, tq=tq, tk=tk, span=span),
        out_shape=jax.ShapeDtypeStruct((B, T, NSA_Q_W), F32),
        grid=(B, T // tq),
        in_specs=[heads, seq, seq, seq, seq,
                  pl.BlockSpec((1, NSA_KV_HEADS, tq, n_cb), lambda b, i: (b, 0, i, 0)),
                  heads,
                  pl.BlockSpec((1, tq, gate_pre.shape[2]), lambda b, i: (b, i, 0))],
        out_specs=pl.BlockSpec((1, tq, NSA_Q_W), lambda b, i: (b, i, 0)),
        scratch_shapes=[stat, stat, stat],
        compiler_params=pltpu.CompilerParams(
            dimension_semantics=("parallel", "arbitrary"), vmem_limit_bytes=VMEM_LIMIT_BYTES),
        name="nsa_prompt",
    )(qr, ks, vs, kw, vw, sel, ocmp, gate_pre)


def _nsa_paged_kernel(pt_ref, q_ref, *refs, tq, pos0):
    page_refs = refs[:PAGES_PER_STEP]
    new_ref, sel_ref, ocmp_ref, owin_ref, g_ref, o_ref, m_ref, l_ref, acc_ref = refs[PAGES_PER_STEP:]
    step = pl.program_id(1)
    q = q_ref[0].reshape(N_QHEADS * tq, LANES).astype(BF16)
    sel = sel_ref[0]

    @pl.when(step == 0)
    def _():
        _init_softmax_state(m_ref, l_ref, acc_ref)

    def attend(k, v, first_key, causal):
        n = k.shape[0]
        s = _nt_dot(q, k).reshape(N_QHEADS, tq, n)
        if causal is None:
            s = s + _selection_bias(sel, first_key, n, None)
        else:
            s = jnp.where(causal[None], s, NEG_INF)
        _online_softmax_step(s, v, m_ref, l_ref, acc_ref, tq)

    attend(jnp.concatenate([r[0, :, :LANES].astype(BF16) for r in page_refs], axis=0),
           jnp.concatenate([r[0, :, LANES:].astype(BF16) for r in page_refs], axis=0),
           step * (PAGES_PER_STEP * PAGE_SIZE), None)

    @pl.when(step == pl.num_programs(1) - 1)
    def _():
        qpos = lax.broadcasted_iota(jnp.int32, (tq, 1), 0)
        kpos = lax.broadcasted_iota(jnp.int32, (1, PAGE_SIZE), 1)
        new = new_ref[0]
        attend(new[:, :LANES].astype(BF16), new[:, LANES:].astype(BF16), pos0, kpos <= qpos)
        o_ref[0] = _gated_heads(g_ref[0], ocmp_ref[0], acc_ref[...] / l_ref[...], owin_ref[0])


def _nsa_paged(page_table, qr, cache, new_tile, sel, ocmp, owin, gate_pre, pos0):
    B, _, T, _ = qr.shape
    n_pages = page_table.shape[1]
    n_cb = sel.shape[-1]
    W = cache.shape[2]
    heads = pl.BlockSpec((1, N_QHEADS, T, LANES), lambda b, s, pt: (b, 0, 0, 0))
    stat = pltpu.VMEM((N_QHEADS, T, LANES), F32)
    grid_spec = pltpu.PrefetchScalarGridSpec(
        num_scalar_prefetch=1,
        grid=(B, n_pages // PAGES_PER_STEP),
        in_specs=[heads] + _page_specs(W) + [
            pl.BlockSpec((1, PAGE_SIZE, W), lambda b, s, pt: (b, 0, 0)),
            pl.BlockSpec((1, NSA_KV_HEADS, T, n_cb), lambda b, s, pt: (b, 0, 0, 0)),
            heads, heads,
            pl.BlockSpec((1, T, gate_pre.shape[2]), lambda b, s, pt: (b, 0, 0))],
        out_specs=pl.BlockSpec((1, T, NSA_Q_W), lambda b, s, pt: (b, 0, 0)),
        scratch_shapes=[stat, stat, stat],
    )
    return pl.pallas_call(
        functools.partial(_nsa_paged_kernel, tq=T, pos0=pos0),
        out_shape=jax.ShapeDtypeStruct((B, T, NSA_Q_W), F32),
        grid_spec=grid_spec,
        compiler_params=pltpu.CompilerParams(
            dimension_semantics=("parallel", "arbitrary"), vmem_limit_bytes=VMEM_LIMIT_BYTES),
        name="nsa_paged",
    )(page_table, qr, *([cache] * PAGES_PER_STEP), new_tile, sel, ocmp, owin, gate_pre)


QPAD_W = N_QHEADS * LANES


def _rope_tables(pos):
    half = ROT_DIM // 2
    inv = jnp.exp(-math.log(ROPE_THETA) * jnp.arange(half, dtype=F32) * (2.0 / ROT_DIM))
    ang = pos.astype(F32)[:, None] * inv[None, :]
    cos, sin = jnp.cos(ang), jnp.sin(ang)
    T = pos.shape[0]
    zeros = lambda n: jnp.zeros((T, n), F32)
    cos_t = jnp.concatenate([cos, cos, jnp.ones((T, HALF - ROT_DIM), F32)], axis=-1)
    msin_t = jnp.concatenate([-sin, zeros(HALF - half)], axis=-1)
    psin_t = jnp.concatenate([zeros(half), sin, zeros(HALF - ROT_DIM)], axis=-1)
    twice = lambda a: jnp.concatenate([a, a], axis=-1)
    return twice(cos_t), twice(msin_t), twice(psin_t)


def _rope_lanes(x, cos_t, msin_t, psin_t):
    half = ROT_DIM // 2
    return x * cos_t + pltpu.roll(x, LANES - half, 1) * msin_t + pltpu.roll(x, half, 1) * psin_t


def _head_rms(x, ones_mat, gain):
    return x * lax.rsqrt(_head_mean_sq(x, ones_mat) + NORM_EPS) * gain


def _pad_head_columns(w):
    D = w.shape[0]
    w = w.reshape(D, N_QHEADS, HALF)
    z = jnp.zeros_like(w)
    lower = (jnp.arange(N_QHEADS) < N_QHEADS // 2)[None, :, None]
    return jnp.concatenate([jnp.where(lower, w, z), jnp.where(lower, z, w)], axis=-1).reshape(D, QPAD_W)


def _pad_head_vector(g):
    return _pad_head_columns(jnp.tile(g, N_QHEADS)[None, :])


def _ab_proj_kernel(x_ref, g_ref, sc_ref, sh_ref, w_ref, cos_ref, msin_ref, psin_ref, gq_ref, gks_ref, gkw_ref,
                    ones_ref, bd_ref, lng_ref, lnb_ref, wm_ref, bias_ref,
                    qn_ref, qr_ref, kvc_ref, kvs_ref, ks_ref, vs_ref, kvw_ref, kw_ref, vw_ref, gate_ref,
                    ob_ref, vln_ref, *, tm):
    h = _mod_norm(x_ref[0], g_ref[...], sc_ref[0], sh_ref[0])
    y = jnp.dot(h.astype(BF16), w_ref[...], preferred_element_type=F32)
    tabs = (cos_ref[...], msin_ref[...], psin_ref[...])
    for hd in range(N_QHEADS):
        lanes = slice(hd * LANES, (hd + 1) * LANES)
        qn = _head_rms(y[:, lanes], ones_ref[...], gq_ref[:, lanes])
        qn_ref[0, hd] = qn
        qr_ref[0, hd] = (_rope_lanes(qn, *tabs) * (HEAD_DIM ** -0.5)).astype(BF16)
    off = QPAD_W
    kvc_ref[0] = y[:, off:off + NSA_KV_W]
    off += NSA_KV_W
    for kv_ref, k_ref, v_ref, gain_ref in ((kvs_ref, ks_ref, vs_ref, gks_ref), (kvw_ref, kw_ref, vw_ref, gkw_ref)):
        k = _rope_lanes(_head_rms(y[:, off:off + LANES], bd_ref[...], gain_ref[...]), *tabs)
        v = y[:, off + LANES:off + NSA_KV_W]
        kv_ref[0, :, :LANES] = k
        kv_ref[0, :, LANES:] = v
        k_ref[0] = k.astype(BF16)
        v_ref[0] = v.astype(BF16)
        off += NSA_KV_W
    u = jax.nn.gelu(y[:, off:off + GM_WIDTH])
    v = jax.nn.gelu(y[:, off + GM_WIDTH:off + 2 * GM_WIDTH])
    off += 2 * GM_WIDTH
    gate_ref[0] = y[:, off:]
    mu = jnp.mean(v, axis=-1, keepdims=True)
    var = jnp.mean(jnp.square(v - mu), axis=-1, keepdims=True)
    vln = (v - mu) * lax.rsqrt(var + NORM_EPS) * lng_ref[...] + lnb_ref[...]
    vln_ref[0] = vln
    lane = lax.broadcasted_iota(jnp.int32, (1, LANES), 1)
    for c in range(tm // GM_CHUNK):
        rows = slice(c * GM_CHUNK, (c + 1) * GM_CHUNK)
        mixed = []
        for p in range(GM_GROUPS // 2):
            vp = vln[rows, p * LANES:(p + 1) * LANES]
            lo = jnp.where(lane < HALF, vp, 0.0).astype(BF16)
            hi = jnp.where(lane < HALF, 0.0, vp).astype(BF16)
            mixed.append(jnp.dot(wm_ref[2 * p], lo, preferred_element_type=F32)
                         + jnp.dot(wm_ref[2 * p + 1], hi, preferred_element_type=F32))
        ob_ref[0, rows, :] = u[rows] * (jnp.concatenate(mixed, axis=-1) + bias_ref[...])


def _ab_proj(x, norm_g, scale, shift, w_in, pos, qk_gain, ln_g, ln_b, wm, mix_bias):
    B, T, D = x.shape
    tm = _row_tile(T)
    g0, g1 = AB_SPLITS[0], AB_SPLITS[0] + AB_SPLITS[1]
    w = jnp.concatenate([_pad_head_columns(w_in[:, :g0]), w_in[:, g1:], w_in[:, g0:g1]], axis=1).astype(BF16)
    N = w.shape[1]
    cos_t, msin_t, psin_t = _rope_tables(pos)
    ones_mat = jnp.full((LANES, LANES), 1.0 / HALF, BF16)
    bd = _block_diag2(jnp.full((HALF, HALF), 1.0 / HALF, F32)).astype(BF16)
    two = lambda g: jnp.concatenate([g, g])[None, :]
    const = lambda a: pl.BlockSpec(a.shape, lambda b, i: (0,) * a.ndim)
    tok = lambda n: pl.BlockSpec((1, tm, n), lambda b, i: (b, i, 0))
    heads = pl.BlockSpec((1, N_QHEADS, tm, LANES), lambda b, i: (b, 0, i, 0))
    table = pl.BlockSpec((tm, LANES), lambda b, i: (i, 0))
    consts = [_pad_head_vector(qk_gain[0]), two(qk_gain[2]), two(qk_gain[3]), ones_mat, bd,
              ln_g[None, :], ln_b[None, :], wm, mix_bias]
    sds = lambda shape, dt: jax.ShapeDtypeStruct(shape, dt)
    return pl.pallas_call(
        functools.partial(_ab_proj_kernel, tm=tm),
        out_shape=[sds((B, N_QHEADS, T, LANES), F32), sds((B, N_QHEADS, T, LANES), BF16),
                   sds((B, T, NSA_KV_W), F32),
                   sds((B, T, NSA_KV_W), F32), sds((B, T, LANES), BF16), sds((B, T, LANES), BF16),
                   sds((B, T, NSA_KV_W), F32), sds((B, T, LANES), BF16), sds((B, T, LANES), BF16),
                   sds((B, T, 3 * NSA_HEADS), F32), sds((B, T, GM_WIDTH), F32), sds((B, T, GM_WIDTH), F32)],
        grid=(B, T // tm),
        in_specs=[tok(D), pl.BlockSpec((1, D), lambda b, i: (0, 0)), _mod_spec(scale, tm), _mod_spec(shift, tm),
                  pl.BlockSpec((D, N), lambda b, i: (0, 0)), table, table, table] + [const(a) for a in consts],
        out_specs=[heads, heads, tok(NSA_KV_W), tok(NSA_KV_W), tok(LANES), tok(LANES), tok(NSA_KV_W), tok(LANES),
                   tok(LANES), tok(3 * NSA_HEADS), tok(GM_WIDTH), tok(GM_WIDTH)],
        compiler_params=pltpu.CompilerParams(
            dimension_semantics=("parallel", "parallel"), vmem_limit_bytes=VMEM_LIMIT_BYTES),
        name="ab_proj",
    )(x, norm_g.reshape(1, D), scale, shift, w, cos_t, msin_t, psin_t, *consts)


def _chunk_mix_weights(ws, bs, rows_per_seq):
    n = min(rows_per_seq, GM_CHUNK)
    wm = jnp.where(jnp.tril(jnp.ones((n, n), bool)), ws[:, :n, :n], 0.0)
    if n < GM_CHUNK:
        eye = jnp.eye(GM_CHUNK // n, dtype=F32)
        wm = jnp.einsum('ab,gij->gaibj', eye, wm).reshape(GM_GROUPS, GM_CHUNK, GM_CHUNK)
    bias = jnp.tile(jnp.transpose(bs[:, :n]), (GM_CHUNK // n, 1))
    return wm.astype(BF16), jnp.repeat(bias, GM_GROUP_W, axis=1)


C_GROUP_W = QPAD_W + 2 * DIL_KV_W


def _c_proj_kernel(x_ref, g_ref, sc_ref, sh_ref, w_ref, cos_ref, msin_ref, psin_ref, gq_ref, gk_ref,
                   ones_ref, bd_ref, *refs, tm, dils):
    out_refs, stage = refs[:-1], refs[-1]
    h = _mod_norm(x_ref[0], g_ref[...], sc_ref[0], sh_ref[0])
    y = jnp.dot(h.astype(BF16), w_ref[...], preferred_element_type=F32)
    tabs = (cos_ref[...], msin_ref[...], psin_ref[...])

    def put(dst, val, d):
        if d == 1:
            dst[0] = val.astype(dst.dtype)
            return
        stage[...] = val
        for r in range(d):
            dst[r] = stage[pl.ds(r, tm // d, stride=d), :].astype(dst.dtype)

    for g, d in enumerate(dils):
        qd_ref, kd_ref, vd_ref, kv_ref = out_refs[4 * g:4 * g + 4]
        off = g * C_GROUP_W
        for hd in range(N_QHEADS):
            lanes = slice(off + hd * LANES, off + (hd + 1) * LANES)
            q = _rope_lanes(_head_rms(y[:, lanes], ones_ref[...], gq_ref[:, hd * LANES:(hd + 1) * LANES]), *tabs)
            put(qd_ref.at[:, hd], q * (HEAD_DIM ** -0.5), d)
        k = _rope_lanes(_head_rms(y[:, off + QPAD_W:off + QPAD_W + LANES], bd_ref[...], gk_ref[...]), *tabs)
        v = y[:, off + QPAD_W + LANES:off + C_GROUP_W]
        kv_ref[0, :, :LANES] = k
        kv_ref[0, :, LANES:] = v
        put(kd_ref, k, d)
        put(vd_ref, v, d)


def _c_proj(x, norm_g, scale, shift, w_in, pos, qk_gain, dils):
    B, T, D = x.shape
    tm = _row_tile(T)
    pieces = []
    for g in range(len(dils)):
        c0 = g * (DIL_Q_W + 2 * DIL_KV_W)
        pieces += [_pad_head_columns(w_in[:, c0:c0 + DIL_Q_W]), w_in[:, c0 + DIL_Q_W:c0 + DIL_Q_W + 2 * DIL_KV_W]]
    w = jnp.concatenate(pieces, axis=1).astype(BF16)
    N = w.shape[1]
    cos_t, msin_t, psin_t = _rope_tables(pos)
    ones_mat = jnp.full((LANES, LANES), 1.0 / HALF, BF16)
    bd = _block_diag2(jnp.full((HALF, HALF), 1.0 / HALF, F32)).astype(BF16)
    consts = [_pad_head_vector(qk_gain[0]), jnp.concatenate([qk_gain[1], qk_gain[1]])[None, :], ones_mat, bd]
    const = lambda a: pl.BlockSpec(a.shape, lambda b, i: (0,) * a.ndim)
    tok = lambda n: pl.BlockSpec((1, tm, n), lambda b, i: (b, i, 0))
    table = pl.BlockSpec((tm, LANES), lambda b, i: (i, 0))
    sds = lambda shape, dt: jax.ShapeDtypeStruct(shape, dt)
    out_shape, out_specs = [], []
    for d in dils:
        out_shape += [sds((B * d, N_QHEADS, T // d, LANES), BF16), sds((B * d, T // d, LANES), BF16),
                      sds((B * d, T // d, LANES), BF16), sds((B, T, 2 * DIL_KV_W), F32)]
        seq = pl.BlockSpec((d, tm // d, LANES), lambda b, i: (b, i, 0))
        out_specs += [pl.BlockSpec((d, N_QHEADS, tm // d, LANES), lambda b, i: (b, 0, i, 0)), seq, seq,
                      tok(2 * DIL_KV_W)]
    return pl.pallas_call(
        functools.partial(_c_proj_kernel, tm=tm, dils=tuple(dils)),
        out_shape=out_shape,
        grid=(B, T // tm),
        in_specs=[tok(D), pl.BlockSpec((1, D), lambda b, i: (0, 0)), _mod_spec(scale, tm), _mod_spec(shift, tm),
                  pl.BlockSpec((D, N), lambda b, i: (0, 0)), table, table, table] + [const(a) for a in consts],
        out_specs=out_specs,
        scratch_shapes=[pltpu.VMEM((tm, LANES), F32)],
        compiler_params=pltpu.CompilerParams(
            dimension_semantics=("parallel", "parallel"), vmem_limit_bytes=VMEM_LIMIT_BYTES),
        name="c_proj",
    )(x, norm_g.reshape(1, D), scale, shift, w, cos_t, msin_t, psin_t, *consts)


def _dil_merge_proj_kernel(*refs, tm, dils):
    o_refs = refs[:len(dils)]
    w_ref, x_ref, gate_ref, out_ref, stage = refs[len(dils):]
    lane = lax.broadcasted_iota(jnp.int32, (1, LANES), 1)
    merged = []
    for hd in range(N_QHEADS):
        lower = hd < N_QHEADS // 2
        vals = []
        for o_ref, d in zip(o_refs, dils):
            if d == 1:
                vals.append(o_ref[0, hd])
            else:
                for r in range(d):
                    stage[pl.ds(r, tm // d, stride=d), :] = o_ref[r, hd]
                vals.append(stage[...])
        lses = [v[:, HALF:HALF + 1] if lower else v[:, 0:1] for v in vals]
        top = jnp.maximum(jnp.maximum(lses[0], lses[1]), lses[2])
        ws = [jnp.exp(l - top) for l in lses]
        total = ws[0] + ws[1] + ws[2]
        merged.append((ws[0] * vals[0] + ws[1] * vals[1] + ws[2] * vals[2]) / total)
    pairs = []
    for c in range(N_QHEADS // 2):
        a, b = merged[2 * c], merged[2 * c + 1]
        if 2 * c < N_QHEADS // 2:
            pairs.append(jnp.where(lane < HALF, a, pltpu.roll(b, HALF, 1)))
        else:
            pairs.append(jnp.where(lane < HALF, pltpu.roll(a, HALF, 1), b))
    o = jnp.concatenate(pairs, axis=-1)
    y = jnp.dot(o.astype(BF16), w_ref[...], preferred_element_type=F32)
    out_ref[0] = x_ref[0] + gate_ref[0] * y


def _dil_merge_proj(outs, dils, w_out, x, gate):
    B, T, D = x.shape
    tm = _row_tile(T)
    in_specs = [pl.BlockSpec((d, N_QHEADS, tm // d, LANES), lambda b, i: (b, 0, i, 0)) for d in dils]
    in_specs += [pl.BlockSpec(w_out.shape, lambda b, i: (0, 0)),
                 pl.BlockSpec((1, tm, D), lambda b, i: (b, i, 0)), _mod_spec(gate, tm)]
    return pl.pallas_call(
        functools.partial(_dil_merge_proj_kernel, tm=tm, dils=tuple(dils)),
        out_shape=jax.ShapeDtypeStruct((B, T, D), F32),
        grid=(B, T // tm),
        in_specs=in_specs,
        out_specs=pl.BlockSpec((1, tm, D), lambda b, i: (b, i, 0)),
        scratch_shapes=[pltpu.VMEM((tm, LANES), F32)],
        compiler_params=pltpu.CompilerParams(
            dimension_semantics=("parallel", "parallel"), vmem_limit_bytes=VMEM_LIMIT_BYTES),
        name="dil_merge_proj",
    )(*outs, w_out, x, gate)


def _kv5(x):
    return x.reshape(x.shape[:2] + (2, NSA_KV_HEADS, HEAD_DIM))


def _mixer_ab(x, norm_g, sc, sh, gate_m, w_in, w_out, start, pasts, weights, page_table, seq_shape):
    qk_gain, pe, w_phi, ln_g, ln_b, ws, bs = weights
    B, T = seq_shape
    pos = start + jnp.arange(x.shape[1], dtype=jnp.int32) % T
    wm, mix_bias = _chunk_mix_weights(ws, bs, T)
    qn, qr, kvc, kvs, ks, vs, kvw, kw, vw, gate, o_b, vln = _ab_proj(
        x, norm_g, sc, sh, w_in, pos, qk_gain, ln_g, ln_b, wm, mix_bias)
    if pasts is None:
        kc, vc = _cmp_finish(_block_mean(kvc), pe, w_phi, qk_gain[1])
        ocmp, sel = _cmp_select(qn, kc, vc, start)
        o_a = _nsa_prompt(qr, ks, vs, kw, vw, sel, ocmp, gate)
        kvw_all = kvw
    else:
        per_seq = lambda a: a.reshape((B, T) + a.shape[2:])
        heads = lambda a: a.reshape(N_QHEADS, B, T, LANES).transpose(1, 0, 2, 3)
        kvc, kvs, kvw, gate, vln = (per_seq(a) for a in (kvc, kvs, kvw, gate, vln))
        qn, qr = heads(qn), heads(qr.astype(F32))
        cache_cmp, cache_slc, win_past = pasts
        kc, vc = _cmp_finish(_page_block_mean(cache_cmp, page_table), pe, w_phi, qk_gain[1])
        ocmp, sel = _cmp_select(qn, kc, vc, start)
        kvw_all = jnp.concatenate([win_past, kvw], axis=1)
        owin = _band_tail(qr, kvw_all, NSA_WINDOW, 1, False)
        new_tile = jnp.pad(kvs, ((0, 0), (0, PAGE_SIZE - T), (0, 0)))
        o_a = _nsa_paged(page_table, qr, cache_slc, new_tile, sel, ocmp, owin, gate, start)
    keep_w = min(NSA_WINDOW, kvw_all.shape[1])
    n_cur = (T - 1) % GM_CHUNK + 1
    x_new = _proj_residual([o_a.reshape(o_b.shape), o_b], (w_out[:NSA_Q_W], w_out[NSA_Q_W:]), x, gate_m)
    return x_new, (_kv5(kvc), _kv5(kvs), _kv5(kvw_all[:, kvw_all.shape[1] - keep_w:]), vln[:, T - n_cur:])


def _mixer_c(x, norm_g, sc, sh, gate_m, w_in, w_out, start, pasts, qk_gain, seq_shape):
    B, T = seq_shape
    pos = start + jnp.arange(x.shape[1], dtype=jnp.int32) % T
    dils = tuple(d for _, d in DIL_CFG) if pasts is None else (1,) * len(DIL_CFG)
    res = _c_proj(x, norm_g, sc, sh, w_in, pos, qk_gain, dils)
    outs, new_bufs = [], []
    for g, (window, dilation) in enumerate(DIL_CFG):
        qd, kd, vd, kv_new = res[4 * g:4 * g + 4]
        if pasts is None:
            outs.append(_band_self(qd, kd, vd, window // dilation, True))
            kv_all = kv_new
        else:
            kv_all = jnp.concatenate([pasts[g], kv_new.reshape(B, T, 2 * DIL_KV_W)], axis=1)
            q = qd.reshape(N_QHEADS, B, T, LANES).transpose(1, 0, 2, 3).astype(F32)
            o = _band_tail(q, kv_all, window, dilation, True)
            outs.append(o.transpose(1, 0, 2, 3).reshape(1, N_QHEADS, B * T, LANES))
        keep = min(window, kv_all.shape[1])
        new_bufs.append(kv_all[:, kv_all.shape[1] - keep:].reshape(B, keep, 2, DIL_KV_HEADS, HEAD_DIM))
    return _dil_merge_proj(outs, dils, w_out, x, gate_m), tuple(new_bufs)


def _adaln_kernel(c_ref, w_ref, b_ref, o_ref):
    act = jax.nn.silu(c_ref[...]).astype(BF16)
    o_ref[...] = jnp.dot(act, w_ref[...].astype(BF16), preferred_element_type=F32) + b_ref[...]


def _adaln(c, w, b):
    B, D = c.shape
    N = w.shape[1]
    tn = D
    mod = pl.pallas_call(
        _adaln_kernel,
        out_shape=jax.ShapeDtypeStruct((B, N), F32),
        grid=(N // tn,),
        in_specs=[pl.BlockSpec((B, D), lambda j: (0, 0)), pl.BlockSpec((D, tn), lambda j: (0, j)),
                  pl.BlockSpec((1, tn), lambda j: (0, j))],
        out_specs=pl.BlockSpec((B, tn), lambda j: (0, j)),
        compiler_params=pltpu.CompilerParams(dimension_semantics=("parallel",)),
        name="adaln",
    )(c, w, b.reshape(1, N))
    return [mod[:, None, j * D:(j + 1) * D] for j in range(N // D)]


def _expand_mod(m, t):
    B, _, D = m.shape
    return jnp.broadcast_to(m, (B, t, D)).reshape(1, B * t, D)


def _layer(groups, mixers, norm_m, norm_f, moe_w):
    w_router, b_router, w_up, b_up, w_down, b_down = moe_w
    xs, states, routed = [], [], []
    counts = jnp.zeros((1, N_EXPERTS), F32)
    for (x, mods), mixer in zip(groups, mixers):
        sh_m, sc_m, g_m, sh_f, sc_f, g_f = mods
        B, T, D = x.shape
        x, state = mixer(x, norm_m, sc_m, sh_m, g_m)
        h, route_i, route_g, counts = _mod_norm_router(x, norm_f, sc_f, sh_f, w_router, b_router, counts)
        xs.append(x)
        states.append(state)
        routed.append((h, route_i, route_g))
    n_asg = sum(x.shape[0] * x.shape[1] for x in xs) * TOP_K
    n_tiles = -(-(n_asg + N_EXPERTS * (MOE_TILE - 1)) // MOE_TILE)
    pad_start, pad_end, n_used = _moe_plan(counts[0])
    x_sorted = jnp.zeros((n_tiles * MOE_TILE, xs[0].shape[2]), F32)
    dests = []
    for h, route_i, _ in routed:
        expert = route_i[..., :TOP_K]
        rank = route_i[..., ROUTE_RANK_LANE:ROUTE_RANK_LANE + TOP_K]
        first = jnp.sum(jnp.where(expert[..., None] == jnp.arange(N_EXPERTS), pad_start, 0), axis=-1)
        dests.append((first + rank).astype(jnp.int32))
        x_sorted = _moe_dispatch(dests[-1], h, x_sorted)
    y_sorted = _moe_ffn(*_moe_tiles(pad_end, n_used, n_tiles), x_sorted, w_up, b_up, w_down, b_down)
    outs = [_moe_combine(dest, y_sorted, route_g, x, mods[5])
            for dest, (_, _, route_g), x, (_, mods) in zip(dests, routed, xs, groups)]
    return outs, states


def _stack(states, j):
    return jnp.stack([s[j] for s in states], axis=0)


def kernel(x_prompt, x_sample, cache_cmp_kv, cache_slc_kv, state_win_kv, state_dil0_kv, state_dil1_kv,
           state_dil2_kv, page_table, c_prompt, c_sample, norm_mix, norm_ffn, w_ada, b_ada, w_in_ab, w_out_ab,
           nsa_qk_gain, nsa_pe, nsa_w_phi, gm_ln_g, gm_ln_b, gm_ws, gm_bs, w_in_c, w_out_c, dil_qk_gain,
           w_router, b_router, w_up, b_up, w_down, b_down):
    depth = norm_mix.shape[0]
    past_len = page_table.shape[1] * PAGE_SIZE
    Bs, Ts, D = x_sample.shape
    x_p = x_prompt
    x_s = x_sample.reshape(1, Bs * Ts, D)
    ab_p, ab_s, dil_p, dil_s = [], [], [], []
    for layer in range(depth):
        i = layer // 2
        moe_w = (w_router[layer], b_router[layer], w_up[layer], b_up[layer], w_down[layer], b_down[layer])
        mods_p = _adaln(c_prompt, w_ada[layer], b_ada[layer])
        mods_s = [_expand_mod(m, Ts) for m in _adaln(c_sample, w_ada[layer], b_ada[layer])]
        groups = [(x_p, mods_p), (x_s, mods_s)]
        if layer % 2 == 0:
            wts = (nsa_qk_gain[i], nsa_pe[i], nsa_w_phi[i], gm_ln_g[i], gm_ln_b[i], gm_ws[i], gm_bs[i])
            flat_kv = lambda a: a.reshape(a.shape[:2] + (NSA_KV_W,))
            pasts_s = (flat_kv(cache_cmp_kv[i]), flat_kv(cache_slc_kv[i]), flat_kv(state_win_kv[i]))

            w_io = (w_in_ab[i], w_out_ab[i].astype(BF16))

            def mix_p(x, g, sc, sh, gm, wts=wts, w_io=w_io):
                return _mixer_ab(x, g, sc, sh, gm, *w_io, 0, None, wts, None, x.shape[:2])

            def mix_s(x, g, sc, sh, gm, wts=wts, w_io=w_io, pasts_s=pasts_s):
                return _mixer_ab(x, g, sc, sh, gm, *w_io, past_len, pasts_s, wts, page_table, (Bs, Ts))

            (x_p, x_s), (st_p, st_s) = _layer(groups, (mix_p, mix_s), norm_mix[layer], norm_ffn[layer], moe_w)
            ab_p.append(st_p)
            ab_s.append(st_s)
        else:
            flat_kv = lambda a: a.reshape(a.shape[:2] + (2 * DIL_KV_W,))
            pasts_s = (flat_kv(state_dil0_kv[i]), flat_kv(state_dil1_kv[i]), flat_kv(state_dil2_kv[i]))

            w_io = (w_in_c[i], w_out_c[i].astype(BF16))

            def mix_p(x, g, sc, sh, gm, gain=dil_qk_gain[i], w_io=w_io):
                return _mixer_c(x, g, sc, sh, gm, *w_io, 0, None, gain, x.shape[:2])

            def mix_s(x, g, sc, sh, gm, pasts_s=pasts_s, gain=dil_qk_gain[i], w_io=w_io):
                return _mixer_c(x, g, sc, sh, gm, *w_io, past_len, pasts_s, gain, (Bs, Ts))

            (x_p, x_s), (st_p, st_s) = _layer(groups, (mix_p, mix_s), norm_mix[layer], norm_ffn[layer], moe_w)
            dil_p.append(st_p)
            dil_s.append(st_s)
    x_s = x_s.reshape(Bs, Ts, D)
    cmp_p, cmp_s = _stack(ab_p, 0), _stack(ab_s, 0)
    slc_p, slc_s = _stack(ab_p, 1), _stack(ab_s, 1)
    win_p, win_s = _stack(ab_p, 2), _stack(ab_s, 2)
    gmv_p, gmv_s = _stack(ab_p, 3), _stack(ab_s, 3)
    d0_p, d0_s = _stack(dil_p, 0), _stack(dil_s, 0)
    d1_p, d1_s = _stack(dil_p, 1), _stack(dil_s, 1)
    d2_p, d2_s = _stack(dil_p, 2), _stack(dil_s, 2)
    return (x_p, x_s, cmp_p, cmp_s, slc_p, slc_s, win_p, win_s, gmv_p, gmv_s, d0_p, d0_s, d1_p, d1_s, d2_p, d2_s)
```

```python
import functools
import math

import jax
import jax.numpy as jnp
from jax import lax
from jax.experimental import pallas as pl
from jax.experimental.pallas import tpu as pltpu

F32 = jnp.float32
BF16 = jnp.bfloat16

D_MODEL = 1024
HEAD_DIM = 64
ROT_DIM = HEAD_DIM // 4
ROPE_THETA = 500000.0
NORM_EPS = 1e-6
NEG_INF = -1e30
Q_BLOCK = 128
PAGE_SIZE = 128

NSA_HEADS = D_MODEL // (2 * HEAD_DIM)
NSA_KV_HEADS = 2
NSA_GROUP = NSA_HEADS // NSA_KV_HEADS
NSA_BLOCK = 64
NSA_N_SEL = 16
NSA_WINDOW = 512
NSA_Q_W = NSA_HEADS * HEAD_DIM
NSA_KV_W = 2 * NSA_KV_HEADS * HEAD_DIM

GM_GROUPS = 8
GM_WIDTH = D_MODEL // 2
GM_GROUP_W = GM_WIDTH // GM_GROUPS
GM_CHUNK = 128

AB_SPLITS = (NSA_Q_W, 3 * NSA_HEADS, NSA_KV_W, NSA_KV_W, NSA_KV_W, GM_WIDTH, GM_WIDTH)

DIL_CFG = ((128, 1), (512, 4), (2048, 16))
DIL_HEADS = 8
DIL_KV_HEADS = 2
DIL_Q_W = DIL_HEADS * HEAD_DIM
DIL_KV_W = DIL_KV_HEADS * HEAD_DIM

N_EXPERTS = 32
TOP_K = 4
D_FF = D_MODEL
SWIGLU_ALPHA = 1.702
SWIGLU_LIMIT = 7.0

VMEM_LIMIT_BYTES = 56 * 1024 * 1024
ROW_TILE = 512
MOE_TILE = 512


def _row_tile(t):
    return t if t <= ROW_TILE else ROW_TILE


def _mod_spec(mod, tm):
    if mod.shape[1] == 1:
        return pl.BlockSpec((1, 1, mod.shape[2]), lambda b, i: (b, 0, 0))
    return pl.BlockSpec((1, tm, mod.shape[2]), lambda b, i: (b, i, 0))


def _split_bf16(v):
    hi = v.astype(BF16)
    lo = (v - hi.astype(F32)).astype(BF16)
    return hi, lo


def _mod_norm(x, g, sc, sh):
    h = x * lax.rsqrt(jnp.mean(x * x, axis=-1, keepdims=True) + NORM_EPS) * g
    return h * (1.0 + sc) + sh


def _proj_residual_kernel(*refs, n_parts):
    a_refs = refs[:n_parts]
    w_refs = refs[n_parts:2 * n_parts]
    x_ref, g_ref, o_ref = refs[2 * n_parts:]
    y = None
    for a_ref, w_ref in zip(a_refs, w_refs):
        p = jnp.dot(a_ref[0].astype(BF16), w_ref[...], preferred_element_type=F32)
        y = p if y is None else y + p
    o_ref[0] = x_ref[0] + g_ref[0] * y


def _proj_residual(a_parts, w_parts_bf16, x, gate):
    B, T, D = x.shape
    tm = _row_tile(T)
    n_parts = len(a_parts)
    in_specs = [pl.BlockSpec((1, tm, a.shape[2]), lambda b, i: (b, i, 0)) for a in a_parts]
    in_specs += [pl.BlockSpec(w.shape, lambda b, i: (0, 0)) for w in w_parts_bf16]
    in_specs += [pl.BlockSpec((1, tm, D), lambda b, i: (b, i, 0)), _mod_spec(gate, tm)]
    return pl.pallas_call(
        functools.partial(_proj_residual_kernel, n_parts=n_parts),
        out_shape=jax.ShapeDtypeStruct((B, T, D), F32),
        grid=(B, T // tm),
        in_specs=in_specs,
        out_specs=pl.BlockSpec((1, tm, D), lambda b, i: (b, i, 0)),
        compiler_params=pltpu.CompilerParams(
            dimension_semantics=("parallel", "parallel"), vmem_limit_bytes=VMEM_LIMIT_BYTES),
        name="proj_residual",
    )(*a_parts, *w_parts_bf16, x, gate)


ROUTE_RANK_LANE = TOP_K


def _mod_norm_router_kernel(x_ref, g_ref, sc_ref, sh_ref, whi_ref, wlo_ref, b_ref, cnt_in_ref,
                            h_ref, ri_ref, rg_ref, cnt_ref, *, tm):
    @pl.when((pl.program_id(0) == 0) & (pl.program_id(1) == 0))
    def _():
        cnt_ref[...] = cnt_in_ref[...]

    h = _mod_norm(x_ref[0], g_ref[...], sc_ref[0], sh_ref[0])
    h_ref[0] = h
    h_hi, h_lo = _split_bf16(h)
    w_hi = whi_ref[...]
    logits = (jnp.dot(h_hi, w_hi, preferred_element_type=F32)
              + jnp.dot(h_lo, w_hi, preferred_element_type=F32)
              + jnp.dot(h_hi, wlo_ref[...], preferred_element_type=F32)) + b_ref[...]
    n_exp = logits.shape[-1]
    lane = lax.broadcasted_iota(jnp.int32, (1, n_exp), 1)
    work = logits
    vals, ids, hits = [], [], []
    for _ in range(TOP_K):
        m = jnp.max(work, axis=-1, keepdims=True)
        idx = jnp.min(jnp.where(work == m, lane, n_exp), axis=-1, keepdims=True)
        hit = lane == idx
        vals.append(m)
        ids.append(idx)
        hits.append(hit)
        work = jnp.where(hit, NEG_INF, work)
    exps = [jnp.exp(v - vals[0]) for v in vals]
    denom = exps[0] + exps[1] + exps[2] + exps[3]
    onehot = jnp.zeros(logits.shape, F32)
    for hit in hits:
        onehot = onehot + jnp.where(hit, 1.0, 0.0)
    row = lax.broadcasted_iota(jnp.int32, (tm, tm), 0)
    col = lax.broadcasted_iota(jnp.int32, (tm, tm), 1)
    earlier = jnp.where(row > col, 1.0, 0.0).astype(BF16)
    before = jnp.dot(earlier, onehot.astype(BF16), preferred_element_type=F32) + cnt_ref[...]
    cnt_ref[...] = cnt_ref[...] + jnp.sum(onehot, axis=0, keepdims=True)
    out_lane = lax.broadcasted_iota(jnp.int32, (1, LANES), 1)
    ri = jnp.zeros((tm, LANES), jnp.int32)
    rg = jnp.zeros((tm, LANES), F32)
    for k in range(TOP_K):
        rank = jnp.sum(jnp.where(hits[k], before, 0.0), axis=-1, keepdims=True).astype(jnp.int32)
        ri = jnp.where(out_lane == k, ids[k], ri)
        ri = jnp.where(out_lane == ROUTE_RANK_LANE + k, rank, ri)
        rg = jnp.where(out_lane == k, exps[k] / denom, rg)
    ri_ref[0] = ri
    rg_ref[0] = rg


def _mod_norm_router(x, norm_g, scale, shift, w_router, b_router, counts_in):
    B, T, D = x.shape
    tm = _row_tile(T)
    E = w_router.shape[1]
    w_hi, w_lo = _split_bf16(w_router)
    tok = lambda n: pl.BlockSpec((1, tm, n), lambda b, i: (b, i, 0))
    return pl.pallas_call(
        functools.partial(_mod_norm_router_kernel, tm=tm),
        out_shape=[jax.ShapeDtypeStruct((B, T, D), F32), jax.ShapeDtypeStruct((B, T, LANES), jnp.int32),
                   jax.ShapeDtypeStruct((B, T, LANES), F32), jax.ShapeDtypeStruct((1, E), F32)],
        grid=(B, T // tm),
        in_specs=[
            tok(D),
            pl.BlockSpec((1, D), lambda b, i: (0, 0)),
            _mod_spec(scale, tm),
            _mod_spec(shift, tm),
            pl.BlockSpec((D, E), lambda b, i: (0, 0)),
            pl.BlockSpec((D, E), lambda b, i: (0, 0)),
            pl.BlockSpec((1, E), lambda b, i: (0, 0)),
            pl.BlockSpec((1, E), lambda b, i: (0, 0)),
        ],
        out_specs=[tok(D), tok(LANES), tok(LANES), pl.BlockSpec((1, E), lambda b, i: (0, 0))],
        compiler_params=pltpu.CompilerParams(
            dimension_semantics=("arbitrary", "arbitrary"), vmem_limit_bytes=VMEM_LIMIT_BYTES),
        name="mod_norm_router",
    )(x, norm_g.reshape(1, D), scale, shift, w_hi, w_lo, b_router.reshape(1, E), counts_in)


def _moe_ffn_kernel(te_ref, tx_ref, tv_ref, x_ref, wu_ref, bu_ref, wd_ref, bd_ref, o_ref):
    i = pl.program_id(0)

    @pl.when(tv_ref[i] != 0)
    def _():
        hu = jnp.dot(x_ref[...].astype(BF16), wu_ref[0, 0].astype(BF16), preferred_element_type=F32) + bu_ref[0, 0]
        glu = jnp.minimum(hu[:, :D_FF], SWIGLU_LIMIT)
        lin = jnp.clip(hu[:, D_FF:], -SWIGLU_LIMIT, SWIGLU_LIMIT)
        act = glu * jax.nn.sigmoid(SWIGLU_ALPHA * glu) * (lin + 1.0)
        o_ref[...] = (jnp.dot(act.astype(BF16), wd_ref[0, 0].astype(BF16), preferred_element_type=F32)
                      + bd_ref[0, 0])

    @pl.when(tv_ref[i] == 0)
    def _():
        o_ref[...] = jnp.zeros_like(o_ref)


def _moe_ffn(tile_exp, tile_x, tile_valid, x_sorted, layer, w_up, b_up, w_down, b_down):
    n_slots, D = x_sorted.shape
    n_tiles = n_slots // MOE_TILE
    depth, E, _, F2 = w_up.shape
    expert = lambda i, te, tx, tv: (layer, te[i], 0, 0)
    grid_spec = pltpu.PrefetchScalarGridSpec(
        num_scalar_prefetch=3,
        grid=(n_tiles,),
        in_specs=[
            pl.BlockSpec((MOE_TILE, D), lambda i, te, tx, tv: (tx[i], 0)),
            pl.BlockSpec((1, 1, D, F2), expert),
            pl.BlockSpec((1, 1, 1, F2), expert),
            pl.BlockSpec((1, 1, F2 // 2, D), expert),
            pl.BlockSpec((1, 1, 1, D), expert),
        ],
        out_specs=pl.BlockSpec((MOE_TILE, D), lambda i, te, tx, tv: (i, 0)),
    )
    return pl.pallas_call(
        _moe_ffn_kernel,
        out_shape=jax.ShapeDtypeStruct((n_slots, D), F32),
        grid_spec=grid_spec,
        compiler_params=pltpu.CompilerParams(
            dimension_semantics=("arbitrary",), vmem_limit_bytes=VMEM_LIMIT_BYTES),
        name="moe_ffn",
    )(tile_exp, tile_x, tile_valid, x_sorted, w_up, b_up.reshape(depth, E, 1, F2), w_down,
      b_down.reshape(depth, E, 1, D))


DISPATCH_TILE = 256
COMBINE_TILE = 128


def _row_copy(src_ref, src_row, dst_ref, dst_row, sem):
    return pltpu.make_async_copy(src_ref.at[pl.ds(src_row, 1)], dst_ref.at[pl.ds(dst_row, 1)], sem)


def _moe_dispatch_kernel(dest_ref, h_ref, xs_in_ref, xs_ref, sem, *, tm):
    del xs_in_ref
    h_rows = h_ref.at[0]

    def issue(t, carry):
        for k in range(TOP_K):
            _row_copy(h_rows, t, xs_ref, dest_ref[0, 0, 0, t * TOP_K + k], sem).start()
        return carry

    lax.fori_loop(0, tm, issue, 0, unroll=4)
    for k in range(TOP_K):
        pltpu.make_async_copy(h_rows, xs_ref.at[pl.ds(0, tm)], sem).wait()


def _dest_spec(tm):
    return pl.BlockSpec((1, 1, 1, tm * TOP_K), lambda b, i: (b, i, 0, 0), memory_space=pltpu.SMEM)


def _moe_dispatch(dest, h, x_sorted):
    B, T, D = h.shape
    tm = min(T, DISPATCH_TILE)
    return pl.pallas_call(
        functools.partial(_moe_dispatch_kernel, tm=tm),
        out_shape=jax.ShapeDtypeStruct(x_sorted.shape, x_sorted.dtype),
        grid=(B, T // tm),
        in_specs=[_dest_spec(tm),
                  pl.BlockSpec((1, tm, D), lambda b, i: (b, i, 0)),
                  pl.BlockSpec(memory_space=pl.ANY)],
        out_specs=pl.BlockSpec(memory_space=pl.ANY),
        scratch_shapes=[pltpu.SemaphoreType.DMA(())],
        input_output_aliases={2: 0},
        compiler_params=pltpu.CompilerParams(dimension_semantics=("arbitrary", "arbitrary")),
        name="moe_dispatch",
    )(dest.reshape(B, T // tm, 1, tm * TOP_K), h, x_sorted)


def _moe_combine_kernel(dest_ref, ys_ref, rg_ref, x_ref, gf_ref, o_ref, buf, sem, *, tm):
    def issue(t, carry):
        for k in range(TOP_K):
            _row_copy(ys_ref, dest_ref[0, 0, 0, t * TOP_K + k], buf.at[k], t, sem).start()
        return carry

    lax.fori_loop(0, tm, issue, 0, unroll=4)
    for k in range(TOP_K):
        pltpu.make_async_copy(ys_ref.at[pl.ds(0, tm)], buf.at[k], sem).wait()
    gates = rg_ref[0]
    y = gates[:, 0:1] * buf[0]
    for k in range(1, TOP_K):
        y = y + gates[:, k:k + 1] * buf[k]
    o_ref[0] = x_ref[0] + gf_ref[0] * y


def _moe_combine(dest, y_sorted, route_g, x, gate_f):
    B, T, D = x.shape
    tm = min(T, COMBINE_TILE)
    tok = lambda n: pl.BlockSpec((1, tm, n), lambda b, i: (b, i, 0))
    return pl.pallas_call(
        functools.partial(_moe_combine_kernel, tm=tm),
        out_shape=jax.ShapeDtypeStruct((B, T, D), F32),
        grid=(B, T // tm),
        in_specs=[_dest_spec(tm), pl.BlockSpec(memory_space=pl.ANY), tok(LANES), tok(D), _mod_spec(gate_f, tm)],
        out_specs=tok(D),
        scratch_shapes=[pltpu.VMEM((TOP_K, tm, D), F32), pltpu.SemaphoreType.DMA(())],
        compiler_params=pltpu.CompilerParams(
            dimension_semantics=("parallel", "parallel"), vmem_limit_bytes=VMEM_LIMIT_BYTES),
        name="moe_combine",
    )(dest.reshape(B, T // tm, 1, tm * TOP_K), y_sorted, route_g, x, gate_f)


def _moe_plan(counts):
    counts = counts.astype(jnp.int32)
    padded = (counts + MOE_TILE - 1) // MOE_TILE * MOE_TILE
    pad_end = jnp.cumsum(padded)
    pad_start = pad_end - padded
    n_used = pad_end[-1] // MOE_TILE
    return pad_start, pad_end, n_used


def _moe_tiles(pad_end, n_used, n_tiles):
    tile = jnp.arange(n_tiles, dtype=jnp.int32)
    tile_valid = (tile < n_used).astype(jnp.int32)
    tile_x = jnp.minimum(tile, n_used - 1)
    tile_exp = jnp.sum((tile_x[:, None] * MOE_TILE >= pad_end[None, :]).astype(jnp.int32), axis=1)
    return jnp.minimum(tile_exp, N_EXPERTS - 1), tile_x, tile_valid


N_QHEADS = 8
LANES = 128
HALF = HEAD_DIM


def _nt_dot(a, b):
    return lax.dot_general(a, b, (((1,), (1,)), ((), ())), preferred_element_type=F32)


def _band_mask(qpos, kpos, window, dilation):
    delta = qpos - kpos
    valid = (delta >= 0) & (delta <= window)
    if dilation > 1:
        valid = valid & ((delta & (dilation - 1)) == 0)
    return valid


def _softmax_pv(s, valid, v, tq):
    n = s.shape[-1]
    s = jnp.where(valid[None], s.reshape(N_QHEADS, tq, n), NEG_INF)
    m = jnp.max(s, axis=-1, keepdims=True)
    p = jnp.exp(s - m)
    l = jnp.sum(p, axis=-1, keepdims=True)
    o = jnp.dot(p.reshape(N_QHEADS * tq, n).astype(BF16), v, preferred_element_type=F32)
    return o.reshape(N_QHEADS, tq, LANES) / l, m + jnp.log(l)


def _store_heads(o_ref, o, lse):
    if lse is None:
        o_ref[0] = o
        return
    lane = lax.broadcasted_iota(jnp.int32, (1, LANES), 1)
    for h in range(N_QHEADS):
        own = (lane < HALF) if h < N_QHEADS // 2 else (lane >= HALF)
        o_ref[0, h] = jnp.where(own, o[h], lse[h])


def _band_self_kernel(q_ref, k_ref, v_ref, o_ref, *, window, tq, span, want_lse):
    T = k_ref.shape[1]
    q0 = pl.program_id(1) * tq
    if span == T:
        start = 0
        k = k_ref[0]
        v = v_ref[0]
    else:
        start = pl.multiple_of(jnp.maximum(q0 - window, 0), LANES)
        k = k_ref[0, pl.ds(start, span), :]
        v = v_ref[0, pl.ds(start, span), :]
    q = q_ref[0].reshape(N_QHEADS * tq, LANES).astype(BF16)
    qpos = q0 + lax.broadcasted_iota(jnp.int32, (tq, 1), 0)
    kpos = start + lax.broadcasted_iota(jnp.int32, (1, span), 1)
    o, lse = _softmax_pv(_nt_dot(q, k), _band_mask(qpos, kpos, window, 1), v, tq)
    _store_heads(o_ref, o, lse if want_lse else None)


def _band_self(q, k, v, window, want_lse):
    B, _, T, _ = q.shape
    tq = min(T, Q_BLOCK)
    span = min(T, window + tq)
    return pl.pallas_call(
        functools.partial(_band_self_kernel, window=window, tq=tq, span=span, want_lse=want_lse),
        out_shape=jax.ShapeDtypeStruct((B, N_QHEADS, T, LANES), F32),
        grid=(B, T // tq),
        in_specs=[
            pl.BlockSpec((1, N_QHEADS, tq, LANES), lambda b, i: (b, 0, i, 0)),
            pl.BlockSpec((1, T, LANES), lambda b, i: (b, 0, 0)),
            pl.BlockSpec((1, T, LANES), lambda b, i: (b, 0, 0)),
        ],
        out_specs=pl.BlockSpec((1, N_QHEADS, tq, LANES), lambda b, i: (b, 0, i, 0)),
        compiler_params=pltpu.CompilerParams(
            dimension_semantics=("parallel", "parallel"), vmem_limit_bytes=VMEM_LIMIT_BYTES),
        name="band_self",
    )(q, k, v)


def _band_tail_kernel(q_ref, kv_ref, o_ref, *, window, dilation, tq, want_lse):
    L = kv_ref.shape[1]
    kv = kv_ref[0]
    k = kv[:, :LANES].astype(BF16)
    v = kv[:, LANES:].astype(BF16)
    q = q_ref[0].reshape(N_QHEADS * tq, LANES).astype(BF16)
    qpos = (L - tq) + lax.broadcasted_iota(jnp.int32, (tq, 1), 0)
    kpos = lax.broadcasted_iota(jnp.int32, (1, L), 1)
    o, lse = _softmax_pv(_nt_dot(q, k), _band_mask(qpos, kpos, window, dilation), v, tq)
    _store_heads(o_ref, o, lse if want_lse else None)


def _band_tail(q, kv_all, window, dilation, want_lse):
    B, _, T, _ = q.shape
    L = kv_all.shape[1]
    return pl.pallas_call(
        functools.partial(_band_tail_kernel, window=window, dilation=dilation, tq=T, want_lse=want_lse),
        out_shape=jax.ShapeDtypeStruct((B, N_QHEADS, T, LANES), F32),
        grid=(B,),
        in_specs=[
            pl.BlockSpec((1, N_QHEADS, T, LANES), lambda b: (b, 0, 0, 0)),
            pl.BlockSpec((1, L, 2 * LANES), lambda b: (b, 0, 0)),
        ],
        out_specs=pl.BlockSpec((1, N_QHEADS, T, LANES), lambda b: (b, 0, 0, 0)),
        compiler_params=pltpu.CompilerParams(
            dimension_semantics=("parallel",), vmem_limit_bytes=VMEM_LIMIT_BYTES),
        name="band_tail",
    )(q, kv_all)


def _block_mean_kernel(x_ref, o_ref):
    rows = x_ref.shape[1]
    x = x_ref[0].reshape(rows // NSA_BLOCK, NSA_BLOCK, x_ref.shape[2])
    o_ref[0] = jnp.sum(x, axis=1) * (1.0 / NSA_BLOCK)


def _block_mean(kvc):
    B, T, W = kvc.shape
    tm = _row_tile(T)
    return pl.pallas_call(
        _block_mean_kernel,
        out_shape=jax.ShapeDtypeStruct((B, T // NSA_BLOCK, W), F32),
        grid=(B, T // tm),
        in_specs=[pl.BlockSpec((1, tm, W), lambda b, i: (b, i, 0))],
        out_specs=pl.BlockSpec((1, tm // NSA_BLOCK, W), lambda b, i: (b, i, 0)),
        compiler_params=pltpu.CompilerParams(dimension_semantics=("parallel", "parallel")),
        name="block_mean",
    )(kvc)


PAGES_PER_STEP = 8


def _page_specs(n_cols):
    return [pl.BlockSpec((1, PAGE_SIZE, n_cols), functools.partial(
        lambda b, s, pt, pg: (pt[b, s * PAGES_PER_STEP + pg], 0, 0), pg=pg)) for pg in range(PAGES_PER_STEP)]


def _page_block_mean_kernel(pt_ref, *refs):
    page_refs, o_ref = refs[:-1], refs[-1]
    per_page = PAGE_SIZE // NSA_BLOCK
    for pg, page_ref in enumerate(page_refs):
        x = page_ref[0].reshape(per_page, NSA_BLOCK, page_ref.shape[2])
        o_ref[0, pg * per_page:(pg + 1) * per_page, :] = jnp.sum(x, axis=1) * (1.0 / NSA_BLOCK)


def _page_block_mean(cache, page_table):
    B, n_pages = page_table.shape
    W = cache.shape[2]
    per_step = PAGES_PER_STEP * PAGE_SIZE // NSA_BLOCK
    grid_spec = pltpu.PrefetchScalarGridSpec(
        num_scalar_prefetch=1,
        grid=(B, n_pages // PAGES_PER_STEP),
        in_specs=_page_specs(W),
        out_specs=pl.BlockSpec((1, per_step, W), lambda b, s, pt: (b, s, 0)),
    )
    return pl.pallas_call(
        _page_block_mean_kernel,
        out_shape=jax.ShapeDtypeStruct((B, n_pages * PAGE_SIZE // NSA_BLOCK, W), F32),
        grid_spec=grid_spec,
        compiler_params=pltpu.CompilerParams(dimension_semantics=("parallel", "arbitrary")),
        name="page_block_mean",
    )(page_table, *([cache] * PAGES_PER_STEP))


def _head_mean_sq(x, ones_bd):
    hi, lo = _split_bf16(x * x)
    return jnp.dot(hi, ones_bd, preferred_element_type=F32) + jnp.dot(lo, ones_bd, preferred_element_type=F32)


def _cmp_finish_kernel(mean_ref, pe_ref, wk_ref, wv_ref, gain_ref, bd_ref, kc_ref, vc_ref):
    s = mean_ref[0] + pe_ref[...]
    sk = jnp.dot(s[:, :LANES].astype(BF16), wk_ref[...], preferred_element_type=F32)
    vc_ref[0] = jnp.dot(s[:, LANES:].astype(BF16), wv_ref[...], preferred_element_type=F32)
    kc_ref[0] = sk * lax.rsqrt(_head_mean_sq(sk, bd_ref[...]) + NORM_EPS) * gain_ref[...]


def _block_diag2(w):
    z = jnp.zeros_like(w)
    return jnp.concatenate([jnp.concatenate([w, z], axis=1), jnp.concatenate([z, w], axis=1)], axis=0)


def _cmp_finish(mean, pe, w_phi, kc_gain):
    B, n_cb, W = mean.shape
    pe_mean = jnp.mean(pe, axis=0)
    pe_row = jnp.concatenate([pe_mean[0], pe_mean[0], pe_mean[1], pe_mean[1]])[None, :]
    wk = _block_diag2(w_phi[0]).astype(BF16)
    wv = _block_diag2(w_phi[1]).astype(BF16)
    bd = _block_diag2(jnp.full((HALF, HALF), 1.0 / HALF, F32)).astype(BF16)
    gain = jnp.concatenate([kc_gain, kc_gain])[None, :]
    full = lambda shape: pl.BlockSpec(shape, lambda b: (0,) * len(shape))
    return pl.pallas_call(
        _cmp_finish_kernel,
        out_shape=[jax.ShapeDtypeStruct((B, n_cb, LANES), F32)] * 2,
        grid=(B,),
        in_specs=[pl.BlockSpec((1, n_cb, W), lambda b: (b, 0, 0)), full((1, W)), full((LANES, LANES)),
                  full((LANES, LANES)), full((1, LANES)), full((LANES, LANES))],
        out_specs=[pl.BlockSpec((1, n_cb, LANES), lambda b: (b, 0, 0))] * 2,
        compiler_params=pltpu.CompilerParams(dimension_semantics=("parallel",)),
        name="cmp_finish",
    )(mean, pe_row, wk, wv, gain, bd)


def _cmp_select_kernel(q_ref, kc_ref, vc_ref, ocmp_ref, sel_ref, *, tq, pos0):
    n_cb = kc_ref.shape[1]
    q0 = pl.program_id(1) * tq
    qh, ql = _split_bf16(q_ref[0].reshape(N_QHEADS * tq, LANES))
    kh, kl = _split_bf16(kc_ref[0])
    s = (_nt_dot(qh, kh) + _nt_dot(ql, kh) + _nt_dot(qh, kl)) * (HEAD_DIM ** -0.5)
    cur = (pos0 + q0 + lax.broadcasted_iota(jnp.int32, (tq, 1), 0)) // NSA_BLOCK
    blk = lax.broadcasted_iota(jnp.int32, (1, n_cb), 1)
    vis = blk < cur
    s = jnp.where(vis[None], s.reshape(N_QHEADS, tq, n_cb), NEG_INF)
    p = jnp.exp(s - jnp.max(s, axis=-1, keepdims=True))
    p = jnp.where(vis[None], p / jnp.sum(p, axis=-1, keepdims=True), 0.0)
    o = jnp.dot(p.reshape(N_QHEADS * tq, n_cb).astype(BF16), vc_ref[0].astype(BF16), preferred_element_type=F32)
    ocmp_ref[0] = o.reshape(N_QHEADS, tq, LANES)
    for kvh in range(NSA_KV_HEADS):
        g0 = kvh * NSA_GROUP
        imp = jnp.where(vis, p[g0] + p[g0 + 1] + p[g0 + 2] + p[g0 + 3], -1.0)
        rank = jnp.zeros((tq, n_cb), jnp.int32)
        for i in range(n_cb):
            col = imp[:, i:i + 1]
            wins_tie = jnp.where(blk > i, 1, 0)
            rank = rank + jnp.where(col > imp, 1, jnp.where(col == imp, wins_tie, 0))
        chosen = jnp.where(vis, jnp.where(rank < NSA_N_SEL - 1, 1.0, 0.0), jnp.where(blk == cur, 1.0, 0.0))
        sel_ref[0, kvh] = chosen.astype(BF16)


def _cmp_select(qn, kc, vc, pos0):
    B, _, T, _ = qn.shape
    n_cb = kc.shape[1]
    tq = min(T, Q_BLOCK)
    return pl.pallas_call(
        functools.partial(_cmp_select_kernel, tq=tq, pos0=pos0),
        out_shape=[jax.ShapeDtypeStruct((B, N_QHEADS, T, LANES), F32),
                   jax.ShapeDtypeStruct((B, NSA_KV_HEADS, T, n_cb), BF16)],
        grid=(B, T // tq),
        in_specs=[
            pl.BlockSpec((1, N_QHEADS, tq, LANES), lambda b, i: (b, 0, i, 0)),
            pl.BlockSpec((1, n_cb, LANES), lambda b, i: (b, 0, 0)),
            pl.BlockSpec((1, n_cb, LANES), lambda b, i: (b, 0, 0)),
        ],
        out_specs=[pl.BlockSpec((1, N_QHEADS, tq, LANES), lambda b, i: (b, 0, i, 0)),
                   pl.BlockSpec((1, NSA_KV_HEADS, tq, n_cb), lambda b, i: (b, 0, i, 0))],
        compiler_params=pltpu.CompilerParams(
            dimension_semantics=("parallel", "parallel"), vmem_limit_bytes=VMEM_LIMIT_BYTES),
        name="cmp_select",
    )(qn, kc, vc)


def _selection_bias_kv(sel, first_key, n_keys, causal):
    n_cb = sel.shape[-1]
    key_blk = (first_key + lax.broadcasted_iota(jnp.int32, (1, n_keys), 1)) // NSA_BLOCK
    expand = jnp.where(lax.broadcasted_iota(jnp.int32, (n_cb, 1), 0) == key_blk, 1.0, 0.0).astype(BF16)
    out = []
    for kvh in range(NSA_KV_HEADS):
        picked = jnp.dot(sel[kvh], expand, preferred_element_type=F32)
        bias = (picked - 1.0) * 1e30
        if causal is not None:
            bias = jnp.where(causal, bias, NEG_INF)
        out.append(bias)
    return out


def _selection_bias(sel, first_key, n_keys, causal):
    return jnp.concatenate([jnp.broadcast_to(b[None], (NSA_GROUP,) + b.shape)
                            for b in _selection_bias_kv(sel, first_key, n_keys, causal)], axis=0)


def _softmax_tile_chunked(s_ref, bias_ref, heads_per_bias, p_ref, v, m_ref, l_ref, acc_ref, tq, n):
    n_chunks = n // LANES
    for h in range(N_QHEADS):
        rows = slice(h * tq, (h + 1) * tq)
        b = h // heads_per_bias

        def biased(c, rows=rows, b=b):
            cols = slice(c * LANES, (c + 1) * LANES)
            return s_ref[rows, cols] + bias_ref[b, :, cols]

        top = biased(0)
        for c in range(1, n_chunks):
            top = jnp.maximum(top, biased(c))
        m_old = m_ref[h]
        m_new = jnp.maximum(m_old, jnp.max(top, axis=-1, keepdims=True))
        alpha = jnp.exp(m_old - m_new)
        total = None
        for c in range(n_chunks):
            p = jnp.exp(biased(c) - m_new)
            p_ref[rows, c * LANES:(c + 1) * LANES] = p.astype(BF16)
            total = p if total is None else total + p
        l_ref[h] = alpha * l_ref[h] + jnp.sum(total, axis=-1, keepdims=True)
        acc_ref[h] = alpha * acc_ref[h]
        m_ref[h] = m_new
    pv = jnp.dot(p_ref[:, :n], v, preferred_element_type=F32)
    acc_ref[...] = acc_ref[...] + pv.reshape(N_QHEADS, tq, LANES)


def _online_softmax_step(s, v, m_ref, l_ref, acc_ref, tq):
    n = s.shape[-1]
    m_old = m_ref[...]
    m_new = jnp.maximum(m_old, jnp.max(s, axis=-1, keepdims=True))
    alpha = jnp.exp(m_old - m_new)
    p = jnp.exp(s - m_new[..., :1])
    l_ref[...] = alpha * l_ref[...] + jnp.sum(p, axis=-1, keepdims=True)
    pv = jnp.dot(p.reshape(N_QHEADS * tq, n).astype(BF16), v, preferred_element_type=F32)
    acc_ref[...] = alpha * acc_ref[...] + pv.reshape(N_QHEADS, tq, LANES)
    m_ref[...] = m_new


def _init_softmax_state(m_ref, l_ref, acc_ref):
    m_ref[...] = jnp.full(m_ref.shape, NEG_INF, F32)
    l_ref[...] = jnp.zeros(l_ref.shape, F32)
    acc_ref[...] = jnp.zeros(acc_ref.shape, F32)


def _gated_heads(gate_pre, o_cmp, o_slc, o_win):
    g = jax.nn.sigmoid(gate_pre)
    lane = lax.broadcasted_iota(jnp.int32, (1, LANES), 1)
    pairs = []
    for c in range(N_QHEADS // 2):
        mixed = []
        for h in (2 * c, 2 * c + 1):
            mixed.append(g[:, 3 * h:3 * h + 1] * o_cmp[h] + g[:, 3 * h + 1:3 * h + 2] * o_slc[h]
                         + g[:, 3 * h + 2:3 * h + 3] * o_win[h])
        if 2 * c < N_QHEADS // 2:
            pairs.append(jnp.where(lane < HALF, mixed[0], pltpu.roll(mixed[1], HALF, 1)))
        else:
            pairs.append(jnp.where(lane < HALF, pltpu.roll(mixed[0], HALF, 1), mixed[1]))
    return jnp.concatenate(pairs, axis=-1)


SLC_KEY_TILE = 512


def _nsa_prompt_kernel(q_ref, ks_ref, vs_ref, kw_ref, vw_ref, sel_ref, ocmp_ref, g_ref, o_ref,
                       m_ref, l_ref, acc_ref, s_ref, p_ref, bias_ref, *, tq, tk, span):
    T = ks_ref.shape[1]
    q0 = pl.program_id(1) * tq
    q = q_ref[0].reshape(N_QHEADS * tq, LANES)
    qpos = q0 + lax.broadcasted_iota(jnp.int32, (tq, 1), 0)
    sel = sel_ref[0]
    _init_softmax_state(m_ref, l_ref, acc_ref)

    def key_tile(j, carry):
        k0 = pl.multiple_of(j * tk, tk)
        kpos = k0 + lax.broadcasted_iota(jnp.int32, (1, tk), 1)
        s_ref[:, :tk] = _nt_dot(q, ks_ref[0, pl.ds(k0, tk), :])
        for kvh, bias in enumerate(_selection_bias_kv(sel, k0, tk, kpos <= qpos)):
            bias_ref[kvh, :, :tk] = bias
        _softmax_tile_chunked(s_ref, bias_ref, NSA_GROUP, p_ref, vs_ref[0, pl.ds(k0, tk), :],
                              m_ref, l_ref, acc_ref, tq, tk)
        return carry

    lax.fori_loop(0, (q0 + tq + tk - 1) // tk, key_tile, 0)
    o_slc = acc_ref[...] / l_ref[...]

    if span == T:
        start = 0
        kw, vw = kw_ref[0], vw_ref[0]
    else:
        start = pl.multiple_of(jnp.maximum(q0 - NSA_WINDOW, 0), LANES)
        kw, vw = kw_ref[0, pl.ds(start, span), :], vw_ref[0, pl.ds(start, span), :]
    kpos = start + lax.broadcasted_iota(jnp.int32, (1, span), 1)
    _init_softmax_state(m_ref, l_ref, acc_ref)
    s_ref[:, :span] = _nt_dot(q, kw)
    bias_ref[0, :, :span] = jnp.where(_band_mask(qpos, kpos, NSA_WINDOW, 1), 0.0, NEG_INF)
    _softmax_tile_chunked(s_ref, bias_ref, N_QHEADS, p_ref, vw, m_ref, l_ref, acc_ref, tq, span)
    o_win = acc_ref[...] / l_ref[...]
    o_ref[0] = _gated_heads(g_ref[0], ocmp_ref[0], o_slc, o_win)


def _nsa_prompt(qr, ks, vs, kw, vw, sel, ocmp, gate_pre):
    B, _, T, _ = qr.shape
    n_cb = sel.shape[-1]
    tq = min(T, Q_BLOCK)
    tk = min(T, SLC_KEY_TILE)
    span = min(T, NSA_WINDOW + tq)
    seq = pl.BlockSpec((1, T, LANES), lambda b, i: (b, 0, 0))
    heads = pl.BlockSpec((1, N_QHEADS, tq, LANES), lambda b, i: (b, 0, i, 0))
    stat = pltpu.VMEM((N_QHEADS, tq, LANES), F32)
    return pl.pallas_call(
        functools.partial(_nsa_prompt_kernel, tq=tq, tk=tk, span=span),
        out_shape=jax.ShapeDtypeStruct((B, T, NSA_Q_W), F32),
        grid=(B, T // tq),
        in_specs=[heads, seq, seq, seq, seq,
                  pl.BlockSpec((1, NSA_KV_HEADS, tq, n_cb), lambda b, i: (b, 0, i, 0)),
                  heads,
                  pl.BlockSpec((1, tq, gate_pre.shape[2]), lambda b, i: (b, i, 0))],
        out_specs=pl.BlockSpec((1, tq, NSA_Q_W), lambda b, i: (b, i, 0)),
        scratch_shapes=[stat, stat, stat,
                        pltpu.VMEM((N_QHEADS * tq, max(tk, span)), F32),
                        pltpu.VMEM((N_QHEADS * tq, max(tk, span)), BF16),
                        pltpu.VMEM((NSA_KV_HEADS, tq, max(tk, span)), F32)],
        compiler_params=pltpu.CompilerParams(
            dimension_semantics=("parallel", "arbitrary"), vmem_limit_bytes=VMEM_LIMIT_BYTES),
        name="nsa_prompt",
    )(qr, ks, vs, kw, vw, sel, ocmp, gate_pre)


def _nsa_paged_kernel(pt_ref, q_ref, *refs, tq, pos0):
    page_refs = refs[:PAGES_PER_STEP]
    new_ref, sel_ref, ocmp_ref, owin_ref, g_ref, o_ref, m_ref, l_ref, acc_ref = refs[PAGES_PER_STEP:]
    step = pl.program_id(1)
    q = q_ref[0].reshape(N_QHEADS * tq, LANES).astype(BF16)
    sel = sel_ref[0]

    @pl.when(step == 0)
    def _():
        _init_softmax_state(m_ref, l_ref, acc_ref)

    def attend(k, v, first_key, causal):
        n = k.shape[0]
        s = _nt_dot(q, k).reshape(N_QHEADS, tq, n)
        if causal is None:
            s = s + _selection_bias(sel, first_key, n, None)
        else:
            s = jnp.where(causal[None], s, NEG_INF)
        _online_softmax_step(s, v, m_ref, l_ref, acc_ref, tq)

    attend(jnp.concatenate([r[0, :, :LANES].astype(BF16) for r in page_refs], axis=0),
           jnp.concatenate([r[0, :, LANES:].astype(BF16) for r in page_refs], axis=0),
           step * (PAGES_PER_STEP * PAGE_SIZE), None)

    @pl.when(step == pl.num_programs(1) - 1)
    def _():
        qpos = lax.broadcasted_iota(jnp.int32, (tq, 1), 0)
        kpos = lax.broadcasted_iota(jnp.int32, (1, PAGE_SIZE), 1)
        new = new_ref[0]
        attend(new[:, :LANES].astype(BF16), new[:, LANES:].astype(BF16), pos0, kpos <= qpos)
        o_ref[0] = _gated_heads(g_ref[0], ocmp_ref[0], acc_ref[...] / l_ref[...], owin_ref[0])


def _nsa_paged(page_table, qr, cache, new_tile, sel, ocmp, owin, gate_pre, pos0):
    B, _, T, _ = qr.shape
    n_pages = page_table.shape[1]
    n_cb = sel.shape[-1]
    W = cache.shape[2]
    heads = pl.BlockSpec((1, N_QHEADS, T, LANES), lambda b, s, pt: (b, 0, 0, 0))
    stat = pltpu.VMEM((N_QHEADS, T, LANES), F32)
    grid_spec = pltpu.PrefetchScalarGridSpec(
        num_scalar_prefetch=1,
        grid=(B, n_pages // PAGES_PER_STEP),
        in_specs=[heads] + _page_specs(W) + [
            pl.BlockSpec((1, PAGE_SIZE, W), lambda b, s, pt: (b, 0, 0)),
            pl.BlockSpec((1, NSA_KV_HEADS, T, n_cb), lambda b, s, pt: (b, 0, 0, 0)),
            heads, heads,
            pl.BlockSpec((1, T, gate_pre.shape[2]), lambda b, s, pt: (b, 0, 0))],
        out_specs=pl.BlockSpec((1, T, NSA_Q_W), lambda b, s, pt: (b, 0, 0)),
        scratch_shapes=[stat, stat, stat],
    )
    return pl.pallas_call(
        functools.partial(_nsa_paged_kernel, tq=T, pos0=pos0),
        out_shape=jax.ShapeDtypeStruct((B, T, NSA_Q_W), F32),
        grid_spec=grid_spec,
        compiler_params=pltpu.CompilerParams(
            dimension_semantics=("parallel", "arbitrary"), vmem_limit_bytes=VMEM_LIMIT_BYTES),
        name="nsa_paged",
    )(page_table, qr, *([cache] * PAGES_PER_STEP), new_tile, sel, ocmp, owin, gate_pre)


QPAD_W = N_QHEADS * LANES


def _rope_tables(pos):
    half = ROT_DIM // 2
    inv = jnp.exp(-math.log(ROPE_THETA) * jnp.arange(half, dtype=F32) * (2.0 / ROT_DIM))
    ang = pos.astype(F32)[:, None] * inv[None, :]
    cos, sin = jnp.cos(ang), jnp.sin(ang)
    T = pos.shape[0]
    zeros = lambda n: jnp.zeros((T, n), F32)
    cos_t = jnp.concatenate([cos, cos, jnp.ones((T, HALF - ROT_DIM), F32)], axis=-1)
    msin_t = jnp.concatenate([-sin, zeros(HALF - half)], axis=-1)
    psin_t = jnp.concatenate([zeros(half), sin, zeros(HALF - ROT_DIM)], axis=-1)
    twice = lambda a: jnp.concatenate([a, a], axis=-1)
    return twice(cos_t), twice(msin_t), twice(psin_t)


def _rope_lanes(x, cos_t, msin_t, psin_t):
    half = ROT_DIM // 2
    return x * cos_t + pltpu.roll(x, LANES - half, 1) * msin_t + pltpu.roll(x, half, 1) * psin_t


def _head_rms(x, ones_mat, gain):
    return x * lax.rsqrt(_head_mean_sq(x, ones_mat) + NORM_EPS) * gain


def _pad_head_columns(w):
    D = w.shape[0]
    w = w.reshape(D, N_QHEADS, HALF)
    z = jnp.zeros_like(w)
    lower = (jnp.arange(N_QHEADS) < N_QHEADS // 2)[None, :, None]
    return jnp.concatenate([jnp.where(lower, w, z), jnp.where(lower, z, w)], axis=-1).reshape(D, QPAD_W)


def _pad_head_vector(g):
    return _pad_head_columns(jnp.tile(g, N_QHEADS)[None, :])


def _ab_proj_kernel(x_ref, g_ref, sc_ref, sh_ref, w_ref, cos_ref, msin_ref, psin_ref, gq_ref, gks_ref, gkw_ref,
                    ones_ref, bd_ref, lng_ref, lnb_ref, wm_ref, bias_ref,
                    qn_ref, qr_ref, kvc_ref, kvs_ref, ks_ref, vs_ref, kvw_ref, kw_ref, vw_ref, gate_ref,
                    ob_ref, vln_ref, *, tm):
    h = _mod_norm(x_ref[0], g_ref[...], sc_ref[0], sh_ref[0])
    y = jnp.dot(h.astype(BF16), w_ref[...], preferred_element_type=F32)
    tabs = (cos_ref[...], msin_ref[...], psin_ref[...])
    for hd in range(N_QHEADS):
        lanes = slice(hd * LANES, (hd + 1) * LANES)
        qn = _head_rms(y[:, lanes], ones_ref[...], gq_ref[:, lanes])
        qn_ref[0, hd] = qn
        qr_ref[0, hd] = (_rope_lanes(qn, *tabs) * (HEAD_DIM ** -0.5)).astype(BF16)
    off = QPAD_W
    kvc_ref[0] = y[:, off:off + NSA_KV_W]
    off += NSA_KV_W
    for kv_ref, k_ref, v_ref, gain_ref in ((kvs_ref, ks_ref, vs_ref, gks_ref), (kvw_ref, kw_ref, vw_ref, gkw_ref)):
        k = _rope_lanes(_head_rms(y[:, off:off + LANES], bd_ref[...], gain_ref[...]), *tabs)
        v = y[:, off + LANES:off + NSA_KV_W]
        kv_ref[0, :, :LANES] = k
        kv_ref[0, :, LANES:] = v
        k_ref[0] = k.astype(BF16)
        v_ref[0] = v.astype(BF16)
        off += NSA_KV_W
    u = jax.nn.gelu(y[:, off:off + GM_WIDTH])
    v = jax.nn.gelu(y[:, off + GM_WIDTH:off + 2 * GM_WIDTH])
    off += 2 * GM_WIDTH
    gate_ref[0] = y[:, off:]
    mu = jnp.mean(v, axis=-1, keepdims=True)
    var = jnp.mean(jnp.square(v - mu), axis=-1, keepdims=True)
    vln = (v - mu) * lax.rsqrt(var + NORM_EPS) * lng_ref[...] + lnb_ref[...]
    vln_ref[0] = vln
    lane = lax.broadcasted_iota(jnp.int32, (1, LANES), 1)
    for c in range(tm // GM_CHUNK):
        rows = slice(c * GM_CHUNK, (c + 1) * GM_CHUNK)
        mixed = []
        for p in range(GM_GROUPS // 2):
            vp = vln[rows, p * LANES:(p + 1) * LANES]
            lo = jnp.where(lane < HALF, vp, 0.0).astype(BF16)
            hi = jnp.where(lane < HALF, 0.0, vp).astype(BF16)
            mixed.append(jnp.dot(wm_ref[2 * p], lo, preferred_element_type=F32)
                         + jnp.dot(wm_ref[2 * p + 1], hi, preferred_element_type=F32))
        ob_ref[0, rows, :] = u[rows] * (jnp.concatenate(mixed, axis=-1) + bias_ref[...])


def _ab_proj(x, norm_g, scale, shift, w_in, pos, qk_gain, ln_g, ln_b, wm, mix_bias):
    B, T, D = x.shape
    tm = _row_tile(T)
    g0, g1 = AB_SPLITS[0], AB_SPLITS[0] + AB_SPLITS[1]
    w = jnp.concatenate([_pad_head_columns(w_in[:, :g0]), w_in[:, g1:], w_in[:, g0:g1]], axis=1).astype(BF16)
    N = w.shape[1]
    cos_t, msin_t, psin_t = _rope_tables(pos)
    ones_mat = jnp.full((LANES, LANES), 1.0 / HALF, BF16)
    bd = _block_diag2(jnp.full((HALF, HALF), 1.0 / HALF, F32)).astype(BF16)
    two = lambda g: jnp.concatenate([g, g])[None, :]
    const = lambda a: pl.BlockSpec(a.shape, lambda b, i: (0,) * a.ndim)
    tok = lambda n: pl.BlockSpec((1, tm, n), lambda b, i: (b, i, 0))
    heads = pl.BlockSpec((1, N_QHEADS, tm, LANES), lambda b, i: (b, 0, i, 0))
    table = pl.BlockSpec((tm, LANES), lambda b, i: (i, 0))
    consts = [_pad_head_vector(qk_gain[0]), two(qk_gain[2]), two(qk_gain[3]), ones_mat, bd,
              ln_g[None, :], ln_b[None, :], wm, mix_bias]
    sds = lambda shape, dt: jax.ShapeDtypeStruct(shape, dt)
    return pl.pallas_call(
        functools.partial(_ab_proj_kernel, tm=tm),
        out_shape=[sds((B, N_QHEADS, T, LANES), F32), sds((B, N_QHEADS, T, LANES), BF16),
                   sds((B, T, NSA_KV_W), F32),
                   sds((B, T, NSA_KV_W), F32), sds((B, T, LANES), BF16), sds((B, T, LANES), BF16),
                   sds((B, T, NSA_KV_W), F32), sds((B, T, LANES), BF16), sds((B, T, LANES), BF16),
                   sds((B, T, 3 * NSA_HEADS), F32), sds((B, T, GM_WIDTH), F32), sds((B, T, GM_WIDTH), F32)],
        grid=(B, T // tm),
        in_specs=[tok(D), pl.BlockSpec((1, D), lambda b, i: (0, 0)), _mod_spec(scale, tm), _mod_spec(shift, tm),
                  pl.BlockSpec((D, N), lambda b, i: (0, 0)), table, table, table] + [const(a) for a in consts],
        out_specs=[heads, heads, tok(NSA_KV_W), tok(NSA_KV_W), tok(LANES), tok(LANES), tok(NSA_KV_W), tok(LANES),
                   tok(LANES), tok(3 * NSA_HEADS), tok(GM_WIDTH), tok(GM_WIDTH)],
        compiler_params=pltpu.CompilerParams(
            dimension_semantics=("parallel", "parallel"), vmem_limit_bytes=VMEM_LIMIT_BYTES),
        name="ab_proj",
    )(x, norm_g.reshape(1, D), scale, shift, w, cos_t, msin_t, psin_t, *consts)


def _chunk_mix_weights(ws, bs, rows_per_seq):
    n = min(rows_per_seq, GM_CHUNK)
    wm = jnp.where(jnp.tril(jnp.ones((n, n), bool)), ws[:, :n, :n], 0.0)
    if n < GM_CHUNK:
        eye = jnp.eye(GM_CHUNK // n, dtype=F32)
        wm = jnp.einsum('ab,gij->gaibj', eye, wm).reshape(GM_GROUPS, GM_CHUNK, GM_CHUNK)
    bias = jnp.tile(jnp.transpose(bs[:, :n]), (GM_CHUNK // n, 1))
    return wm.astype(BF16), jnp.repeat(bias, GM_GROUP_W, axis=1)


C_GROUP_W = QPAD_W + 2 * DIL_KV_W


def _c_proj_kernel(x_ref, g_ref, sc_ref, sh_ref, w_ref, cos_ref, msin_ref, psin_ref, gq_ref, gk_ref,
                   ones_ref, bd_ref, *refs, tm, dils):
    out_refs, stage = refs[:-1], refs[-1]
    h = _mod_norm(x_ref[0], g_ref[...], sc_ref[0], sh_ref[0])
    y = jnp.dot(h.astype(BF16), w_ref[...], preferred_element_type=F32)
    tabs = (cos_ref[...], msin_ref[...], psin_ref[...])

    def put(dst, val, d):
        if d == 1:
            dst[0] = val.astype(dst.dtype)
            return
        stage[...] = val
        for r in range(d):
            dst[r] = stage[pl.ds(r, tm // d, stride=d), :].astype(dst.dtype)

    for g, d in enumerate(dils):
        qd_ref, kd_ref, vd_ref, kv_ref = out_refs[4 * g:4 * g + 4]
        off = g * C_GROUP_W
        for hd in range(N_QHEADS):
            lanes = slice(off + hd * LANES, off + (hd + 1) * LANES)
            q = _rope_lanes(_head_rms(y[:, lanes], ones_ref[...], gq_ref[:, hd * LANES:(hd + 1) * LANES]), *tabs)
            put(qd_ref.at[:, hd], q * (HEAD_DIM ** -0.5), d)
        k = _rope_lanes(_head_rms(y[:, off + QPAD_W:off + QPAD_W + LANES], bd_ref[...], gk_ref[...]), *tabs)
        v = y[:, off + QPAD_W + LANES:off + C_GROUP_W]
        kv_ref[0, :, :LANES] = k
        kv_ref[0, :, LANES:] = v
        put(kd_ref, k, d)
        put(vd_ref, v, d)


def _c_proj(x, norm_g, scale, shift, w_in, pos, qk_gain, dils):
    B, T, D = x.shape
    tm = _row_tile(T)
    pieces = []
    for g in range(len(dils)):
        c0 = g * (DIL_Q_W + 2 * DIL_KV_W)
        pieces += [_pad_head_columns(w_in[:, c0:c0 + DIL_Q_W]), w_in[:, c0 + DIL_Q_W:c0 + DIL_Q_W + 2 * DIL_KV_W]]
    w = jnp.concatenate(pieces, axis=1).astype(BF16)
    N = w.shape[1]
    cos_t, msin_t, psin_t = _rope_tables(pos)
    ones_mat = jnp.full((LANES, LANES), 1.0 / HALF, BF16)
    bd = _block_diag2(jnp.full((HALF, HALF), 1.0 / HALF, F32)).astype(BF16)
    consts = [_pad_head_vector(qk_gain[0]), jnp.concatenate([qk_gain[1], qk_gain[1]])[None, :], ones_mat, bd]
    const = lambda a: pl.BlockSpec(a.shape, lambda b, i: (0,) * a.ndim)
    tok = lambda n: pl.BlockSpec((1, tm, n), lambda b, i: (b, i, 0))
    table = pl.BlockSpec((tm, LANES), lambda b, i: (i, 0))
    sds = lambda shape, dt: jax.ShapeDtypeStruct(shape, dt)
    out_shape, out_specs = [], []
    for d in dils:
        out_shape += [sds((B * d, N_QHEADS, T // d, LANES), BF16), sds((B * d, T // d, LANES), BF16),
                      sds((B * d, T // d, LANES), BF16), sds((B, T, 2 * DIL_KV_W), F32)]
        seq = pl.BlockSpec((d, tm // d, LANES), lambda b, i: (b, i, 0))
        out_specs += [pl.BlockSpec((d, N_QHEADS, tm // d, LANES), lambda b, i: (b, 0, i, 0)), seq, seq,
                      tok(2 * DIL_KV_W)]
    return pl.pallas_call(
        functools.partial(_c_proj_kernel, tm=tm, dils=tuple(dils)),
        out_shape=out_shape,
        grid=(B, T // tm),
        in_specs=[tok(D), pl.BlockSpec((1, D), lambda b, i: (0, 0)), _mod_spec(scale, tm), _mod_spec(shift, tm),
                  pl.BlockSpec((D, N), lambda b, i: (0, 0)), table, table, table] + [const(a) for a in consts],
        out_specs=out_specs,
        scratch_shapes=[pltpu.VMEM((tm, LANES), F32)],
        compiler_params=pltpu.CompilerParams(
            dimension_semantics=("parallel", "parallel"), vmem_limit_bytes=VMEM_LIMIT_BYTES),
        name="c_proj",
    )(x, norm_g.reshape(1, D), scale, shift, w, cos_t, msin_t, psin_t, *consts)


def _dil_merge_proj_kernel(*refs, tm, dils):
    o_refs = refs[:len(dils)]
    w_ref, x_ref, gate_ref, out_ref, stage = refs[len(dils):]
    lane = lax.broadcasted_iota(jnp.int32, (1, LANES), 1)
    merged = []
    for hd in range(N_QHEADS):
        lower = hd < N_QHEADS // 2
        vals = []
        for o_ref, d in zip(o_refs, dils):
            if d == 1:
                vals.append(o_ref[0, hd])
            else:
                for r in range(d):
                    stage[pl.ds(r, tm // d, stride=d), :] = o_ref[r, hd]
                vals.append(stage[...])
        lses = [v[:, HALF:HALF + 1] if lower else v[:, 0:1] for v in vals]
        top = jnp.maximum(jnp.maximum(lses[0], lses[1]), lses[2])
        ws = [jnp.exp(l - top) for l in lses]
        total = ws[0] + ws[1] + ws[2]
        merged.append((ws[0] * vals[0] + ws[1] * vals[1] + ws[2] * vals[2]) / total)
    pairs = []
    for c in range(N_QHEADS // 2):
        a, b = merged[2 * c], merged[2 * c + 1]
        if 2 * c < N_QHEADS // 2:
            pairs.append(jnp.where(lane < HALF, a, pltpu.roll(b, HALF, 1)))
        else:
            pairs.append(jnp.where(lane < HALF, pltpu.roll(a, HALF, 1), b))
    o = jnp.concatenate(pairs, axis=-1)
    y = jnp.dot(o.astype(BF16), w_ref[...], preferred_element_type=F32)
    out_ref[0] = x_ref[0] + gate_ref[0] * y


def _dil_merge_proj(outs, dils, w_out, x, gate):
    B, T, D = x.shape
    tm = _row_tile(T)
    in_specs = [pl.BlockSpec((d, N_QHEADS, tm // d, LANES), lambda b, i: (b, 0, i, 0)) for d in dils]
    in_specs += [pl.BlockSpec(w_out.shape, lambda b, i: (0, 0)),
                 pl.BlockSpec((1, tm, D), lambda b, i: (b, i, 0)), _mod_spec(gate, tm)]
    return pl.pallas_call(
        functools.partial(_dil_merge_proj_kernel, tm=tm, dils=tuple(dils)),
        out_shape=jax.ShapeDtypeStruct((B, T, D), F32),
        grid=(B, T // tm),
        in_specs=in_specs,
        out_specs=pl.BlockSpec((1, tm, D), lambda b, i: (b, i, 0)),
        scratch_shapes=[pltpu.VMEM((tm, LANES), F32)],
        compiler_params=pltpu.CompilerParams(
            dimension_semantics=("parallel", "parallel"), vmem_limit_bytes=VMEM_LIMIT_BYTES),
        name="dil_merge_proj",
    )(*outs, w_out, x, gate)


def _kv5(x):
    return x.reshape(x.shape[:2] + (2, NSA_KV_HEADS, HEAD_DIM))


def _mixer_ab(x, norm_g, sc, sh, gate_m, w_in, w_out, start, pasts, weights, page_table, seq_shape):
    qk_gain, pe, w_phi, ln_g, ln_b, ws, bs = weights
    B, T = seq_shape
    pos = start + jnp.arange(x.shape[1], dtype=jnp.int32) % T
    wm, mix_bias = _chunk_mix_weights(ws, bs, T)
    qn, qr, kvc, kvs, ks, vs, kvw, kw, vw, gate, o_b, vln = _ab_proj(
        x, norm_g, sc, sh, w_in, pos, qk_gain, ln_g, ln_b, wm, mix_bias)
    if pasts is None:
        kc, vc = _cmp_finish(_block_mean(kvc), pe, w_phi, qk_gain[1])
        ocmp, sel = _cmp_select(qn, kc, vc, start)
        o_a = _nsa_prompt(qr, ks, vs, kw, vw, sel, ocmp, gate)
        kvw_all = kvw
    else:
        per_seq = lambda a: a.reshape((B, T) + a.shape[2:])
        heads = lambda a: a.reshape(N_QHEADS, B, T, LANES).transpose(1, 0, 2, 3)
        kvc, kvs, kvw, gate, vln = (per_seq(a) for a in (kvc, kvs, kvw, gate, vln))
        qn, qr = heads(qn), heads(qr.astype(F32))
        cache_cmp, cache_slc, win_past = pasts
        kc, vc = _cmp_finish(_page_block_mean(cache_cmp, page_table), pe, w_phi, qk_gain[1])
        ocmp, sel = _cmp_select(qn, kc, vc, start)
        kvw_all = jnp.concatenate([win_past, kvw], axis=1)
        owin = _band_tail(qr, kvw_all, NSA_WINDOW, 1, False)
        new_tile = jnp.pad(kvs, ((0, 0), (0, PAGE_SIZE - T), (0, 0)))
        o_a = _nsa_paged(page_table, qr, cache_slc, new_tile, sel, ocmp, owin, gate, start)
    keep_w = min(NSA_WINDOW, kvw_all.shape[1])
    n_cur = (T - 1) % GM_CHUNK + 1
    x_new = _proj_residual([o_a.reshape(o_b.shape), o_b], (w_out[:NSA_Q_W], w_out[NSA_Q_W:]), x, gate_m)
    return x_new, (_kv5(kvc), _kv5(kvs), _kv5(kvw_all[:, kvw_all.shape[1] - keep_w:]), vln[:, T - n_cur:])


def _mixer_c(x, norm_g, sc, sh, gate_m, w_in, w_out, start, pasts, qk_gain, seq_shape):
    B, T = seq_shape
    pos = start + jnp.arange(x.shape[1], dtype=jnp.int32) % T
    dils = tuple(d for _, d in DIL_CFG) if pasts is None else (1,) * len(DIL_CFG)
    res = _c_proj(x, norm_g, sc, sh, w_in, pos, qk_gain, dils)
    outs, new_bufs = [], []
    for g, (window, dilation) in enumerate(DIL_CFG):
        qd, kd, vd, kv_new = res[4 * g:4 * g + 4]
        if pasts is None:
            outs.append(_band_self(qd, kd, vd, window // dilation, True))
            kv_all = kv_new
        else:
            kv_all = jnp.concatenate([pasts[g], kv_new.reshape(B, T, 2 * DIL_KV_W)], axis=1)
            q = qd.reshape(N_QHEADS, B, T, LANES).transpose(1, 0, 2, 3).astype(F32)
            o = _band_tail(q, kv_all, window, dilation, True)
            outs.append(o.transpose(1, 0, 2, 3).reshape(1, N_QHEADS, B * T, LANES))
        keep = min(window, kv_all.shape[1])
        new_bufs.append(kv_all[:, kv_all.shape[1] - keep:].reshape(B, keep, 2, DIL_KV_HEADS, HEAD_DIM))
    return _dil_merge_proj(outs, dils, w_out, x, gate_m), tuple(new_bufs)


def _adaln_kernel(c_ref, w_ref, b_ref, o_ref):
    act = jax.nn.silu(c_ref[...]).astype(BF16)
    o_ref[...] = jnp.dot(act, w_ref[0].astype(BF16), preferred_element_type=F32) + b_ref[0]


def _adaln(c, layer, w, b):
    B, D = c.shape
    depth, _, N = w.shape
    tn = D
    mod = pl.pallas_call(
        _adaln_kernel,
        out_shape=jax.ShapeDtypeStruct((B, N), F32),
        grid=(N // tn,),
        in_specs=[pl.BlockSpec((B, D), lambda j: (0, 0)), pl.BlockSpec((1, D, tn), lambda j: (layer, 0, j)),
                  pl.BlockSpec((1, 1, tn), lambda j: (layer, 0, j))],
        out_specs=pl.BlockSpec((B, tn), lambda j: (0, j)),
        compiler_params=pltpu.CompilerParams(dimension_semantics=("parallel",)),
        name="adaln",
    )(c, w, b.reshape(depth, 1, N))
    return [mod[:, None, j * D:(j + 1) * D] for j in range(N // D)]


def _expand_mod(m, t):
    B, _, D = m.shape
    return jnp.broadcast_to(m, (B, t, D)).reshape(1, B * t, D)


def _layer(groups, mixers, norm_m, norm_f, moe_w):
    w_router, b_router, *ffn_w = moe_w
    xs, states, routed = [], [], []
    counts = jnp.zeros((1, N_EXPERTS), F32)
    for (x, mods), mixer in zip(groups, mixers):
        sh_m, sc_m, g_m, sh_f, sc_f, g_f = mods
        B, T, D = x.shape
        x, state = mixer(x, norm_m, sc_m, sh_m, g_m)
        h, route_i, route_g, counts = _mod_norm_router(x, norm_f, sc_f, sh_f, w_router, b_router, counts)
        xs.append(x)
        states.append(state)
        routed.append((h, route_i, route_g))
    n_asg = sum(x.shape[0] * x.shape[1] for x in xs) * TOP_K
    n_tiles = -(-(n_asg + N_EXPERTS * (MOE_TILE - 1)) // MOE_TILE)
    pad_start, pad_end, n_used = _moe_plan(counts[0])
    x_sorted = jnp.zeros((n_tiles * MOE_TILE, xs[0].shape[2]), F32)
    dests = []
    for h, route_i, _ in routed:
        expert = route_i[..., :TOP_K]
        rank = route_i[..., ROUTE_RANK_LANE:ROUTE_RANK_LANE + TOP_K]
        first = jnp.sum(jnp.where(expert[..., None] == jnp.arange(N_EXPERTS), pad_start, 0), axis=-1)
        dests.append((first + rank).astype(jnp.int32))
        x_sorted = _moe_dispatch(dests[-1], h, x_sorted)
    y_sorted = _moe_ffn(*_moe_tiles(pad_end, n_used, n_tiles), x_sorted, *ffn_w)
    outs = [_moe_combine(dest, y_sorted, route_g, x, mods[5])
            for dest, (_, _, route_g), x, (_, mods) in zip(dests, routed, xs, groups)]
    return outs, states


def _stack(states, j):
    return jnp.stack([s[j] for s in states], axis=0)


def kernel(x_prompt, x_sample, cache_cmp_kv, cache_slc_kv, state_win_kv, state_dil0_kv, state_dil1_kv,
           state_dil2_kv, page_table, c_prompt, c_sample, norm_mix, norm_ffn, w_ada, b_ada, w_in_ab, w_out_ab,
           nsa_qk_gain, nsa_pe, nsa_w_phi, gm_ln_g, gm_ln_b, gm_ws, gm_bs, w_in_c, w_out_c, dil_qk_gain,
           w_router, b_router, w_up, b_up, w_down, b_down):
    depth = norm_mix.shape[0]
    past_len = page_table.shape[1] * PAGE_SIZE
    Bs, Ts, D = x_sample.shape
    x_p = x_prompt
    x_s = x_sample.reshape(1, Bs * Ts, D)
    ab_p, ab_s, dil_p, dil_s = [], [], [], []
    for layer in range(depth):
        i = layer // 2
        moe_w = (w_router[layer], b_router[layer], layer, w_up, b_up, w_down, b_down)
        mods_p = _adaln(c_prompt, layer, w_ada, b_ada)
        mods_s = [_expand_mod(m, Ts) for m in _adaln(c_sample, layer, w_ada, b_ada)]
        groups = [(x_p, mods_p), (x_s, mods_s)]
        if layer % 2 == 0:
            wts = (nsa_qk_gain[i], nsa_pe[i], nsa_w_phi[i], gm_ln_g[i], gm_ln_b[i], gm_ws[i], gm_bs[i])
            flat_kv = lambda a: a.reshape(a.shape[:2] + (NSA_KV_W,))
            pasts_s = (flat_kv(cache_cmp_kv[i]), flat_kv(cache_slc_kv[i]), flat_kv(state_win_kv[i]))

            w_io = (w_in_ab[i], w_out_ab[i].astype(BF16))

            def mix_p(x, g, sc, sh, gm, wts=wts, w_io=w_io):
                return _mixer_ab(x, g, sc, sh, gm, *w_io, 0, None, wts, None, x.shape[:2])

            def mix_s(x, g, sc, sh, gm, wts=wts, w_io=w_io, pasts_s=pasts_s):
                return _mixer_ab(x, g, sc, sh, gm, *w_io, past_len, pasts_s, wts, page_table, (Bs, Ts))

            (x_p, x_s), (st_p, st_s) = _layer(groups, (mix_p, mix_s), norm_mix[layer], norm_ffn[layer], moe_w)
            ab_p.append(st_p)
            ab_s.append(st_s)
        else:
            flat_kv = lambda a: a.reshape(a.shape[:2] + (2 * DIL_KV_W,))
            pasts_s = (flat_kv(state_dil0_kv[i]), flat_kv(state_dil1_kv[i]), flat_kv(state_dil2_kv[i]))

            w_io = (w_in_c[i], w_out_c[i].astype(BF16))

            def mix_p(x, g, sc, sh, gm, gain=dil_qk_gain[i], w_io=w_io):
                return _mixer_c(x, g, sc, sh, gm, *w_io, 0, None, gain, x.shape[:2])

            def mix_s(x, g, sc, sh, gm, pasts_s=pasts_s, gain=dil_qk_gain[i], w_io=w_io):
                return _mixer_c(x, g, sc, sh, gm, *w_io, past_len, pasts_s, gain, (Bs, Ts))

            (x_p, x_s), (st_p, st_s) = _layer(groups, (mix_p, mix_s), norm_mix[layer], norm_ffn[layer], moe_w)
            dil_p.append(st_p)
            dil_s.append(st_s)
    x_s = x_s.reshape(Bs, Ts, D)
    cmp_p, cmp_s = _stack(ab_p, 0), _stack(ab_s, 0)
    slc_p, slc_s = _stack(ab_p, 1), _stack(ab_s, 1)
    win_p, win_s = _stack(ab_p, 2), _stack(ab_s, 2)
    gmv_p, gmv_s = _stack(ab_p, 3), _stack(ab_s, 3)
    d0_p, d0_s = _stack(dil_p, 0), _stack(dil_s, 0)
    d1_p, d1_s = _stack(dil_p, 1), _stack(dil_s, 1)
    d2_p, d2_s = _stack(dil_p, 2), _stack(dil_s, 2)
    return (x_p, x_s, cmp_p, cmp_s, slc_p, slc_s, win_p, win_s, gmv_p, gmv_s, d0_p, d0_s, d1_p, d1_s, d2_p, d2_s)
```

```python
import functools
import math

import jax
import jax.numpy as jnp
from jax import lax
from jax.experimental import pallas as pl
from jax.experimental.pallas import tpu as pltpu

F32 = jnp.float32
BF16 = jnp.bfloat16

D_MODEL = 1024
HEAD_DIM = 64
ROT_DIM = HEAD_DIM // 4
ROPE_THETA = 500000.0
NORM_EPS = 1e-6
NEG_INF = -1e30
Q_BLOCK = 128
PAGE_SIZE = 128

NSA_HEADS = D_MODEL // (2 * HEAD_DIM)
NSA_KV_HEADS = 2
NSA_GROUP = NSA_HEADS // NSA_KV_HEADS
NSA_BLOCK = 64
NSA_N_SEL = 16
NSA_WINDOW = 512
NSA_Q_W = NSA_HEADS * HEAD_DIM
NSA_KV_W = 2 * NSA_KV_HEADS * HEAD_DIM

GM_GROUPS = 8
GM_WIDTH = D_MODEL // 2
GM_GROUP_W = GM_WIDTH // GM_GROUPS
GM_CHUNK = 128

AB_SPLITS = (NSA_Q_W, 3 * NSA_HEADS, NSA_KV_W, NSA_KV_W, NSA_KV_W, GM_WIDTH, GM_WIDTH)

DIL_CFG = ((128, 1), (512, 4), (2048, 16))
DIL_HEADS = 8
DIL_KV_HEADS = 2
DIL_Q_W = DIL_HEADS * HEAD_DIM
DIL_KV_W = DIL_KV_HEADS * HEAD_DIM

N_EXPERTS = 32
TOP_K = 4
D_FF = D_MODEL
SWIGLU_ALPHA = 1.702
SWIGLU_LIMIT = 7.0

VMEM_LIMIT_BYTES = 56 * 1024 * 1024
ROW_TILE = 512
MOE_TILE = 512


def _row_tile(t):
    return t if t <= ROW_TILE else ROW_TILE


def _mod_spec(mod, tm):
    if mod.shape[1] == 1:
        return pl.BlockSpec((1, 1, mod.shape[2]), lambda b, i: (b, 0, 0))
    return pl.BlockSpec((1, tm, mod.shape[2]), lambda b, i: (b, i, 0))


def _split_bf16(v):
    hi = v.astype(BF16)
    lo = (v - hi.astype(F32)).astype(BF16)
    return hi, lo


def _mod_norm(x, g, sc, sh):
    h = x * lax.rsqrt(jnp.mean(x * x, axis=-1, keepdims=True) + NORM_EPS) * g
    return h * (1.0 + sc) + sh


def _proj_residual_kernel(*refs, n_parts):
    a_refs = refs[:n_parts]
    w_refs = refs[n_parts:2 * n_parts]
    x_ref, g_ref, o_ref = refs[2 * n_parts:]
    y = None
    for a_ref, w_ref in zip(a_refs, w_refs):
        p = jnp.dot(a_ref[0].astype(BF16), w_ref[...], preferred_element_type=F32)
        y = p if y is None else y + p
    o_ref[0] = x_ref[0] + g_ref[0] * y


def _proj_residual(a_parts, w_parts_bf16, x, gate):
    B, T, D = x.shape
    tm = _row_tile(T)
    n_parts = len(a_parts)
    in_specs = [pl.BlockSpec((1, tm, a.shape[2]), lambda b, i: (b, i, 0)) for a in a_parts]
    in_specs += [pl.BlockSpec(w.shape, lambda b, i: (0, 0)) for w in w_parts_bf16]
    in_specs += [pl.BlockSpec((1, tm, D), lambda b, i: (b, i, 0)), _mod_spec(gate, tm)]
    return pl.pallas_call(
        functools.partial(_proj_residual_kernel, n_parts=n_parts),
        out_shape=jax.ShapeDtypeStruct((B, T, D), F32),
        grid=(B, T // tm),
        in_specs=in_specs,
        out_specs=pl.BlockSpec((1, tm, D), lambda b, i: (b, i, 0)),
        compiler_params=pltpu.CompilerParams(
            dimension_semantics=("parallel", "parallel"), vmem_limit_bytes=VMEM_LIMIT_BYTES),
        name="proj_residual",
    )(*a_parts, *w_parts_bf16, x, gate)


ROUTE_RANK_LANE = TOP_K


def _mod_norm_router_kernel(x_ref, g_ref, sc_ref, sh_ref, whi_ref, wlo_ref, b_ref, cnt_in_ref,
                            h_ref, ri_ref, rg_ref, cnt_ref, *, tm):
    @pl.when((pl.program_id(0) == 0) & (pl.program_id(1) == 0))
    def _():
        cnt_ref[...] = cnt_in_ref[...]

    h = _mod_norm(x_ref[0], g_ref[...], sc_ref[0], sh_ref[0])
    h_ref[0] = h
    h_hi, h_lo = _split_bf16(h)
    w_hi = whi_ref[...]
    logits = (jnp.dot(h_hi, w_hi, preferred_element_type=F32)
              + jnp.dot(h_lo, w_hi, preferred_element_type=F32)
              + jnp.dot(h_hi, wlo_ref[...], preferred_element_type=F32)) + b_ref[...]
    n_exp = logits.shape[-1]
    lane = lax.broadcasted_iota(jnp.int32, (1, n_exp), 1)
    work = logits
    vals, ids, hits = [], [], []
    for _ in range(TOP_K):
        m = jnp.max(work, axis=-1, keepdims=True)
        idx = jnp.min(jnp.where(work == m, lane, n_exp), axis=-1, keepdims=True)
        hit = lane == idx
        vals.append(m)
        ids.append(idx)
        hits.append(hit)
        work = jnp.where(hit, NEG_INF, work)
    exps = [jnp.exp(v - vals[0]) for v in vals]
    denom = exps[0] + exps[1] + exps[2] + exps[3]
    onehot = jnp.zeros(logits.shape, F32)
    for hit in hits:
        onehot = onehot + jnp.where(hit, 1.0, 0.0)
    row = lax.broadcasted_iota(jnp.int32, (tm, tm), 0)
    col = lax.broadcasted_iota(jnp.int32, (tm, tm), 1)
    earlier = jnp.where(row > col, 1.0, 0.0).astype(BF16)
    before = jnp.dot(earlier, onehot.astype(BF16), preferred_element_type=F32) + cnt_ref[...]
    cnt_ref[...] = cnt_ref[...] + jnp.sum(onehot, axis=0, keepdims=True)
    out_lane = lax.broadcasted_iota(jnp.int32, (1, LANES), 1)
    ri = jnp.zeros((tm, LANES), jnp.int32)
    rg = jnp.zeros((tm, LANES), F32)
    for k in range(TOP_K):
        rank = jnp.sum(jnp.where(hits[k], before, 0.0), axis=-1, keepdims=True).astype(jnp.int32)
        ri = jnp.where(out_lane == k, ids[k], ri)
        ri = jnp.where(out_lane == ROUTE_RANK_LANE + k, rank, ri)
        rg = jnp.where(out_lane == k, exps[k] / denom, rg)
    ri_ref[0] = ri
    rg_ref[0] = rg


def _mod_norm_router(x, norm_g, scale, shift, w_router, b_router, counts_in):
    B, T, D = x.shape
    tm = _row_tile(T)
    E = w_router.shape[1]
    w_hi, w_lo = _split_bf16(w_router)
    tok = lambda n: pl.BlockSpec((1, tm, n), lambda b, i: (b, i, 0))
    return pl.pallas_call(
        functools.partial(_mod_norm_router_kernel, tm=tm),
        out_shape=[jax.ShapeDtypeStruct((B, T, D), F32), jax.ShapeDtypeStruct((B, T, LANES), jnp.int32),
                   jax.ShapeDtypeStruct((B, T, LANES), F32), jax.ShapeDtypeStruct((1, E), F32)],
        grid=(B, T // tm),
        in_specs=[
            tok(D),
            pl.BlockSpec((1, D), lambda b, i: (0, 0)),
            _mod_spec(scale, tm),
            _mod_spec(shift, tm),
            pl.BlockSpec((D, E), lambda b, i: (0, 0)),
            pl.BlockSpec((D, E), lambda b, i: (0, 0)),
            pl.BlockSpec((1, E), lambda b, i: (0, 0)),
            pl.BlockSpec((1, E), lambda b, i: (0, 0)),
        ],
        out_specs=[tok(D), tok(LANES), tok(LANES), pl.BlockSpec((1, E), lambda b, i: (0, 0))],
        compiler_params=pltpu.CompilerParams(
            dimension_semantics=("arbitrary", "arbitrary"), vmem_limit_bytes=VMEM_LIMIT_BYTES),
        name="mod_norm_router",
    )(x, norm_g.reshape(1, D), scale, shift, w_hi, w_lo, b_router.reshape(1, E), counts_in)


def _moe_ffn_kernel(te_ref, tx_ref, tv_ref, x_ref, wu_ref, bu_ref, wd_ref, bd_ref, o_ref):
    i = pl.program_id(0)

    @pl.when(tv_ref[i] != 0)
    def _():
        hu = jnp.dot(x_ref[...].astype(BF16), wu_ref[0, 0].astype(BF16), preferred_element_type=F32) + bu_ref[0, 0]
        glu = jnp.minimum(hu[:, :D_FF], SWIGLU_LIMIT)
        lin = jnp.clip(hu[:, D_FF:], -SWIGLU_LIMIT, SWIGLU_LIMIT)
        act = glu * jax.nn.sigmoid(SWIGLU_ALPHA * glu) * (lin + 1.0)
        o_ref[...] = (jnp.dot(act.astype(BF16), wd_ref[0, 0].astype(BF16), preferred_element_type=F32)
                      + bd_ref[0, 0])

    @pl.when(tv_ref[i] == 0)
    def _():
        o_ref[...] = jnp.zeros_like(o_ref)


def _moe_ffn(tile_exp, tile_x, tile_valid, x_sorted, layer, w_up, b_up, w_down, b_down):
    n_slots, D = x_sorted.shape
    n_tiles = n_slots // MOE_TILE
    depth, E, _, F2 = w_up.shape
    expert = lambda i, te, tx, tv: (layer, te[i], 0, 0)
    grid_spec = pltpu.PrefetchScalarGridSpec(
        num_scalar_prefetch=3,
        grid=(n_tiles,),
        in_specs=[
            pl.BlockSpec((MOE_TILE, D), lambda i, te, tx, tv: (tx[i], 0)),
            pl.BlockSpec((1, 1, D, F2), expert),
            pl.BlockSpec((1, 1, 1, F2), expert),
            pl.BlockSpec((1, 1, F2 // 2, D), expert),
            pl.BlockSpec((1, 1, 1, D), expert),
        ],
        out_specs=pl.BlockSpec((MOE_TILE, D), lambda i, te, tx, tv: (i, 0)),
    )
    return pl.pallas_call(
        _moe_ffn_kernel,
        out_shape=jax.ShapeDtypeStruct((n_slots, D), F32),
        grid_spec=grid_spec,
        compiler_params=pltpu.CompilerParams(
            dimension_semantics=("arbitrary",), vmem_limit_bytes=VMEM_LIMIT_BYTES),
        name="moe_ffn",
    )(tile_exp, tile_x, tile_valid, x_sorted, w_up, b_up.reshape(depth, E, 1, F2), w_down,
      b_down.reshape(depth, E, 1, D))


DISPATCH_TILE = 512
COMBINE_TILE = 256


def _row_copy(src_ref, src_row, dst_ref, dst_row, sem):
    return pltpu.make_async_copy(src_ref.at[pl.ds(src_row, 1)], dst_ref.at[pl.ds(dst_row, 1)], sem)


def _moe_dispatch_kernel(dest_ref, h_ref, xs_in_ref, xs_ref, sem, *, tm):
    del xs_in_ref
    h_rows = h_ref.at[0]

    def issue(t, carry):
        for k in range(TOP_K):
            _row_copy(h_rows, t, xs_ref, dest_ref[0, 0, 0, t * TOP_K + k], sem).start()
        return carry

    lax.fori_loop(0, tm, issue, 0, unroll=4)
    for k in range(TOP_K):
        pltpu.make_async_copy(h_rows, xs_ref.at[pl.ds(0, tm)], sem).wait()


def _dest_spec(tm):
    return pl.BlockSpec((1, 1, 1, tm * TOP_K), lambda b, i: (b, i, 0, 0), memory_space=pltpu.SMEM)


def _moe_dispatch(dest, h, x_sorted):
    B, T, D = h.shape
    tm = min(T, DISPATCH_TILE)
    return pl.pallas_call(
        functools.partial(_moe_dispatch_kernel, tm=tm),
        out_shape=jax.ShapeDtypeStruct(x_sorted.shape, x_sorted.dtype),
        grid=(B, T // tm),
        in_specs=[_dest_spec(tm),
                  pl.BlockSpec((1, tm, D), lambda b, i: (b, i, 0)),
                  pl.BlockSpec(memory_space=pl.ANY)],
        out_specs=pl.BlockSpec(memory_space=pl.ANY),
        scratch_shapes=[pltpu.SemaphoreType.DMA(())],
        input_output_aliases={2: 0},
        compiler_params=pltpu.CompilerParams(dimension_semantics=("arbitrary", "arbitrary")),
        name="moe_dispatch",
    )(dest.reshape(B, T // tm, 1, tm * TOP_K), h, x_sorted)


def _moe_combine_kernel(dest_ref, ys_ref, rg_ref, x_ref, gf_ref, o_ref, buf, sem, *, tm):
    def issue(t, carry):
        for k in range(TOP_K):
            _row_copy(ys_ref, dest_ref[0, 0, 0, t * TOP_K + k], buf.at[k], t, sem).start()
        return carry

    lax.fori_loop(0, tm, issue, 0, unroll=4)
    for k in range(TOP_K):
        pltpu.make_async_copy(ys_ref.at[pl.ds(0, tm)], buf.at[k], sem).wait()
    gates = rg_ref[0]
    y = gates[:, 0:1] * buf[0]
    for k in range(1, TOP_K):
        y = y + gates[:, k:k + 1] * buf[k]
    o_ref[0] = x_ref[0] + gf_ref[0] * y


def _moe_combine(dest, y_sorted, route_g, x, gate_f):
    B, T, D = x.shape
    tm = min(T, COMBINE_TILE)
    tok = lambda n: pl.BlockSpec((1, tm, n), lambda b, i: (b, i, 0))
    return pl.pallas_call(
        functools.partial(_moe_combine_kernel, tm=tm),
        out_shape=jax.ShapeDtypeStruct((B, T, D), F32),
        grid=(B, T // tm),
        in_specs=[_dest_spec(tm), pl.BlockSpec(memory_space=pl.ANY), tok(LANES), tok(D), _mod_spec(gate_f, tm)],
        out_specs=tok(D),
        scratch_shapes=[pltpu.VMEM((TOP_K, tm, D), F32), pltpu.SemaphoreType.DMA(())],
        compiler_params=pltpu.CompilerParams(
            dimension_semantics=("parallel", "parallel"), vmem_limit_bytes=VMEM_LIMIT_BYTES),
        name="moe_combine",
    )(dest.reshape(B, T // tm, 1, tm * TOP_K), y_sorted, route_g, x, gate_f)


def _moe_plan(counts):
    counts = counts.astype(jnp.int32)
    padded = (counts + MOE_TILE - 1) // MOE_TILE * MOE_TILE
    pad_end = jnp.cumsum(padded)
    pad_start = pad_end - padded
    n_used = pad_end[-1] // MOE_TILE
    return pad_start, pad_end, n_used


def _moe_tiles(pad_end, n_used, n_tiles):
    tile = jnp.arange(n_tiles, dtype=jnp.int32)
    tile_valid = (tile < n_used).astype(jnp.int32)
    tile_x = jnp.minimum(tile, n_used - 1)
    tile_exp = jnp.sum((tile_x[:, None] * MOE_TILE >= pad_end[None, :]).astype(jnp.int32), axis=1)
    return jnp.minimum(tile_exp, N_EXPERTS - 1), tile_x, tile_valid


N_QHEADS = 8
LANES = 128
HALF = HEAD_DIM


def _nt_dot(a, b):
    return lax.dot_general(a, b, (((1,), (1,)), ((), ())), preferred_element_type=F32)


def _band_mask(qpos, kpos, window, dilation):
    delta = qpos - kpos
    valid = (delta >= 0) & (delta <= window)
    if dilation > 1:
        valid = valid & ((delta & (dilation - 1)) == 0)
    return valid


def _softmax_pv(s, valid, v, tq):
    n = s.shape[-1]
    s = jnp.where(valid[None], s.reshape(N_QHEADS, tq, n), NEG_INF)
    m = jnp.max(s, axis=-1, keepdims=True)
    p = jnp.exp(s - m)
    l = jnp.sum(p, axis=-1, keepdims=True)
    o = jnp.dot(p.reshape(N_QHEADS * tq, n).astype(BF16), v, preferred_element_type=F32)
    return o.reshape(N_QHEADS, tq, LANES) / l, m + jnp.log(l)


def _store_heads(o_ref, o, lse):
    if lse is None:
        o_ref[0] = o
        return
    lane = lax.broadcasted_iota(jnp.int32, (1, LANES), 1)
    for h in range(N_QHEADS):
        own = (lane < HALF) if h < N_QHEADS // 2 else (lane >= HALF)
        o_ref[0, h] = jnp.where(own, o[h], lse[h])


def _band_self_kernel(q_ref, k_ref, v_ref, o_ref, *, window, tq, span, want_lse):
    T = k_ref.shape[1]
    q0 = pl.program_id(1) * tq
    if span == T:
        start = 0
        k = k_ref[0]
        v = v_ref[0]
    else:
        start = pl.multiple_of(jnp.maximum(q0 - window, 0), LANES)
        k = k_ref[0, pl.ds(start, span), :]
        v = v_ref[0, pl.ds(start, span), :]
    q = q_ref[0].reshape(N_QHEADS * tq, LANES).astype(BF16)
    qpos = q0 + lax.broadcasted_iota(jnp.int32, (tq, 1), 0)
    kpos = start + lax.broadcasted_iota(jnp.int32, (1, span), 1)
    o, lse = _softmax_pv(_nt_dot(q, k), _band_mask(qpos, kpos, window, 1), v, tq)
    _store_heads(o_ref, o, lse if want_lse else None)


def _band_self(q, k, v, window, want_lse):
    B, _, T, _ = q.shape
    tq = min(T, Q_BLOCK)
    span = min(T, window + tq)
    return pl.pallas_call(
        functools.partial(_band_self_kernel, window=window, tq=tq, span=span, want_lse=want_lse),
        out_shape=jax.ShapeDtypeStruct((B, N_QHEADS, T, LANES), F32),
        grid=(B, T // tq),
        in_specs=[
            pl.BlockSpec((1, N_QHEADS, tq, LANES), lambda b, i: (b, 0, i, 0)),
            pl.BlockSpec((1, T, LANES), lambda b, i: (b, 0, 0)),
            pl.BlockSpec((1, T, LANES), lambda b, i: (b, 0, 0)),
        ],
        out_specs=pl.BlockSpec((1, N_QHEADS, tq, LANES), lambda b, i: (b, 0, i, 0)),
        compiler_params=pltpu.CompilerParams(
            dimension_semantics=("parallel", "parallel"), vmem_limit_bytes=VMEM_LIMIT_BYTES),
        name="band_self",
    )(q, k, v)


def _band_tail_kernel(q_ref, kv_ref, o_ref, *, window, dilation, tq, want_lse):
    L = kv_ref.shape[1]
    kv = kv_ref[0]
    k = kv[:, :LANES].astype(BF16)
    v = kv[:, LANES:].astype(BF16)
    q = q_ref[0].reshape(N_QHEADS * tq, LANES).astype(BF16)
    qpos = (L - tq) + lax.broadcasted_iota(jnp.int32, (tq, 1), 0)
    kpos = lax.broadcasted_iota(jnp.int32, (1, L), 1)
    o, lse = _softmax_pv(_nt_dot(q, k), _band_mask(qpos, kpos, window, dilation), v, tq)
    _store_heads(o_ref, o, lse if want_lse else None)


def _band_tail(q, kv_all, window, dilation, want_lse):
    B, _, T, _ = q.shape
    L = kv_all.shape[1]
    return pl.pallas_call(
        functools.partial(_band_tail_kernel, window=window, dilation=dilation, tq=T, want_lse=want_lse),
        out_shape=jax.ShapeDtypeStruct((B, N_QHEADS, T, LANES), F32),
        grid=(B,),
        in_specs=[
            pl.BlockSpec((1, N_QHEADS, T, LANES), lambda b: (b, 0, 0, 0)),
            pl.BlockSpec((1, L, 2 * LANES), lambda b: (b, 0, 0)),
        ],
        out_specs=pl.BlockSpec((1, N_QHEADS, T, LANES), lambda b: (b, 0, 0, 0)),
        compiler_params=pltpu.CompilerParams(
            dimension_semantics=("parallel",), vmem_limit_bytes=VMEM_LIMIT_BYTES),
        name="band_tail",
    )(q, kv_all)


def _block_mean_kernel(x_ref, o_ref):
    rows = x_ref.shape[1]
    x = x_ref[0].reshape(rows // NSA_BLOCK, NSA_BLOCK, x_ref.shape[2])
    o_ref[0] = jnp.sum(x, axis=1) * (1.0 / NSA_BLOCK)


def _block_mean(kvc):
    B, T, W = kvc.shape
    tm = _row_tile(T)
    return pl.pallas_call(
        _block_mean_kernel,
        out_shape=jax.ShapeDtypeStruct((B, T // NSA_BLOCK, W), F32),
        grid=(B, T // tm),
        in_specs=[pl.BlockSpec((1, tm, W), lambda b, i: (b, i, 0))],
        out_specs=pl.BlockSpec((1, tm // NSA_BLOCK, W), lambda b, i: (b, i, 0)),
        compiler_params=pltpu.CompilerParams(dimension_semantics=("parallel", "parallel")),
        name="block_mean",
    )(kvc)


PAGES_PER_STEP = 8


def _feature_major_pages(cache):
    n_phys, rows = cache.shape[:2]
    return cache.transpose(0, 2, 3, 4, 1).reshape(n_phys, -1, rows)


def _page_specs(n_feat):
    return [pl.BlockSpec((1, n_feat, PAGE_SIZE), functools.partial(
        lambda b, s, pt, pg: (pt[b, s * PAGES_PER_STEP + pg], 0, 0), pg=pg)) for pg in range(PAGES_PER_STEP)]


def _page_block_mean_kernel(pt_ref, *refs):
    page_refs, o_ref = refs[:-1], refs[-1]
    step = pl.program_id(1)
    n_blocks = o_ref.shape[2]
    per_page = PAGE_SIZE // NSA_BLOCK

    @pl.when(step == 0)
    def _():
        o_ref[...] = jnp.zeros(o_ref.shape, F32)

    row_blk = lax.broadcasted_iota(jnp.int32, (PAGE_SIZE, 1), 0) // NSA_BLOCK
    col = lax.broadcasted_iota(jnp.int32, (1, n_blocks), 1)
    acc = o_ref[0]
    for pg, page_ref in enumerate(page_refs):
        first = (step * PAGES_PER_STEP + pg) * per_page
        avg = jnp.where(col == first + row_blk, 1.0 / NSA_BLOCK, 0.0).astype(BF16)
        hi, lo = _split_bf16(page_ref[0])
        acc = acc + jnp.dot(hi, avg, preferred_element_type=F32) + jnp.dot(lo, avg, preferred_element_type=F32)
    o_ref[0] = acc


def _page_block_mean(pages, page_table):
    B, n_pages = page_table.shape
    F = pages.shape[1]
    n_blocks = n_pages * PAGE_SIZE // NSA_BLOCK
    grid_spec = pltpu.PrefetchScalarGridSpec(
        num_scalar_prefetch=1,
        grid=(B, n_pages // PAGES_PER_STEP),
        in_specs=_page_specs(F),
        out_specs=pl.BlockSpec((1, F, n_blocks), lambda b, s, pt: (b, 0, 0)),
    )
    return pl.pallas_call(
        _page_block_mean_kernel,
        out_shape=jax.ShapeDtypeStruct((B, F, n_blocks), F32),
        grid_spec=grid_spec,
        compiler_params=pltpu.CompilerParams(dimension_semantics=("parallel", "arbitrary")),
        name="page_block_mean",
    )(page_table, *([pages] * PAGES_PER_STEP))


def _head_mean_sq(x, ones_bd):
    hi, lo = _split_bf16(x * x)
    return jnp.dot(hi, ones_bd, preferred_element_type=F32) + jnp.dot(lo, ones_bd, preferred_element_type=F32)


def _cmp_finish_kernel(mean_ref, pe_ref, wk_ref, wv_ref, gain_ref, bd_ref, kc_ref, vc_ref):
    s = mean_ref[0] + pe_ref[...]
    sk = jnp.dot(s[:, :LANES].astype(BF16), wk_ref[...], preferred_element_type=F32)
    vc_ref[0] = jnp.dot(s[:, LANES:].astype(BF16), wv_ref[...], preferred_element_type=F32)
    kc_ref[0] = sk * lax.rsqrt(_head_mean_sq(sk, bd_ref[...]) + NORM_EPS) * gain_ref[...]


def _block_diag2(w):
    z = jnp.zeros_like(w)
    return jnp.concatenate([jnp.concatenate([w, z], axis=1), jnp.concatenate([z, w], axis=1)], axis=0)


def _cmp_finish(mean, pe, w_phi, kc_gain):
    B, n_cb, W = mean.shape
    pe_mean = jnp.mean(pe, axis=0)
    pe_row = jnp.concatenate([pe_mean[0], pe_mean[0], pe_mean[1], pe_mean[1]])[None, :]
    wk = _block_diag2(w_phi[0]).astype(BF16)
    wv = _block_diag2(w_phi[1]).astype(BF16)
    bd = _block_diag2(jnp.full((HALF, HALF), 1.0 / HALF, F32)).astype(BF16)
    gain = jnp.concatenate([kc_gain, kc_gain])[None, :]
    full = lambda shape: pl.BlockSpec(shape, lambda b: (0,) * len(shape))
    return pl.pallas_call(
        _cmp_finish_kernel,
        out_shape=[jax.ShapeDtypeStruct((B, n_cb, LANES), F32)] * 2,
        grid=(B,),
        in_specs=[pl.BlockSpec((1, n_cb, W), lambda b: (b, 0, 0)), full((1, W)), full((LANES, LANES)),
                  full((LANES, LANES)), full((1, LANES)), full((LANES, LANES))],
        out_specs=[pl.BlockSpec((1, n_cb, LANES), lambda b: (b, 0, 0))] * 2,
        compiler_params=pltpu.CompilerParams(dimension_semantics=("parallel",)),
        name="cmp_finish",
    )(mean, pe_row, wk, wv, gain, bd)


def _cmp_select_kernel(q_ref, kc_ref, vc_ref, ocmp_ref, sel_ref, *, tq, pos0):
    n_cb = kc_ref.shape[1]
    q0 = pl.program_id(1) * tq
    qh, ql = _split_bf16(q_ref[0].reshape(N_QHEADS * tq, LANES))
    kh, kl = _split_bf16(kc_ref[0])
    s = (_nt_dot(qh, kh) + _nt_dot(ql, kh) + _nt_dot(qh, kl)) * (HEAD_DIM ** -0.5)
    cur = (pos0 + q0 + lax.broadcasted_iota(jnp.int32, (tq, 1), 0)) // NSA_BLOCK
    blk = lax.broadcasted_iota(jnp.int32, (1, n_cb), 1)
    vis = blk < cur
    s = jnp.where(vis[None], s.reshape(N_QHEADS, tq, n_cb), NEG_INF)
    p = jnp.exp(s - jnp.max(s, axis=-1, keepdims=True))
    p = jnp.where(vis[None], p / jnp.sum(p, axis=-1, keepdims=True), 0.0)
    o = jnp.dot(p.reshape(N_QHEADS * tq, n_cb).astype(BF16), vc_ref[0].astype(BF16), preferred_element_type=F32)
    ocmp_ref[0] = o.reshape(N_QHEADS, tq, LANES)
    for kvh in range(NSA_KV_HEADS):
        g0 = kvh * NSA_GROUP
        imp = jnp.where(vis, p[g0] + p[g0 + 1] + p[g0 + 2] + p[g0 + 3], -1.0)
        rank = jnp.zeros((tq, n_cb), jnp.int32)
        for i in range(n_cb):
            col = imp[:, i:i + 1]
            wins_tie = jnp.where(blk > i, 1, 0)
            rank = rank + jnp.where(col > imp, 1, jnp.where(col == imp, wins_tie, 0))
        chosen = jnp.where(vis, jnp.where(rank < NSA_N_SEL - 1, 1.0, 0.0), jnp.where(blk == cur, 1.0, 0.0))
        sel_ref[0, kvh] = chosen.astype(BF16)


def _cmp_select(qn, kc, vc, pos0):
    B, _, T, _ = qn.shape
    n_cb = kc.shape[1]
    tq = min(T, Q_BLOCK)
    return pl.pallas_call(
        functools.partial(_cmp_select_kernel, tq=tq, pos0=pos0),
        out_shape=[jax.ShapeDtypeStruct((B, N_QHEADS, T, LANES), F32),
                   jax.ShapeDtypeStruct((B, NSA_KV_HEADS, T, n_cb), BF16)],
        grid=(B, T // tq),
        in_specs=[
            pl.BlockSpec((1, N_QHEADS, tq, LANES), lambda b, i: (b, 0, i, 0)),
            pl.BlockSpec((1, n_cb, LANES), lambda b, i: (b, 0, 0)),
            pl.BlockSpec((1, n_cb, LANES), lambda b, i: (b, 0, 0)),
        ],
        out_specs=[pl.BlockSpec((1, N_QHEADS, tq, LANES), lambda b, i: (b, 0, i, 0)),
                   pl.BlockSpec((1, NSA_KV_HEADS, tq, n_cb), lambda b, i: (b, 0, i, 0))],
        compiler_params=pltpu.CompilerParams(
            dimension_semantics=("parallel", "parallel"), vmem_limit_bytes=VMEM_LIMIT_BYTES),
        name="cmp_select",
    )(qn, kc, vc)


def _selection_bias_kv(sel, first_key, n_keys, causal):
    n_cb = sel.shape[-1]
    key_blk = (first_key + lax.broadcasted_iota(jnp.int32, (1, n_keys), 1)) // NSA_BLOCK
    expand = jnp.where(lax.broadcasted_iota(jnp.int32, (n_cb, 1), 0) == key_blk, 1.0, 0.0).astype(BF16)
    out = []
    for kvh in range(NSA_KV_HEADS):
        picked = jnp.dot(sel[kvh], expand, preferred_element_type=F32)
        bias = (picked - 1.0) * 1e30
        if causal is not None:
            bias = jnp.where(causal, bias, NEG_INF)
        out.append(bias)
    return out


def _selection_bias(sel, first_key, n_keys, causal):
    return jnp.concatenate([jnp.broadcast_to(b[None], (NSA_GROUP,) + b.shape)
                            for b in _selection_bias_kv(sel, first_key, n_keys, causal)], axis=0)


def _online_softmax_step(s, v, m_ref, l_ref, acc_ref, tq, v_feature_major=False):
    n = s.shape[-1]
    m_old = m_ref[...]
    m_new = jnp.maximum(m_old, jnp.max(s, axis=-1, keepdims=True))
    alpha = jnp.exp(m_old - m_new)
    p = jnp.exp(s - m_new[..., :1])
    l_ref[...] = alpha * l_ref[...] + jnp.sum(p, axis=-1, keepdims=True)
    p2 = p.reshape(N_QHEADS * tq, n).astype(BF16)
    pv = _nt_dot(p2, v) if v_feature_major else jnp.dot(p2, v, preferred_element_type=F32)
    acc_ref[...] = alpha * acc_ref[...] + pv.reshape(N_QHEADS, tq, LANES)
    m_ref[...] = m_new


def _init_softmax_state(m_ref, l_ref, acc_ref):
    m_ref[...] = jnp.full(m_ref.shape, NEG_INF, F32)
    l_ref[...] = jnp.zeros(l_ref.shape, F32)
    acc_ref[...] = jnp.zeros(acc_ref.shape, F32)


def _gated_heads(gate_pre, o_cmp, o_slc, o_win):
    g = jax.nn.sigmoid(gate_pre)
    lane = lax.broadcasted_iota(jnp.int32, (1, LANES), 1)
    pairs = []
    for c in range(N_QHEADS // 2):
        mixed = []
        for h in (2 * c, 2 * c + 1):
            mixed.append(g[:, 3 * h:3 * h + 1] * o_cmp[h] + g[:, 3 * h + 1:3 * h + 2] * o_slc[h]
                         + g[:, 3 * h + 2:3 * h + 3] * o_win[h])
        if 2 * c < N_QHEADS // 2:
            pairs.append(jnp.where(lane < HALF, mixed[0], pltpu.roll(mixed[1], HALF, 1)))
        else:
            pairs.append(jnp.where(lane < HALF, pltpu.roll(mixed[0], HALF, 1), mixed[1]))
    return jnp.concatenate(pairs, axis=-1)


SLC_KEY_TILE = 512


def _nsa_prompt_kernel(q_ref, ks_ref, vs_ref, kw_ref, vw_ref, sel_ref, ocmp_ref, g_ref, o_ref,
                       m_ref, l_ref, acc_ref, *, tq, tk, span):
    T = ks_ref.shape[1]
    q0 = pl.program_id(1) * tq
    q = q_ref[0].reshape(N_QHEADS * tq, LANES)
    qpos = q0 + lax.broadcasted_iota(jnp.int32, (tq, 1), 0)
    sel = sel_ref[0]
    _init_softmax_state(m_ref, l_ref, acc_ref)

    def key_tile(j, carry):
        k0 = pl.multiple_of(j * tk, tk)
        kpos = k0 + lax.broadcasted_iota(jnp.int32, (1, tk), 1)
        s = _nt_dot(q, ks_ref[0, pl.ds(k0, tk), :]).reshape(N_QHEADS, tq, tk)
        s = s + _selection_bias(sel, k0, tk, kpos <= qpos)
        _online_softmax_step(s, vs_ref[0, pl.ds(k0, tk), :], m_ref, l_ref, acc_ref, tq)
        return carry

    lax.fori_loop(0, (q0 + tq + tk - 1) // tk, key_tile, 0)
    o_slc = acc_ref[...] / l_ref[...]

    if span == T:
        start = 0
        kw, vw = kw_ref[0], vw_ref[0]
    else:
        start = pl.multiple_of(jnp.maximum(q0 - NSA_WINDOW, 0), LANES)
        kw, vw = kw_ref[0, pl.ds(start, span), :], vw_ref[0, pl.ds(start, span), :]
    kpos = start + lax.broadcasted_iota(jnp.int32, (1, span), 1)
    o_win, _ = _softmax_pv(_nt_dot(q, kw), _band_mask(qpos, kpos, NSA_WINDOW, 1), vw, tq)
    o_ref[0] = _gated_heads(g_ref[0], ocmp_ref[0], o_slc, o_win)


def _nsa_prompt(qr, ks, vs, kw, vw, sel, ocmp, gate_pre):
    B, _, T, _ = qr.shape
    n_cb = sel.shape[-1]
    tq = min(T, Q_BLOCK)
    tk = min(T, SLC_KEY_TILE)
    span = min(T, NSA_WINDOW + tq)
    seq = pl.BlockSpec((1, T, LANES), lambda b, i: (b, 0, 0))
    heads = pl.BlockSpec((1, N_QHEADS, tq, LANES), lambda b, i: (b, 0, i, 0))
    stat = pltpu.VMEM((N_QHEADS, tq, LANES), F32)
    return pl.pallas_call(
        functools.partial(_nsa_prompt_kernel, tq=tq, tk=tk, span=span),
        out_shape=jax.ShapeDtypeStruct((B, T, NSA_Q_W), F32),
        grid=(B, T // tq),
        in_specs=[heads, seq, seq, seq, seq,
                  pl.BlockSpec((1, NSA_KV_HEADS, tq, n_cb), lambda b, i: (b, 0, i, 0)),
                  heads,
                  pl.BlockSpec((1, tq, gate_pre.shape[2]), lambda b, i: (b, i, 0))],
        out_specs=pl.BlockSpec((1, tq, NSA_Q_W), lambda b, i: (b, i, 0)),
        scratch_shapes=[stat, stat, stat],
        compiler_params=pltpu.CompilerParams(
            dimension_semantics=("parallel", "arbitrary"), vmem_limit_bytes=VMEM_LIMIT_BYTES),
        name="nsa_prompt",
    )(qr, ks, vs, kw, vw, sel, ocmp, gate_pre)


def _nsa_paged_kernel(pt_ref, q_ref, *refs, tq, pos0):
    page_refs = refs[:PAGES_PER_STEP]
    new_ref, sel_ref, ocmp_ref, owin_ref, g_ref, o_ref, m_ref, l_ref, acc_ref = refs[PAGES_PER_STEP:]
    step = pl.program_id(1)
    q = q_ref[0].reshape(N_QHEADS * tq, LANES).astype(BF16)
    sel = sel_ref[0]

    @pl.when(step == 0)
    def _():
        _init_softmax_state(m_ref, l_ref, acc_ref)

    def attend(kt, vt, first_key, causal):
        n = kt.shape[1]
        s = jnp.dot(q, kt, preferred_element_type=F32).reshape(N_QHEADS, tq, n)
        if causal is None:
            s = s + _selection_bias(sel, first_key, n, None)
        else:
            s = jnp.where(causal[None], s, NEG_INF)
        _online_softmax_step(s, vt, m_ref, l_ref, acc_ref, tq, v_feature_major=True)

    attend(jnp.concatenate([r[0, :LANES, :].astype(BF16) for r in page_refs], axis=1),
           jnp.concatenate([r[0, LANES:, :].astype(BF16) for r in page_refs], axis=1),
           step * (PAGES_PER_STEP * PAGE_SIZE), None)

    @pl.when(step == pl.num_programs(1) - 1)
    def _():
        qpos = lax.broadcasted_iota(jnp.int32, (tq, 1), 0)
        kpos = lax.broadcasted_iota(jnp.int32, (1, PAGE_SIZE), 1)
        new = new_ref[0]
        attend(new[:LANES, :].astype(BF16), new[LANES:, :].astype(BF16), pos0, kpos <= qpos)
        o_ref[0] = _gated_heads(g_ref[0], ocmp_ref[0], acc_ref[...] / l_ref[...], owin_ref[0])


def _nsa_paged(page_table, qr, cache, new_tile, sel, ocmp, owin, gate_pre, pos0):
    B, _, T, _ = qr.shape
    n_pages = page_table.shape[1]
    n_cb = sel.shape[-1]
    W = cache.shape[1]
    heads = pl.BlockSpec((1, N_QHEADS, T, LANES), lambda b, s, pt: (b, 0, 0, 0))
    stat = pltpu.VMEM((N_QHEADS, T, LANES), F32)
    grid_spec = pltpu.PrefetchScalarGridSpec(
        num_scalar_prefetch=1,
        grid=(B, n_pages // PAGES_PER_STEP),
        in_specs=[heads] + _page_specs(W) + [
            pl.BlockSpec((1, W, PAGE_SIZE), lambda b, s, pt: (b, 0, 0)),
            pl.BlockSpec((1, NSA_KV_HEADS, T, n_cb), lambda b, s, pt: (b, 0, 0, 0)),
            heads, heads,
            pl.BlockSpec((1, T, gate_pre.shape[2]), lambda b, s, pt: (b, 0, 0))],
        out_specs=pl.BlockSpec((1, T, NSA_Q_W), lambda b, s, pt: (b, 0, 0)),
        scratch_shapes=[stat, stat, stat],
    )
    return pl.pallas_call(
        functools.partial(_nsa_paged_kernel, tq=T, pos0=pos0),
        out_shape=jax.ShapeDtypeStruct((B, T, NSA_Q_W), F32),
        grid_spec=grid_spec,
        compiler_params=pltpu.CompilerParams(
            dimension_semantics=("parallel", "arbitrary"), vmem_limit_bytes=VMEM_LIMIT_BYTES),
        name="nsa_paged",
    )(page_table, qr, *([cache] * PAGES_PER_STEP), new_tile, sel, ocmp, owin, gate_pre)


QPAD_W = N_QHEADS * LANES


def _rope_tables(pos):
    half = ROT_DIM // 2
    inv = jnp.exp(-math.log(ROPE_THETA) * jnp.arange(half, dtype=F32) * (2.0 / ROT_DIM))
    ang = pos.astype(F32)[:, None] * inv[None, :]
    cos, sin = jnp.cos(ang), jnp.sin(ang)
    T = pos.shape[0]
    zeros = lambda n: jnp.zeros((T, n), F32)
    cos_t = jnp.concatenate([cos, cos, jnp.ones((T, HALF - ROT_DIM), F32)], axis=-1)
    msin_t = jnp.concatenate([-sin, zeros(HALF - half)], axis=-1)
    psin_t = jnp.concatenate([zeros(half), sin, zeros(HALF - ROT_DIM)], axis=-1)
    twice = lambda a: jnp.concatenate([a, a], axis=-1)
    return twice(cos_t), twice(msin_t), twice(psin_t)


def _rope_lanes(x, cos_t, msin_t, psin_t):
    half = ROT_DIM // 2
    return x * cos_t + pltpu.roll(x, LANES - half, 1) * msin_t + pltpu.roll(x, half, 1) * psin_t


def _head_rms(x, ones_mat, gain):
    return x * lax.rsqrt(_head_mean_sq(x, ones_mat) + NORM_EPS) * gain


def _pad_head_columns(w):
    D = w.shape[0]
    w = w.reshape(D, N_QHEADS, HALF)
    z = jnp.zeros_like(w)
    lower = (jnp.arange(N_QHEADS) < N_QHEADS // 2)[None, :, None]
    return jnp.concatenate([jnp.where(lower, w, z), jnp.where(lower, z, w)], axis=-1).reshape(D, QPAD_W)


def _pad_head_vector(g):
    return _pad_head_columns(jnp.tile(g, N_QHEADS)[None, :])


def _ab_proj_kernel(x_ref, g_ref, sc_ref, sh_ref, w_ref, cos_ref, msin_ref, psin_ref, gq_ref, gks_ref, gkw_ref,
                    ones_ref, bd_ref, lng_ref, lnb_ref, wm_ref, bias_ref,
                    qn_ref, qr_ref, kvc_ref, kvs_ref, ks_ref, vs_ref, kvw_ref, kw_ref, vw_ref, gate_ref,
                    ob_ref, vln_ref, *, tm):
    h = _mod_norm(x_ref[0], g_ref[...], sc_ref[0], sh_ref[0])
    y = jnp.dot(h.astype(BF16), w_ref[...], preferred_element_type=F32)
    tabs = (cos_ref[...], msin_ref[...], psin_ref[...])
    for hd in range(N_QHEADS):
        lanes = slice(hd * LANES, (hd + 1) * LANES)
        qn = _head_rms(y[:, lanes], ones_ref[...], gq_ref[:, lanes])
        qn_ref[0, hd] = qn
        qr_ref[0, hd] = (_rope_lanes(qn, *tabs) * (HEAD_DIM ** -0.5)).astype(BF16)
    off = QPAD_W
    kvc_ref[0] = y[:, off:off + NSA_KV_W]
    off += NSA_KV_W
    for kv_ref, k_ref, v_ref, gain_ref in ((kvs_ref, ks_ref, vs_ref, gks_ref), (kvw_ref, kw_ref, vw_ref, gkw_ref)):
        k = _rope_lanes(_head_rms(y[:, off:off + LANES], bd_ref[...], gain_ref[...]), *tabs)
        v = y[:, off + LANES:off + NSA_KV_W]
        kv_ref[0, :, :LANES] = k
        kv_ref[0, :, LANES:] = v
        k_ref[0] = k.astype(BF16)
        v_ref[0] = v.astype(BF16)
        off += NSA_KV_W
    u = jax.nn.gelu(y[:, off:off + GM_WIDTH])
    v = jax.nn.gelu(y[:, off + GM_WIDTH:off + 2 * GM_WIDTH])
    off += 2 * GM_WIDTH
    gate_ref[0] = y[:, off:]
    mu = jnp.mean(v, axis=-1, keepdims=True)
    var = jnp.mean(jnp.square(v - mu), axis=-1, keepdims=True)
    vln = (v - mu) * lax.rsqrt(var + NORM_EPS) * lng_ref[...] + lnb_ref[...]
    vln_ref[0] = vln
    lane = lax.broadcasted_iota(jnp.int32, (1, LANES), 1)
    for c in range(tm // GM_CHUNK):
        rows = slice(c * GM_CHUNK, (c + 1) * GM_CHUNK)
        mixed = []
        for p in range(GM_GROUPS // 2):
            vp = vln[rows, p * LANES:(p + 1) * LANES]
            lo = jnp.where(lane < HALF, vp, 0.0).astype(BF16)
            hi = jnp.where(lane < HALF, 0.0, vp).astype(BF16)
            mixed.append(jnp.dot(wm_ref[2 * p], lo, preferred_element_type=F32)
                         + jnp.dot(wm_ref[2 * p + 1], hi, preferred_element_type=F32))
        ob_ref[0, rows, :] = u[rows] * (jnp.concatenate(mixed, axis=-1) + bias_ref[...])


def _ab_proj(x, norm_g, scale, shift, w_in, pos, qk_gain, ln_g, ln_b, wm, mix_bias):
    B, T, D = x.shape
    tm = _row_tile(T)
    g0, g1 = AB_SPLITS[0], AB_SPLITS[0] + AB_SPLITS[1]
    w = jnp.concatenate([_pad_head_columns(w_in[:, :g0]), w_in[:, g1:], w_in[:, g0:g1]], axis=1).astype(BF16)
    N = w.shape[1]
    cos_t, msin_t, psin_t = _rope_tables(pos)
    ones_mat = jnp.full((LANES, LANES), 1.0 / HALF, BF16)
    bd = _block_diag2(jnp.full((HALF, HALF), 1.0 / HALF, F32)).astype(BF16)
    two = lambda g: jnp.concatenate([g, g])[None, :]
    const = lambda a: pl.BlockSpec(a.shape, lambda b, i: (0,) * a.ndim)
    tok = lambda n: pl.BlockSpec((1, tm, n), lambda b, i: (b, i, 0))
    heads = pl.BlockSpec((1, N_QHEADS, tm, LANES), lambda b, i: (b, 0, i, 0))
    table = pl.BlockSpec((tm, LANES), lambda b, i: (i, 0))
    consts = [_pad_head_vector(qk_gain[0]), two(qk_gain[2]), two(qk_gain[3]), ones_mat, bd,
              ln_g[None, :], ln_b[None, :], wm, mix_bias]
    sds = lambda shape, dt: jax.ShapeDtypeStruct(shape, dt)
    return pl.pallas_call(
        functools.partial(_ab_proj_kernel, tm=tm),
        out_shape=[sds((B, N_QHEADS, T, LANES), F32), sds((B, N_QHEADS, T, LANES), BF16),
                   sds((B, T, NSA_KV_W), F32),
                   sds((B, T, NSA_KV_W), F32), sds((B, T, LANES), BF16), sds((B, T, LANES), BF16),
                   sds((B, T, NSA_KV_W), F32), sds((B, T, LANES), BF16), sds((B, T, LANES), BF16),
                   sds((B, T, 3 * NSA_HEADS), F32), sds((B, T, GM_WIDTH), F32), sds((B, T, GM_WIDTH), F32)],
        grid=(B, T // tm),
        in_specs=[tok(D), pl.BlockSpec((1, D), lambda b, i: (0, 0)), _mod_spec(scale, tm), _mod_spec(shift, tm),
                  pl.BlockSpec((D, N), lambda b, i: (0, 0)), table, table, table] + [const(a) for a in consts],
        out_specs=[heads, heads, tok(NSA_KV_W), tok(NSA_KV_W), tok(LANES), tok(LANES), tok(NSA_KV_W), tok(LANES),
                   tok(LANES), tok(3 * NSA_HEADS), tok(GM_WIDTH), tok(GM_WIDTH)],
        compiler_params=pltpu.CompilerParams(
            dimension_semantics=("parallel", "parallel"), vmem_limit_bytes=VMEM_LIMIT_BYTES),
        name="ab_proj",
    )(x, norm_g.reshape(1, D), scale, shift, w, cos_t, msin_t, psin_t, *consts)


def _chunk_mix_weights(ws, bs, rows_per_seq):
    n = min(rows_per_seq, GM_CHUNK)
    wm = jnp.where(jnp.tril(jnp.ones((n, n), bool)), ws[:, :n, :n], 0.0)
    if n < GM_CHUNK:
        eye = jnp.eye(GM_CHUNK // n, dtype=F32)
        wm = jnp.einsum('ab,gij->gaibj', eye, wm).reshape(GM_GROUPS, GM_CHUNK, GM_CHUNK)
    bias = jnp.tile(jnp.transpose(bs[:, :n]), (GM_CHUNK // n, 1))
    return wm.astype(BF16), jnp.repeat(bias, GM_GROUP_W, axis=1)


C_GROUP_W = QPAD_W + 2 * DIL_KV_W


def _c_proj_kernel(x_ref, g_ref, sc_ref, sh_ref, w_ref, cos_ref, msin_ref, psin_ref, gq_ref, gk_ref,
                   ones_ref, bd_ref, *refs, tm, dils):
    out_refs, stage = refs[:-1], refs[-1]
    h = _mod_norm(x_ref[0], g_ref[...], sc_ref[0], sh_ref[0])
    y = jnp.dot(h.astype(BF16), w_ref[...], preferred_element_type=F32)
    tabs = (cos_ref[...], msin_ref[...], psin_ref[...])

    def put(dst, val, d):
        if d == 1:
            dst[0] = val.astype(dst.dtype)
            return
        stage[...] = val
        for r in range(d):
            dst[r] = stage[pl.ds(r, tm // d, stride=d), :].astype(dst.dtype)

    for g, d in enumerate(dils):
        qd_ref, kd_ref, vd_ref, kv_ref = out_refs[4 * g:4 * g + 4]
        off = g * C_GROUP_W
        for hd in range(N_QHEADS):
            lanes = slice(off + hd * LANES, off + (hd + 1) * LANES)
            q = _rope_lanes(_head_rms(y[:, lanes], ones_ref[...], gq_ref[:, hd * LANES:(hd + 1) * LANES]), *tabs)
            put(qd_ref.at[:, hd], q * (HEAD_DIM ** -0.5), d)
        k = _rope_lanes(_head_rms(y[:, off + QPAD_W:off + QPAD_W + LANES], bd_ref[...], gk_ref[...]), *tabs)
        v = y[:, off + QPAD_W + LANES:off + C_GROUP_W]
        kv_ref[0, :, :LANES] = k
        kv_ref[0, :, LANES:] = v
        put(kd_ref, k, d)
        put(vd_ref, v, d)


def _c_proj(x, norm_g, scale, shift, w_in, pos, qk_gain, dils):
    B, T, D = x.shape
    tm = _row_tile(T)
    pieces = []
    for g in range(len(dils)):
        c0 = g * (DIL_Q_W + 2 * DIL_KV_W)
        pieces += [_pad_head_columns(w_in[:, c0:c0 + DIL_Q_W]), w_in[:, c0 + DIL_Q_W:c0 + DIL_Q_W + 2 * DIL_KV_W]]
    w = jnp.concatenate(pieces, axis=1).astype(BF16)
    N = w.shape[1]
    cos_t, msin_t, psin_t = _rope_tables(pos)
    ones_mat = jnp.full((LANES, LANES), 1.0 / HALF, BF16)
    bd = _block_diag2(jnp.full((HALF, HALF), 1.0 / HALF, F32)).astype(BF16)
    consts = [_pad_head_vector(qk_gain[0]), jnp.concatenate([qk_gain[1], qk_gain[1]])[None, :], ones_mat, bd]
    const = lambda a: pl.BlockSpec(a.shape, lambda b, i: (0,) * a.ndim)
    tok = lambda n: pl.BlockSpec((1, tm, n), lambda b, i: (b, i, 0))
    table = pl.BlockSpec((tm, LANES), lambda b, i: (i, 0))
    sds = lambda shape, dt: jax.ShapeDtypeStruct(shape, dt)
    out_shape, out_specs = [], []
    for d in dils:
        out_shape += [sds((B * d, N_QHEADS, T // d, LANES), BF16), sds((B * d, T // d, LANES), BF16),
                      sds((B * d, T // d, LANES), BF16), sds((B, T, 2 * DIL_KV_W), F32)]
        seq = pl.BlockSpec((d, tm // d, LANES), lambda b, i: (b, i, 0))
        out_specs += [pl.BlockSpec((d, N_QHEADS, tm // d, LANES), lambda b, i: (b, 0, i, 0)), seq, seq,
                      tok(2 * DIL_KV_W)]
    return pl.pallas_call(
        functools.partial(_c_proj_kernel, tm=tm, dils=tuple(dils)),
        out_shape=out_shape,
        grid=(B, T // tm),
        in_specs=[tok(D), pl.BlockSpec((1, D), lambda b, i: (0, 0)), _mod_spec(scale, tm), _mod_spec(shift, tm),
                  pl.BlockSpec((D, N), lambda b, i: (0, 0)), table, table, table] + [const(a) for a in consts],
        out_specs=out_specs,
        scratch_shapes=[pltpu.VMEM((tm, LANES), F32)],
        compiler_params=pltpu.CompilerParams(
            dimension_semantics=("parallel", "parallel"), vmem_limit_bytes=VMEM_LIMIT_BYTES),
        name="c_proj",
    )(x, norm_g.reshape(1, D), scale, shift, w, cos_t, msin_t, psin_t, *consts)


def _dil_merge_proj_kernel(*refs, tm, dils):
    o_refs = refs[:len(dils)]
    w_ref, x_ref, gate_ref, out_ref, stage = refs[len(dils):]
    lane = lax.broadcasted_iota(jnp.int32, (1, LANES), 1)
    merged = []
    for hd in range(N_QHEADS):
        lower = hd < N_QHEADS // 2
        vals = []
        for o_ref, d in zip(o_refs, dils):
            if d == 1:
                vals.append(o_ref[0, hd])
            else:
                for r in range(d):
                    stage[pl.ds(r, tm // d, stride=d), :] = o_ref[r, hd]
                vals.append(stage[...])
        lses = [v[:, HALF:HALF + 1] if lower else v[:, 0:1] for v in vals]
        top = jnp.maximum(jnp.maximum(lses[0], lses[1]), lses[2])
        ws = [jnp.exp(l - top) for l in lses]
        total = ws[0] + ws[1] + ws[2]
        merged.append((ws[0] * vals[0] + ws[1] * vals[1] + ws[2] * vals[2]) / total)
    pairs = []
    for c in range(N_QHEADS // 2):
        a, b = merged[2 * c], merged[2 * c + 1]
        if 2 * c < N_QHEADS // 2:
            pairs.append(jnp.where(lane < HALF, a, pltpu.roll(b, HALF, 1)))
        else:
            pairs.append(jnp.where(lane < HALF, pltpu.roll(a, HALF, 1), b))
    o = jnp.concatenate(pairs, axis=-1)
    y = jnp.dot(o.astype(BF16), w_ref[...], preferred_element_type=F32)
    out_ref[0] = x_ref[0] + gate_ref[0] * y


def _dil_merge_proj(outs, dils, w_out, x, gate):
    B, T, D = x.shape
    tm = _row_tile(T)
    in_specs = [pl.BlockSpec((d, N_QHEADS, tm // d, LANES), lambda b, i: (b, 0, i, 0)) for d in dils]
    in_specs += [pl.BlockSpec(w_out.shape, lambda b, i: (0, 0)),
                 pl.BlockSpec((1, tm, D), lambda b, i: (b, i, 0)), _mod_spec(gate, tm)]
    return pl.pallas_call(
        functools.partial(_dil_merge_proj_kernel, tm=tm, dils=tuple(dils)),
        out_shape=jax.ShapeDtypeStruct((B, T, D), F32),
        grid=(B, T // tm),
        in_specs=in_specs,
        out_specs=pl.BlockSpec((1, tm, D), lambda b, i: (b, i, 0)),
        scratch_shapes=[pltpu.VMEM((tm, LANES), F32)],
        compiler_params=pltpu.CompilerParams(
            dimension_semantics=("parallel", "parallel"), vmem_limit_bytes=VMEM_LIMIT_BYTES),
        name="dil_merge_proj",
    )(*outs, w_out, x, gate)


def _kv5(x):
    return x.reshape(x.shape[:2] + (2, NSA_KV_HEADS, HEAD_DIM))


def _mixer_ab(x, norm_g, sc, sh, gate_m, w_in, w_out, start, pasts, weights, page_table, seq_shape):
    qk_gain, pe, w_phi, ln_g, ln_b, ws, bs = weights
    B, T = seq_shape
    pos = start + jnp.arange(x.shape[1], dtype=jnp.int32) % T
    wm, mix_bias = _chunk_mix_weights(ws, bs, T)
    qn, qr, kvc, kvs, ks, vs, kvw, kw, vw, gate, o_b, vln = _ab_proj(
        x, norm_g, sc, sh, w_in, pos, qk_gain, ln_g, ln_b, wm, mix_bias)
    if pasts is None:
        kc, vc = _cmp_finish(_block_mean(kvc), pe, w_phi, qk_gain[1])
        ocmp, sel = _cmp_select(qn, kc, vc, start)
        o_a = _nsa_prompt(qr, ks, vs, kw, vw, sel, ocmp, gate)
        kvw_all = kvw
    else:
        per_seq = lambda a: a.reshape((B, T) + a.shape[2:])
        heads = lambda a: a.reshape(N_QHEADS, B, T, LANES).transpose(1, 0, 2, 3)
        kvc, kvs, kvw, gate, vln = (per_seq(a) for a in (kvc, kvs, kvw, gate, vln))
        qn, qr = heads(qn), heads(qr.astype(F32))
        cache_cmp, cache_slc, win_past = pasts
        means = jnp.swapaxes(_page_block_mean(cache_cmp, page_table), 1, 2)
        kc, vc = _cmp_finish(means, pe, w_phi, qk_gain[1])
        ocmp, sel = _cmp_select(qn, kc, vc, start)
        kvw_all = jnp.concatenate([win_past, kvw], axis=1)
        owin = _band_tail(qr, kvw_all, NSA_WINDOW, 1, False)
        new_tile = jnp.swapaxes(jnp.pad(kvs, ((0, 0), (0, PAGE_SIZE - T), (0, 0))), 1, 2)
        o_a = _nsa_paged(page_table, qr, cache_slc, new_tile, sel, ocmp, owin, gate, start)
    keep_w = min(NSA_WINDOW, kvw_all.shape[1])
    n_cur = (T - 1) % GM_CHUNK + 1
    x_new = _proj_residual([o_a.reshape(o_b.shape), o_b], (w_out[:NSA_Q_W], w_out[NSA_Q_W:]), x, gate_m)
    return x_new, (_kv5(kvc), _kv5(kvs), _kv5(kvw_all[:, kvw_all.shape[1] - keep_w:]), vln[:, T - n_cur:])


def _mixer_c(x, norm_g, sc, sh, gate_m, w_in, w_out, start, pasts, qk_gain, seq_shape):
    B, T = seq_shape
    pos = start + jnp.arange(x.shape[1], dtype=jnp.int32) % T
    dils = tuple(d for _, d in DIL_CFG) if pasts is None else (1,) * len(DIL_CFG)
    res = _c_proj(x, norm_g, sc, sh, w_in, pos, qk_gain, dils)
    outs, new_bufs = [], []
    for g, (window, dilation) in enumerate(DIL_CFG):
        qd, kd, vd, kv_new = res[4 * g:4 * g + 4]
        if pasts is None:
            outs.append(_band_self(qd, kd, vd, window // dilation, True))
            kv_all = kv_new
        else:
            kv_all = jnp.concatenate([pasts[g], kv_new.reshape(B, T, 2 * DIL_KV_W)], axis=1)
            q = qd.reshape(N_QHEADS, B, T, LANES).transpose(1, 0, 2, 3).astype(F32)
            o = _band_tail(q, kv_all, window, dilation, True)
            outs.append(o.transpose(1, 0, 2, 3).reshape(1, N_QHEADS, B * T, LANES))
        keep = min(window, kv_all.shape[1])
        new_bufs.append(kv_all[:, kv_all.shape[1] - keep:].reshape(B, keep, 2, DIL_KV_HEADS, HEAD_DIM))
    return _dil_merge_proj(outs, dils, w_out, x, gate_m), tuple(new_bufs)


def _adaln_kernel(c_ref, w_ref, b_ref, o_ref):
    act = jax.nn.silu(c_ref[...]).astype(BF16)
    o_ref[...] = jnp.dot(act, w_ref[0].astype(BF16), preferred_element_type=F32) + b_ref[0]


def _adaln(c, layer, w, b):
    B, D = c.shape
    depth, _, N = w.shape
    tn = D
    mod = pl.pallas_call(
        _adaln_kernel,
        out_shape=jax.ShapeDtypeStruct((B, N), F32),
        grid=(N // tn,),
        in_specs=[pl.BlockSpec((B, D), lambda j: (0, 0)), pl.BlockSpec((1, D, tn), lambda j: (layer, 0, j)),
                  pl.BlockSpec((1, 1, tn), lambda j: (layer, 0, j))],
        out_specs=pl.BlockSpec((B, tn), lambda j: (0, j)),
        compiler_params=pltpu.CompilerParams(dimension_semantics=("parallel",)),
        name="adaln",
    )(c, w, b.reshape(depth, 1, N))
    return [mod[:, None, j * D:(j + 1) * D] for j in range(N // D)]


def _expand_mod(m, t):
    B, _, D = m.shape
    return jnp.broadcast_to(m, (B, t, D)).reshape(1, B * t, D)


def _slot_buffer(n_tok, d_model):
    n_tiles = -(-(n_tok * TOP_K + N_EXPERTS * (MOE_TILE - 1)) // MOE_TILE)
    return jnp.zeros((n_tiles * MOE_TILE, d_model), F32)


def _layer(groups, mixers, norm_m, norm_f, moe_w, x_sorted):
    w_router, b_router, *ffn_w = moe_w
    xs, states, routed = [], [], []
    counts = jnp.zeros((1, N_EXPERTS), F32)
    for (x, mods), mixer in zip(groups, mixers):
        sh_m, sc_m, g_m, sh_f, sc_f, g_f = mods
        B, T, D = x.shape
        x, state = mixer(x, norm_m, sc_m, sh_m, g_m)
        h, route_i, route_g, counts = _mod_norm_router(x, norm_f, sc_f, sh_f, w_router, b_router, counts)
        xs.append(x)
        states.append(state)
        routed.append((h, route_i, route_g))
    n_tiles = x_sorted.shape[0] // MOE_TILE
    pad_start, pad_end, n_used = _moe_plan(counts[0])
    dests = []
    for h, route_i, _ in routed:
        expert = route_i[..., :TOP_K]
        rank = route_i[..., ROUTE_RANK_LANE:ROUTE_RANK_LANE + TOP_K]
        first = jnp.sum(jnp.where(expert[..., None] == jnp.arange(N_EXPERTS), pad_start, 0), axis=-1)
        dests.append((first + rank).astype(jnp.int32))
        x_sorted = _moe_dispatch(dests[-1], h, x_sorted)
    y_sorted = _moe_ffn(*_moe_tiles(pad_end, n_used, n_tiles), x_sorted, *ffn_w)
    outs = [_moe_combine(dest, y_sorted, route_g, x, mods[5])
            for dest, (_, _, route_g), x, (_, mods) in zip(dests, routed, xs, groups)]
    return outs, states, x_sorted


def _stack(states, j):
    return jnp.stack([s[j] for s in states], axis=0)


def kernel(x_prompt, x_sample, cache_cmp_kv, cache_slc_kv, state_win_kv, state_dil0_kv, state_dil1_kv,
           state_dil2_kv, page_table, c_prompt, c_sample, norm_mix, norm_ffn, w_ada, b_ada, w_in_ab, w_out_ab,
           nsa_qk_gain, nsa_pe, nsa_w_phi, gm_ln_g, gm_ln_b, gm_ws, gm_bs, w_in_c, w_out_c, dil_qk_gain,
           w_router, b_router, w_up, b_up, w_down, b_down):
    depth = norm_mix.shape[0]
    past_len = page_table.shape[1] * PAGE_SIZE
    Bs, Ts, D = x_sample.shape
    x_p = x_prompt
    x_s = x_sample.reshape(1, Bs * Ts, D)
    ab_p, ab_s, dil_p, dil_s = [], [], [], []
    slots = _slot_buffer(x_prompt.shape[0] * x_prompt.shape[1] + Bs * Ts, D)
    for layer in range(depth):
        i = layer // 2
        moe_w = (w_router[layer], b_router[layer], layer, w_up, b_up, w_down, b_down)
        mods_p = _adaln(c_prompt, layer, w_ada, b_ada)
        mods_s = [_expand_mod(m, Ts) for m in _adaln(c_sample, layer, w_ada, b_ada)]
        groups = [(x_p, mods_p), (x_s, mods_s)]
        if layer % 2 == 0:
            wts = (nsa_qk_gain[i], nsa_pe[i], nsa_w_phi[i], gm_ln_g[i], gm_ln_b[i], gm_ws[i], gm_bs[i])
            flat_kv = lambda a: a.reshape(a.shape[:2] + (NSA_KV_W,))
            pasts_s = (_feature_major_pages(cache_cmp_kv[i]), _feature_major_pages(cache_slc_kv[i]),
                       flat_kv(state_win_kv[i]))

            w_io = (w_in_ab[i], w_out_ab[i].astype(BF16))

            def mix_p(x, g, sc, sh, gm, wts=wts, w_io=w_io):
                return _mixer_ab(x, g, sc, sh, gm, *w_io, 0, None, wts, None, x.shape[:2])

            def mix_s(x, g, sc, sh, gm, wts=wts, w_io=w_io, pasts_s=pasts_s):
                return _mixer_ab(x, g, sc, sh, gm, *w_io, past_len, pasts_s, wts, page_table, (Bs, Ts))

            (x_p, x_s), (st_p, st_s), slots = _layer(
                groups, (mix_p, mix_s), norm_mix[layer], norm_ffn[layer], moe_w, slots)
            ab_p.append(st_p)
            ab_s.append(st_s)
        else:
            flat_kv = lambda a: a.reshape(a.shape[:2] + (2 * DIL_KV_W,))
            pasts_s = (flat_kv(state_dil0_kv[i]), flat_kv(state_dil1_kv[i]), flat_kv(state_dil2_kv[i]))

            w_io = (w_in_c[i], w_out_c[i].astype(BF16))

            def mix_p(x, g, sc, sh, gm, gain=dil_qk_gain[i], w_io=w_io):
                return _mixer_c(x, g, sc, sh, gm, *w_io, 0, None, gain, x.shape[:2])

            def mix_s(x, g, sc, sh, gm, pasts_s=pasts_s, gain=dil_qk_gain[i], w_io=w_io):
                return _mixer_c(x, g, sc, sh, gm, *w_io, past_len, pasts_s, gain, (Bs, Ts))

            (x_p, x_s), (st_p, st_s), slots = _layer(
                groups, (mix_p, mix_s), norm_mix[layer], norm_ffn[layer], moe_w, slots)
            dil_p.append(st_p)
            dil_s.append(st_s)
    x_s = x_s.reshape(Bs, Ts, D)
    cmp_p, cmp_s = _stack(ab_p, 0), _stack(ab_s, 0)
    slc_p, slc_s = _stack(ab_p, 1), _stack(ab_s, 1)
    win_p, win_s = _stack(ab_p, 2), _stack(ab_s, 2)
    gmv_p, gmv_s = _stack(ab_p, 3), _stack(ab_s, 3)
    d0_p, d0_s = _stack(dil_p, 0), _stack(dil_s, 0)
    d1_p, d1_s = _stack(dil_p, 1), _stack(dil_s, 1)
    d2_p, d2_s = _stack(dil_p, 2), _stack(dil_s, 2)
    return (x_p, x_s, cmp_p, cmp_s, slc_p, slc_s, win_p, win_s, gmv_p, gmv_s, d0_p, d0_s, d1_p, d1_s, d2_p, d2_s)
```

```python
import functools
import math

import jax
import jax.numpy as jnp
from jax import lax
from jax.experimental import pallas as pl
from jax.experimental.pallas import tpu as pltpu

F32 = jnp.float32
BF16 = jnp.bfloat16

D_MODEL = 1024
HEAD_DIM = 64
ROT_DIM = HEAD_DIM // 4
ROPE_THETA = 500000.0
NORM_EPS = 1e-6
NEG_INF = -1e30
Q_BLOCK = 128
ATTN_Q_TILE = 256
PAGE_SIZE = 128

NSA_HEADS = D_MODEL // (2 * HEAD_DIM)
NSA_KV_HEADS = 2
NSA_GROUP = NSA_HEADS // NSA_KV_HEADS
NSA_BLOCK = 64
NSA_N_SEL = 16
NSA_WINDOW = 512
NSA_Q_W = NSA_HEADS * HEAD_DIM
NSA_KV_W = 2 * NSA_KV_HEADS * HEAD_DIM

GM_GROUPS = 8
GM_WIDTH = D_MODEL // 2
GM_GROUP_W = GM_WIDTH // GM_GROUPS
GM_CHUNK = 128

AB_SPLITS = (NSA_Q_W, 3 * NSA_HEADS, NSA_KV_W, NSA_KV_W, NSA_KV_W, GM_WIDTH, GM_WIDTH)

DIL_CFG = ((128, 1), (512, 4), (2048, 16))
DIL_HEADS = 8
DIL_KV_HEADS = 2
DIL_Q_W = DIL_HEADS * HEAD_DIM
DIL_KV_W = DIL_KV_HEADS * HEAD_DIM

N_EXPERTS = 32
TOP_K = 4
D_FF = D_MODEL
SWIGLU_ALPHA = 1.702
SWIGLU_LIMIT = 7.0

VMEM_LIMIT_BYTES = 56 * 1024 * 1024
ROW_TILE = 512
MOE_TILE = 512


def _row_tile(t):
    return t if t <= ROW_TILE else ROW_TILE


def _mod_spec(mod, tm):
    if mod.shape[1] == 1:
        return pl.BlockSpec((1, 1, mod.shape[2]), lambda b, i: (b, 0, 0))
    return pl.BlockSpec((1, tm, mod.shape[2]), lambda b, i: (b, i, 0))


def _split_bf16(v):
    hi = v.astype(BF16)
    lo = (v - hi.astype(F32)).astype(BF16)
    return hi, lo


def _mod_norm(x, g, sc, sh):
    h = x * lax.rsqrt(jnp.mean(x * x, axis=-1, keepdims=True) + NORM_EPS) * g
    return h * (1.0 + sc) + sh


def _proj_residual_kernel(*refs, n_parts):
    a_refs = refs[:n_parts]
    w_refs = refs[n_parts:2 * n_parts]
    x_ref, g_ref, o_ref = refs[2 * n_parts:]
    y = None
    for a_ref, w_ref in zip(a_refs, w_refs):
        p = jnp.dot(a_ref[0].astype(BF16), w_ref[...], preferred_element_type=F32)
        y = p if y is None else y + p
    o_ref[0] = x_ref[0] + g_ref[0] * y


def _proj_residual(a_parts, w_parts_bf16, x, gate):
    B, T, D = x.shape
    tm = _row_tile(T)
    n_parts = len(a_parts)
    in_specs = [pl.BlockSpec((1, tm, a.shape[2]), lambda b, i: (b, i, 0)) for a in a_parts]
    in_specs += [pl.BlockSpec(w.shape, lambda b, i: (0, 0)) for w in w_parts_bf16]
    in_specs += [pl.BlockSpec((1, tm, D), lambda b, i: (b, i, 0)), _mod_spec(gate, tm)]
    return pl.pallas_call(
        functools.partial(_proj_residual_kernel, n_parts=n_parts),
        out_shape=jax.ShapeDtypeStruct((B, T, D), F32),
        grid=(B, T // tm),
        in_specs=in_specs,
        out_specs=pl.BlockSpec((1, tm, D), lambda b, i: (b, i, 0)),
        compiler_params=pltpu.CompilerParams(
            dimension_semantics=("parallel", "parallel"), vmem_limit_bytes=VMEM_LIMIT_BYTES),
        name="proj_residual",
    )(*a_parts, *w_parts_bf16, x, gate)


ROUTE_RANK_LANE = TOP_K


def _mod_norm_router_kernel(x_ref, g_ref, sc_ref, sh_ref, whi_ref, wlo_ref, b_ref, cnt_in_ref,
                            h_ref, ri_ref, rg_ref, cnt_ref, *, tm):
    @pl.when((pl.program_id(0) == 0) & (pl.program_id(1) == 0))
    def _():
        cnt_ref[...] = cnt_in_ref[...]

    h = _mod_norm(x_ref[0], g_ref[...], sc_ref[0], sh_ref[0])
    h_ref[0] = h
    h_hi, h_lo = _split_bf16(h)
    w_hi = whi_ref[...]
    logits = (jnp.dot(h_hi, w_hi, preferred_element_type=F32)
              + jnp.dot(h_lo, w_hi, preferred_element_type=F32)
              + jnp.dot(h_hi, wlo_ref[...], preferred_element_type=F32)) + b_ref[...]
    n_exp = logits.shape[-1]
    lane = lax.broadcasted_iota(jnp.int32, (1, n_exp), 1)
    work = logits
    vals, ids, hits = [], [], []
    for _ in range(TOP_K):
        m = jnp.max(work, axis=-1, keepdims=True)
        idx = jnp.min(jnp.where(work == m, lane, n_exp), axis=-1, keepdims=True)
        hit = lane == idx
        vals.append(m)
        ids.append(idx)
        hits.append(hit)
        work = jnp.where(hit, NEG_INF, work)
    exps = [jnp.exp(v - vals[0]) for v in vals]
    denom = exps[0] + exps[1] + exps[2] + exps[3]
    onehot = jnp.zeros(logits.shape, F32)
    for hit in hits:
        onehot = onehot + jnp.where(hit, 1.0, 0.0)
    row = lax.broadcasted_iota(jnp.int32, (tm, tm), 0)
    col = lax.broadcasted_iota(jnp.int32, (tm, tm), 1)
    earlier = jnp.where(row > col, 1.0, 0.0).astype(BF16)
    before = jnp.dot(earlier, onehot.astype(BF16), preferred_element_type=F32) + cnt_ref[...]
    cnt_ref[...] = cnt_ref[...] + jnp.sum(onehot, axis=0, keepdims=True)
    out_lane = lax.broadcasted_iota(jnp.int32, (1, LANES), 1)
    ri = jnp.zeros((tm, LANES), jnp.int32)
    rg = jnp.zeros((tm, LANES), F32)
    for k in range(TOP_K):
        rank = jnp.sum(jnp.where(hits[k], before, 0.0), axis=-1, keepdims=True).astype(jnp.int32)
        ri = jnp.where(out_lane == k, ids[k], ri)
        ri = jnp.where(out_lane == ROUTE_RANK_LANE + k, rank, ri)
        rg = jnp.where(out_lane == k, exps[k] / denom, rg)
    ri_ref[0] = ri
    rg_ref[0] = rg


def _mod_norm_router(x, norm_g, scale, shift, w_router, b_router, counts_in):
    B, T, D = x.shape
    tm = _row_tile(T)
    E = w_router.shape[1]
    w_hi, w_lo = _split_bf16(w_router)
    tok = lambda n: pl.BlockSpec((1, tm, n), lambda b, i: (b, i, 0))
    return pl.pallas_call(
        functools.partial(_mod_norm_router_kernel, tm=tm),
        out_shape=[jax.ShapeDtypeStruct((B, T, D), F32), jax.ShapeDtypeStruct((B, T, LANES), jnp.int32),
                   jax.ShapeDtypeStruct((B, T, LANES), F32), jax.ShapeDtypeStruct((1, E), F32)],
        grid=(B, T // tm),
        in_specs=[
            tok(D),
            pl.BlockSpec((1, D), lambda b, i: (0, 0)),
            _mod_spec(scale, tm),
            _mod_spec(shift, tm),
            pl.BlockSpec((D, E), lambda b, i: (0, 0)),
            pl.BlockSpec((D, E), lambda b, i: (0, 0)),
            pl.BlockSpec((1, E), lambda b, i: (0, 0)),
            pl.BlockSpec((1, E), lambda b, i: (0, 0)),
        ],
        out_specs=[tok(D), tok(LANES), tok(LANES), pl.BlockSpec((1, E), lambda b, i: (0, 0))],
        compiler_params=pltpu.CompilerParams(
            dimension_semantics=("arbitrary", "arbitrary"), vmem_limit_bytes=VMEM_LIMIT_BYTES),
        name="mod_norm_router",
    )(x, norm_g.reshape(1, D), scale, shift, w_hi, w_lo, b_router.reshape(1, E), counts_in)


def _moe_ffn_kernel(te_ref, tx_ref, tv_ref, x_ref, wu_ref, bu_ref, wd_ref, bd_ref, o_ref):
    i = pl.program_id(0)

    @pl.when(tv_ref[i] != 0)
    def _():
        hu = jnp.dot(x_ref[...].astype(BF16), wu_ref[0, 0].astype(BF16), preferred_element_type=F32) + bu_ref[0, 0]
        glu = jnp.minimum(hu[:, :D_FF], SWIGLU_LIMIT)
        lin = jnp.clip(hu[:, D_FF:], -SWIGLU_LIMIT, SWIGLU_LIMIT)
        act = glu * jax.nn.sigmoid(SWIGLU_ALPHA * glu) * (lin + 1.0)
        o_ref[...] = (jnp.dot(act.astype(BF16), wd_ref[0, 0].astype(BF16), preferred_element_type=F32)
                      + bd_ref[0, 0])

    @pl.when(tv_ref[i] == 0)
    def _():
        o_ref[...] = jnp.zeros_like(o_ref)


def _moe_ffn(tile_exp, tile_x, tile_valid, x_sorted, layer, w_up, b_up, w_down, b_down):
    n_slots, D = x_sorted.shape
    n_tiles = n_slots // MOE_TILE
    depth, E, _, F2 = w_up.shape
    expert = lambda i, te, tx, tv: (layer, te[i], 0, 0)
    grid_spec = pltpu.PrefetchScalarGridSpec(
        num_scalar_prefetch=3,
        grid=(n_tiles,),
        in_specs=[
            pl.BlockSpec((MOE_TILE, D), lambda i, te, tx, tv: (tx[i], 0)),
            pl.BlockSpec((1, 1, D, F2), expert),
            pl.BlockSpec((1, 1, 1, F2), expert),
            pl.BlockSpec((1, 1, F2 // 2, D), expert),
            pl.BlockSpec((1, 1, 1, D), expert),
        ],
        out_specs=pl.BlockSpec((MOE_TILE, D), lambda i, te, tx, tv: (i, 0)),
    )
    return pl.pallas_call(
        _moe_ffn_kernel,
        out_shape=jax.ShapeDtypeStruct((n_slots, D), F32),
        grid_spec=grid_spec,
        compiler_params=pltpu.CompilerParams(
            dimension_semantics=("arbitrary",), vmem_limit_bytes=VMEM_LIMIT_BYTES),
        name="moe_ffn",
    )(tile_exp, tile_x, tile_valid, x_sorted, w_up, b_up.reshape(depth, E, 1, F2), w_down,
      b_down.reshape(depth, E, 1, D))


DISPATCH_TILE = 512
COMBINE_TILE = 256


def _row_copy(src_ref, src_row, dst_ref, dst_row, sem):
    return pltpu.make_async_copy(src_ref.at[pl.ds(src_row, 1)], dst_ref.at[pl.ds(dst_row, 1)], sem)


def _moe_dispatch_kernel(dest_ref, h_ref, xs_in_ref, xs_ref, sem, *, tm):
    del xs_in_ref
    h_rows = h_ref.at[0]

    def issue(t, carry):
        for k in range(TOP_K):
            _row_copy(h_rows, t, xs_ref, dest_ref[0, 0, 0, t * TOP_K + k], sem).start()
        return carry

    lax.fori_loop(0, tm, issue, 0, unroll=4)
    for k in range(TOP_K):
        pltpu.make_async_copy(h_rows, xs_ref.at[pl.ds(0, tm)], sem).wait()


def _dest_spec(tm):
    return pl.BlockSpec((1, 1, 1, tm * TOP_K), lambda b, i: (b, i, 0, 0), memory_space=pltpu.SMEM)


def _moe_dispatch(dest, h, x_sorted):
    B, T, D = h.shape
    tm = min(T, DISPATCH_TILE)
    return pl.pallas_call(
        functools.partial(_moe_dispatch_kernel, tm=tm),
        out_shape=jax.ShapeDtypeStruct(x_sorted.shape, x_sorted.dtype),
        grid=(B, T // tm),
        in_specs=[_dest_spec(tm),
                  pl.BlockSpec((1, tm, D), lambda b, i: (b, i, 0)),
                  pl.BlockSpec(memory_space=pl.ANY)],
        out_specs=pl.BlockSpec(memory_space=pl.ANY),
        scratch_shapes=[pltpu.SemaphoreType.DMA(())],
        input_output_aliases={2: 0},
        compiler_params=pltpu.CompilerParams(dimension_semantics=("arbitrary", "arbitrary")),
        name="moe_dispatch",
    )(dest.reshape(B, T // tm, 1, tm * TOP_K), h, x_sorted)


def _moe_combine_kernel(dest_ref, ys_ref, rg_ref, x_ref, gf_ref, o_ref, buf, sem, *, tm):
    def issue(t, carry):
        for k in range(TOP_K):
            _row_copy(ys_ref, dest_ref[0, 0, 0, t * TOP_K + k], buf.at[k], t, sem).start()
        return carry

    lax.fori_loop(0, tm, issue, 0, unroll=4)
    for k in range(TOP_K):
        pltpu.make_async_copy(ys_ref.at[pl.ds(0, tm)], buf.at[k], sem).wait()
    gates = rg_ref[0]
    y = gates[:, 0:1] * buf[0]
    for k in range(1, TOP_K):
        y = y + gates[:, k:k + 1] * buf[k]
    o_ref[0] = x_ref[0] + gf_ref[0] * y


def _moe_combine(dest, y_sorted, route_g, x, gate_f):
    B, T, D = x.shape
    tm = min(T, COMBINE_TILE)
    tok = lambda n: pl.BlockSpec((1, tm, n), lambda b, i: (b, i, 0))
    return pl.pallas_call(
        functools.partial(_moe_combine_kernel, tm=tm),
        out_shape=jax.ShapeDtypeStruct((B, T, D), F32),
        grid=(B, T // tm),
        in_specs=[_dest_spec(tm), pl.BlockSpec(memory_space=pl.ANY), tok(LANES), tok(D), _mod_spec(gate_f, tm)],
        out_specs=tok(D),
        scratch_shapes=[pltpu.VMEM((TOP_K, tm, D), F32), pltpu.SemaphoreType.DMA(())],
        compiler_params=pltpu.CompilerParams(
            dimension_semantics=("parallel", "parallel"), vmem_limit_bytes=VMEM_LIMIT_BYTES),
        name="moe_combine",
    )(dest.reshape(B, T // tm, 1, tm * TOP_K), y_sorted, route_g, x, gate_f)


def _moe_plan(counts):
    counts = counts.astype(jnp.int32)
    padded = (counts + MOE_TILE - 1) // MOE_TILE * MOE_TILE
    pad_end = jnp.cumsum(padded)
    pad_start = pad_end - padded
    n_used = pad_end[-1] // MOE_TILE
    return pad_start, pad_end, n_used


def _moe_tiles(pad_end, n_used, n_tiles):
    tile = jnp.arange(n_tiles, dtype=jnp.int32)
    tile_valid = (tile < n_used).astype(jnp.int32)
    tile_x = jnp.minimum(tile, n_used - 1)
    tile_exp = jnp.sum((tile_x[:, None] * MOE_TILE >= pad_end[None, :]).astype(jnp.int32), axis=1)
    return jnp.minimum(tile_exp, N_EXPERTS - 1), tile_x, tile_valid


N_QHEADS = 8
LANES = 128
HALF = HEAD_DIM


def _nt_dot(a, b):
    return lax.dot_general(a, b, (((1,), (1,)), ((), ())), preferred_element_type=F32)


def _band_mask(qpos, kpos, window, dilation):
    delta = qpos - kpos
    valid = (delta >= 0) & (delta <= window)
    if dilation > 1:
        valid = valid & ((delta & (dilation - 1)) == 0)
    return valid


def _softmax_pv(s, valid, v, tq):
    n = s.shape[-1]
    s = jnp.where(valid[None], s.reshape(N_QHEADS, tq, n), NEG_INF)
    m = jnp.max(s, axis=-1, keepdims=True)
    p = jnp.exp(s - m)
    l = jnp.sum(p, axis=-1, keepdims=True)
    o = jnp.dot(p.reshape(N_QHEADS * tq, n).astype(BF16), v, preferred_element_type=F32)
    return o.reshape(N_QHEADS, tq, LANES) / l, m + jnp.log(l)


def _store_heads(o_ref, o, lse):
    if lse is None:
        o_ref[0] = o
        return
    lane = lax.broadcasted_iota(jnp.int32, (1, LANES), 1)
    for h in range(N_QHEADS):
        own = (lane < HALF) if h < N_QHEADS // 2 else (lane >= HALF)
        o_ref[0, h] = jnp.where(own, o[h], lse[h])


def _band_self_kernel(q_ref, k_ref, v_ref, o_ref, *, window, tq, span, want_lse):
    T = k_ref.shape[1]
    q0 = pl.program_id(1) * tq
    if span == T:
        start = 0
        k = k_ref[0]
        v = v_ref[0]
    else:
        start = pl.multiple_of(jnp.maximum(q0 - window, 0), LANES)
        k = k_ref[0, pl.ds(start, span), :]
        v = v_ref[0, pl.ds(start, span), :]
    q = q_ref[0].reshape(N_QHEADS * tq, LANES).astype(BF16)
    qpos = q0 + lax.broadcasted_iota(jnp.int32, (tq, 1), 0)
    kpos = start + lax.broadcasted_iota(jnp.int32, (1, span), 1)
    o, lse = _softmax_pv(_nt_dot(q, k), _band_mask(qpos, kpos, window, 1), v, tq)
    _store_heads(o_ref, o, lse if want_lse else None)


def _band_self(q, k, v, window, want_lse):
    B, _, T, _ = q.shape
    tq = min(T, ATTN_Q_TILE)
    span = min(T, window + tq)
    return pl.pallas_call(
        functools.partial(_band_self_kernel, window=window, tq=tq, span=span, want_lse=want_lse),
        out_shape=jax.ShapeDtypeStruct((B, N_QHEADS, T, LANES), F32),
        grid=(B, T // tq),
        in_specs=[
            pl.BlockSpec((1, N_QHEADS, tq, LANES), lambda b, i: (b, 0, i, 0)),
            pl.BlockSpec((1, T, LANES), lambda b, i: (b, 0, 0)),
            pl.BlockSpec((1, T, LANES), lambda b, i: (b, 0, 0)),
        ],
        out_specs=pl.BlockSpec((1, N_QHEADS, tq, LANES), lambda b, i: (b, 0, i, 0)),
        compiler_params=pltpu.CompilerParams(
            dimension_semantics=("parallel", "parallel"), vmem_limit_bytes=VMEM_LIMIT_BYTES),
        name="band_self",
    )(q, k, v)


def _band_tail_kernel(q_ref, kv_ref, o_ref, *, window, dilation, tq, want_lse):
    L = kv_ref.shape[1]
    kv = kv_ref[0]
    k = kv[:, :LANES].astype(BF16)
    v = kv[:, LANES:].astype(BF16)
    q = q_ref[0].reshape(N_QHEADS * tq, LANES).astype(BF16)
    qpos = (L - tq) + lax.broadcasted_iota(jnp.int32, (tq, 1), 0)
    kpos = lax.broadcasted_iota(jnp.int32, (1, L), 1)
    o, lse = _softmax_pv(_nt_dot(q, k), _band_mask(qpos, kpos, window, dilation), v, tq)
    _store_heads(o_ref, o, lse if want_lse else None)


def _band_tail(q, kv_all, window, dilation, want_lse):
    B, _, T, _ = q.shape
    L = kv_all.shape[1]
    return pl.pallas_call(
        functools.partial(_band_tail_kernel, window=window, dilation=dilation, tq=T, want_lse=want_lse),
        out_shape=jax.ShapeDtypeStruct((B, N_QHEADS, T, LANES), F32),
        grid=(B,),
        in_specs=[
            pl.BlockSpec((1, N_QHEADS, T, LANES), lambda b: (b, 0, 0, 0)),
            pl.BlockSpec((1, L, 2 * LANES), lambda b: (b, 0, 0)),
        ],
        out_specs=pl.BlockSpec((1, N_QHEADS, T, LANES), lambda b: (b, 0, 0, 0)),
        compiler_params=pltpu.CompilerParams(
            dimension_semantics=("parallel",), vmem_limit_bytes=VMEM_LIMIT_BYTES),
        name="band_tail",
    )(q, kv_all)


def _block_mean_kernel(x_ref, o_ref):
    rows = x_ref.shape[1]
    x = x_ref[0].reshape(rows // NSA_BLOCK, NSA_BLOCK, x_ref.shape[2])
    o_ref[0] = jnp.sum(x, axis=1) * (1.0 / NSA_BLOCK)


def _block_mean(kvc):
    B, T, W = kvc.shape
    tm = _row_tile(T)
    return pl.pallas_call(
        _block_mean_kernel,
        out_shape=jax.ShapeDtypeStruct((B, T // NSA_BLOCK, W), F32),
        grid=(B, T // tm),
        in_specs=[pl.BlockSpec((1, tm, W), lambda b, i: (b, i, 0))],
        out_specs=pl.BlockSpec((1, tm // NSA_BLOCK, W), lambda b, i: (b, i, 0)),
        compiler_params=pltpu.CompilerParams(dimension_semantics=("parallel", "parallel")),
        name="block_mean",
    )(kvc)


PAGES_PER_STEP = 8


def _feature_major_pages(cache):
    n_phys, rows = cache.shape[:2]
    return cache.transpose(0, 2, 3, 4, 1).reshape(n_phys, -1, rows)


def _page_specs(n_feat):
    return [pl.BlockSpec((1, n_feat, PAGE_SIZE), functools.partial(
        lambda b, s, pt, pg: (pt[b, s * PAGES_PER_STEP + pg], 0, 0), pg=pg)) for pg in range(PAGES_PER_STEP)]


def _page_block_mean_kernel(pt_ref, *refs):
    page_refs, o_ref = refs[:-1], refs[-1]
    step = pl.program_id(1)
    n_blocks = o_ref.shape[2]
    per_page = PAGE_SIZE // NSA_BLOCK

    @pl.when(step == 0)
    def _():
        o_ref[...] = jnp.zeros(o_ref.shape, F32)

    row_blk = lax.broadcasted_iota(jnp.int32, (PAGE_SIZE, 1), 0) // NSA_BLOCK
    col = lax.broadcasted_iota(jnp.int32, (1, n_blocks), 1)
    acc = o_ref[0]
    for pg, page_ref in enumerate(page_refs):
        first = (step * PAGES_PER_STEP + pg) * per_page
        avg = jnp.where(col == first + row_blk, 1.0 / NSA_BLOCK, 0.0).astype(BF16)
        hi, lo = _split_bf16(page_ref[0])
        acc = acc + jnp.dot(hi, avg, preferred_element_type=F32) + jnp.dot(lo, avg, preferred_element_type=F32)
    o_ref[0] = acc


def _page_block_mean(pages, page_table):
    B, n_pages = page_table.shape
    F = pages.shape[1]
    n_blocks = n_pages * PAGE_SIZE // NSA_BLOCK
    grid_spec = pltpu.PrefetchScalarGridSpec(
        num_scalar_prefetch=1,
        grid=(B, n_pages // PAGES_PER_STEP),
        in_specs=_page_specs(F),
        out_specs=pl.BlockSpec((1, F, n_blocks), lambda b, s, pt: (b, 0, 0)),
    )
    return pl.pallas_call(
        _page_block_mean_kernel,
        out_shape=jax.ShapeDtypeStruct((B, F, n_blocks), F32),
        grid_spec=grid_spec,
        compiler_params=pltpu.CompilerParams(dimension_semantics=("parallel", "arbitrary")),
        name="page_block_mean",
    )(page_table, *([pages] * PAGES_PER_STEP))


def _head_mean_sq(x, ones_bd):
    return jnp.dot((x * x).astype(BF16), ones_bd, preferred_element_type=F32)


def _cmp_finish_kernel(mean_ref, pe_ref, wk_ref, wv_ref, gain_ref, bd_ref, kc_ref, vc_ref):
    s = mean_ref[0] + pe_ref[...]
    sk = jnp.dot(s[:, :LANES].astype(BF16), wk_ref[...], preferred_element_type=F32)
    vc_ref[0] = jnp.dot(s[:, LANES:].astype(BF16), wv_ref[...], preferred_element_type=F32)
    kc_ref[0] = sk * lax.rsqrt(_head_mean_sq(sk, bd_ref[...]) + NORM_EPS) * gain_ref[...]


def _block_diag2(w):
    z = jnp.zeros_like(w)
    return jnp.concatenate([jnp.concatenate([w, z], axis=1), jnp.concatenate([z, w], axis=1)], axis=0)


def _cmp_finish(mean, pe, w_phi, kc_gain):
    B, n_cb, W = mean.shape
    pe_mean = jnp.mean(pe, axis=0)
    pe_row = jnp.concatenate([pe_mean[0], pe_mean[0], pe_mean[1], pe_mean[1]])[None, :]
    wk = _block_diag2(w_phi[0]).astype(BF16)
    wv = _block_diag2(w_phi[1]).astype(BF16)
    bd = _block_diag2(jnp.full((HALF, HALF), 1.0 / HALF, F32)).astype(BF16)
    gain = jnp.concatenate([kc_gain, kc_gain])[None, :]
    full = lambda shape: pl.BlockSpec(shape, lambda b: (0,) * len(shape))
    return pl.pallas_call(
        _cmp_finish_kernel,
        out_shape=[jax.ShapeDtypeStruct((B, n_cb, LANES), F32)] * 2,
        grid=(B,),
        in_specs=[pl.BlockSpec((1, n_cb, W), lambda b: (b, 0, 0)), full((1, W)), full((LANES, LANES)),
                  full((LANES, LANES)), full((1, LANES)), full((LANES, LANES))],
        out_specs=[pl.BlockSpec((1, n_cb, LANES), lambda b: (b, 0, 0))] * 2,
        compiler_params=pltpu.CompilerParams(dimension_semantics=("parallel",)),
        name="cmp_finish",
    )(mean, pe_row, wk, wv, gain, bd)


def _cmp_select_kernel(q_ref, kc_ref, vc_ref, ocmp_ref, sel_ref, *, tq, pos0):
    n_cb = kc_ref.shape[1]
    q0 = pl.program_id(1) * tq
    qh, ql = _split_bf16(q_ref[0].reshape(N_QHEADS * tq, LANES))
    kh, kl = _split_bf16(kc_ref[0])
    s = (_nt_dot(qh, kh) + _nt_dot(ql, kh) + _nt_dot(qh, kl)) * (HEAD_DIM ** -0.5)
    cur = (pos0 + q0 + lax.broadcasted_iota(jnp.int32, (tq, 1), 0)) // NSA_BLOCK
    blk = lax.broadcasted_iota(jnp.int32, (1, n_cb), 1)
    vis = blk < cur
    s = jnp.where(vis[None], s.reshape(N_QHEADS, tq, n_cb), NEG_INF)
    p = jnp.exp(s - jnp.max(s, axis=-1, keepdims=True))
    p = jnp.where(vis[None], p / jnp.sum(p, axis=-1, keepdims=True), 0.0)
    o = jnp.dot(p.reshape(N_QHEADS * tq, n_cb).astype(BF16), vc_ref[0].astype(BF16), preferred_element_type=F32)
    ocmp_ref[0] = o.reshape(N_QHEADS, tq, LANES)
    for kvh in range(NSA_KV_HEADS):
        g0 = kvh * NSA_GROUP
        imp = jnp.where(vis, p[g0] + p[g0 + 1] + p[g0 + 2] + p[g0 + 3], -1.0)
        if tq % LANES == 0:
            imp_t = imp.T
            blk_t = lax.broadcasted_iota(jnp.int32, (n_cb, tq), 0)
            cur_t = (pos0 + q0 + lax.broadcasted_iota(jnp.int32, (1, tq), 1)) // NSA_BLOCK
            rank = jnp.zeros((n_cb, tq), jnp.int32)
            for i in range(n_cb):
                row = imp_t[i:i + 1, :]
                wins_tie = jnp.where(blk_t > i, 1, 0)
                rank = rank + jnp.where(row > imp_t, 1, jnp.where(row == imp_t, wins_tie, 0))
            chosen = jnp.where(blk_t < cur_t, jnp.where(rank < NSA_N_SEL - 1, 1.0, 0.0),
                               jnp.where(blk_t == cur_t, 1.0, 0.0)).T
        else:
            rank = jnp.zeros((tq, n_cb), jnp.int32)
            for i in range(n_cb):
                col = imp[:, i:i + 1]
                wins_tie = jnp.where(blk > i, 1, 0)
                rank = rank + jnp.where(col > imp, 1, jnp.where(col == imp, wins_tie, 0))
            chosen = jnp.where(vis, jnp.where(rank < NSA_N_SEL - 1, 1.0, 0.0), jnp.where(blk == cur, 1.0, 0.0))
        sel_ref[0, kvh] = chosen.astype(BF16)


def _cmp_select(qn, kc, vc, pos0):
    B, _, T, _ = qn.shape
    n_cb = kc.shape[1]
    tq = min(T, Q_BLOCK)
    return pl.pallas_call(
        functools.partial(_cmp_select_kernel, tq=tq, pos0=pos0),
        out_shape=[jax.ShapeDtypeStruct((B, N_QHEADS, T, LANES), F32),
                   jax.ShapeDtypeStruct((B, NSA_KV_HEADS, T, n_cb), BF16)],
        grid=(B, T // tq),
        in_specs=[
            pl.BlockSpec((1, N_QHEADS, tq, LANES), lambda b, i: (b, 0, i, 0)),
            pl.BlockSpec((1, n_cb, LANES), lambda b, i: (b, 0, 0)),
            pl.BlockSpec((1, n_cb, LANES), lambda b, i: (b, 0, 0)),
        ],
        out_specs=[pl.BlockSpec((1, N_QHEADS, tq, LANES), lambda b, i: (b, 0, i, 0)),
                   pl.BlockSpec((1, NSA_KV_HEADS, tq, n_cb), lambda b, i: (b, 0, i, 0))],
        compiler_params=pltpu.CompilerParams(
            dimension_semantics=("parallel", "parallel"), vmem_limit_bytes=VMEM_LIMIT_BYTES),
        name="cmp_select",
    )(qn, kc, vc)


def _selection_bias_kv(sel, first_key, n_keys, causal):
    n_cb = sel.shape[-1]
    key_blk = (first_key + lax.broadcasted_iota(jnp.int32, (1, n_keys), 1)) // NSA_BLOCK
    expand = jnp.where(lax.broadcasted_iota(jnp.int32, (n_cb, 1), 0) == key_blk, 1.0, 0.0).astype(BF16)
    out = []
    for kvh in range(NSA_KV_HEADS):
        picked = jnp.dot(sel[kvh], expand, preferred_element_type=F32)
        bias = (picked - 1.0) * 1e30
        if causal is not None:
            bias = jnp.where(causal, bias, NEG_INF)
        out.append(bias)
    return out


def _selection_bias(sel, first_key, n_keys, causal):
    return jnp.concatenate([jnp.broadcast_to(b[None], (NSA_GROUP,) + b.shape)
                            for b in _selection_bias_kv(sel, first_key, n_keys, causal)], axis=0)


def _online_softmax_step(s, v, m_ref, l_ref, acc_ref, tq, v_feature_major=False):
    n = s.shape[-1]
    m_old = m_ref[...]
    m_new = jnp.maximum(m_old, jnp.max(s, axis=-1, keepdims=True))
    alpha = jnp.exp(m_old - m_new)
    p = jnp.exp(s - m_new[..., :1])
    l_ref[...] = alpha * l_ref[...] + jnp.sum(p, axis=-1, keepdims=True)
    p2 = p.reshape(N_QHEADS * tq, n).astype(BF16)
    pv = _nt_dot(p2, v) if v_feature_major else jnp.dot(p2, v, preferred_element_type=F32)
    acc_ref[...] = alpha * acc_ref[...] + pv.reshape(N_QHEADS, tq, LANES)
    m_ref[...] = m_new


def _init_softmax_state(m_ref, l_ref, acc_ref):
    m_ref[...] = jnp.full(m_ref.shape, NEG_INF, F32)
    l_ref[...] = jnp.zeros(l_ref.shape, F32)
    acc_ref[...] = jnp.zeros(acc_ref.shape, F32)


def _gated_heads(gate_pre, o_cmp, o_slc, o_win):
    g = jax.nn.sigmoid(gate_pre)
    lane = lax.broadcasted_iota(jnp.int32, (1, LANES), 1)
    pairs = []
    for c in range(N_QHEADS // 2):
        mixed = []
        for h in (2 * c, 2 * c + 1):
            mixed.append(g[:, 3 * h:3 * h + 1] * o_cmp[h] + g[:, 3 * h + 1:3 * h + 2] * o_slc[h]
                         + g[:, 3 * h + 2:3 * h + 3] * o_win[h])
        if 2 * c < N_QHEADS // 2:
            pairs.append(jnp.where(lane < HALF, mixed[0], pltpu.roll(mixed[1], HALF, 1)))
        else:
            pairs.append(jnp.where(lane < HALF, pltpu.roll(mixed[0], HALF, 1), mixed[1]))
    return jnp.concatenate(pairs, axis=-1)


SLC_KEY_TILE = 512


def _nsa_prompt_kernel(q_ref, ks_ref, vs_ref, kw_ref, vw_ref, sel_ref, ocmp_ref, g_ref, o_ref,
                       m_ref, l_ref, acc_ref, *, tq, tk, span):
    T = ks_ref.shape[1]
    q0 = pl.program_id(1) * tq
    q = q_ref[0].reshape(N_QHEADS * tq, LANES)
    qpos = q0 + lax.broadcasted_iota(jnp.int32, (tq, 1), 0)
    sel = sel_ref[0]
    _init_softmax_state(m_ref, l_ref, acc_ref)

    def key_tile(j, carry):
        k0 = pl.multiple_of(j * tk, tk)
        kpos = k0 + lax.broadcasted_iota(jnp.int32, (1, tk), 1)
        s = _nt_dot(q, ks_ref[0, pl.ds(k0, tk), :]).reshape(N_QHEADS, tq, tk)
        s = s + _selection_bias(sel, k0, tk, kpos <= qpos)
        _online_softmax_step(s, vs_ref[0, pl.ds(k0, tk), :], m_ref, l_ref, acc_ref, tq)
        return carry

    lax.fori_loop(0, (q0 + tq + tk - 1) // tk, key_tile, 0)
    o_slc = acc_ref[...] / l_ref[...]

    if span == T:
        start = 0
        kw, vw = kw_ref[0], vw_ref[0]
    else:
        start = pl.multiple_of(jnp.maximum(q0 - NSA_WINDOW, 0), LANES)
        kw, vw = kw_ref[0, pl.ds(start, span), :], vw_ref[0, pl.ds(start, span), :]
    kpos = start + lax.broadcasted_iota(jnp.int32, (1, span), 1)
    o_win, _ = _softmax_pv(_nt_dot(q, kw), _band_mask(qpos, kpos, NSA_WINDOW, 1), vw, tq)
    o_ref[0] = _gated_heads(g_ref[0], ocmp_ref[0], o_slc, o_win)


def _nsa_prompt(qr, ks, vs, kw, vw, sel, ocmp, gate_pre):
    B, _, T, _ = qr.shape
    n_cb = sel.shape[-1]
    tq = min(T, Q_BLOCK)
    tk = min(T, SLC_KEY_TILE)
    span = min(T, NSA_WINDOW + tq)
    seq = pl.BlockSpec((1, T, LANES), lambda b, i: (b, 0, 0))
    heads = pl.BlockSpec((1, N_QHEADS, tq, LANES), lambda b, i: (b, 0, i, 0))
    stat = pltpu.VMEM((N_QHEADS, tq, LANES), F32)
    return pl.pallas_call(
        functools.partial(_nsa_prompt_kernel, tq=tq, tk=tk, span=span),
        out_shape=jax.ShapeDtypeStruct((B, T, NSA_Q_W), F32),
        grid=(B, T // tq),
        in_specs=[heads, seq, seq, seq, seq,
                  pl.BlockSpec((1, NSA_KV_HEADS, tq, n_cb), lambda b, i: (b, 0, i, 0)),
                  heads,
                  pl.BlockSpec((1, tq, gate_pre.shape[2]), lambda b, i: (b, i, 0))],
        out_specs=pl.BlockSpec((1, tq, NSA_Q_W), lambda b, i: (b, i, 0)),
        scratch_shapes=[stat, stat, stat],
        compiler_params=pltpu.CompilerParams(
            dimension_semantics=("parallel", "arbitrary"), vmem_limit_bytes=VMEM_LIMIT_BYTES),
        name="nsa_prompt",
    )(qr, ks, vs, kw, vw, sel, ocmp, gate_pre)


def _nsa_paged_kernel(pt_ref, q_ref, *refs, tq, pos0):
    page_refs = refs[:PAGES_PER_STEP]
    new_ref, sel_ref, ocmp_ref, owin_ref, g_ref, o_ref, m_ref, l_ref, acc_ref = refs[PAGES_PER_STEP:]
    step = pl.program_id(1)
    q = q_ref[0].reshape(N_QHEADS * tq, LANES).astype(BF16)
    sel = sel_ref[0]

    @pl.when(step == 0)
    def _():
        _init_softmax_state(m_ref, l_ref, acc_ref)

    def attend(kt, vt, first_key, causal):
        n = kt.shape[1]
        s = jnp.dot(q, kt, preferred_element_type=F32).reshape(N_QHEADS, tq, n)
        if causal is None:
            s = s + _selection_bias(sel, first_key, n, None)
        else:
            s = jnp.where(causal[None], s, NEG_INF)
        _online_softmax_step(s, vt, m_ref, l_ref, acc_ref, tq, v_feature_major=True)

    attend(jnp.concatenate([r[0, :LANES, :].astype(BF16) for r in page_refs], axis=1),
           jnp.concatenate([r[0, LANES:, :].astype(BF16) for r in page_refs], axis=1),
           step * (PAGES_PER_STEP * PAGE_SIZE), None)

    @pl.when(step == pl.num_programs(1) - 1)
    def _():
        qpos = lax.broadcasted_iota(jnp.int32, (tq, 1), 0)
        kpos = lax.broadcasted_iota(jnp.int32, (1, PAGE_SIZE), 1)
        new = new_ref[0]
        attend(new[:LANES, :].astype(BF16), new[LANES:, :].astype(BF16), pos0, kpos <= qpos)
        o_ref[0] = _gated_heads(g_ref[0], ocmp_ref[0], acc_ref[...] / l_ref[...], owin_ref[0])


def _nsa_paged(page_table, qr, cache, new_tile, sel, ocmp, owin, gate_pre, pos0):
    B, _, T, _ = qr.shape
    n_pages = page_table.shape[1]
    n_cb = sel.shape[-1]
    W = cache.shape[1]
    heads = pl.BlockSpec((1, N_QHEADS, T, LANES), lambda b, s, pt: (b, 0, 0, 0))
    stat = pltpu.VMEM((N_QHEADS, T, LANES), F32)
    grid_spec = pltpu.PrefetchScalarGridSpec(
        num_scalar_prefetch=1,
        grid=(B, n_pages // PAGES_PER_STEP),
        in_specs=[heads] + _page_specs(W) + [
            pl.BlockSpec((1, W, PAGE_SIZE), lambda b, s, pt: (b, 0, 0)),
            pl.BlockSpec((1, NSA_KV_HEADS, T, n_cb), lambda b, s, pt: (b, 0, 0, 0)),
            heads, heads,
            pl.BlockSpec((1, T, gate_pre.shape[2]), lambda b, s, pt: (b, 0, 0))],
        out_specs=pl.BlockSpec((1, T, NSA_Q_W), lambda b, s, pt: (b, 0, 0)),
        scratch_shapes=[stat, stat, stat],
    )
    return pl.pallas_call(
        functools.partial(_nsa_paged_kernel, tq=T, pos0=pos0),
        out_shape=jax.ShapeDtypeStruct((B, T, NSA_Q_W), F32),
        grid_spec=grid_spec,
        compiler_params=pltpu.CompilerParams(
            dimension_semantics=("parallel", "arbitrary"), vmem_limit_bytes=VMEM_LIMIT_BYTES),
        name="nsa_paged",
    )(page_table, qr, *([cache] * PAGES_PER_STEP), new_tile, sel, ocmp, owin, gate_pre)


QPAD_W = N_QHEADS * LANES


def _rope_tables(pos):
    half = ROT_DIM // 2
    inv = jnp.exp(-math.log(ROPE_THETA) * jnp.arange(half, dtype=F32) * (2.0 / ROT_DIM))
    ang = pos.astype(F32)[:, None] * inv[None, :]
    cos, sin = jnp.cos(ang), jnp.sin(ang)
    T = pos.shape[0]
    zeros = lambda n: jnp.zeros((T, n), F32)
    cos_t = jnp.concatenate([cos, cos, jnp.ones((T, HALF - ROT_DIM), F32)], axis=-1)
    msin_t = jnp.concatenate([-sin, zeros(HALF - half)], axis=-1)
    psin_t = jnp.concatenate([zeros(half), sin, zeros(HALF - ROT_DIM)], axis=-1)
    twice = lambda a: jnp.concatenate([a, a], axis=-1)
    return twice(cos_t), twice(msin_t), twice(psin_t)


def _rope_lanes(x, cos_t, msin_t, psin_t):
    half = ROT_DIM // 2
    return x * cos_t + pltpu.roll(x, LANES - half, 1) * msin_t + pltpu.roll(x, half, 1) * psin_t


def _head_rms(x, ones_mat, gain):
    return x * lax.rsqrt(_head_mean_sq(x, ones_mat) + NORM_EPS) * gain


def _pad_head_columns(w):
    D = w.shape[0]
    w = w.reshape(D, N_QHEADS, HALF)
    z = jnp.zeros_like(w)
    lower = (jnp.arange(N_QHEADS) < N_QHEADS // 2)[None, :, None]
    return jnp.concatenate([jnp.where(lower, w, z), jnp.where(lower, z, w)], axis=-1).reshape(D, QPAD_W)


def _pad_head_vector(g):
    return _pad_head_columns(jnp.tile(g, N_QHEADS)[None, :])


def _ab_proj_kernel(x_ref, g_ref, sc_ref, sh_ref, w_ref, cos_ref, msin_ref, psin_ref, gq_ref, gks_ref, gkw_ref,
                    ones_ref, bd_ref, lng_ref, lnb_ref, wm_ref, bias_ref,
                    qn_ref, qr_ref, kvc_ref, kvs_ref, ks_ref, vs_ref, kvw_ref, kw_ref, vw_ref, gate_ref,
                    ob_ref, vln_ref, *, tm):
    h = _mod_norm(x_ref[0], g_ref[...], sc_ref[0], sh_ref[0])
    y = jnp.dot(h.astype(BF16), w_ref[...], preferred_element_type=F32)
    tabs = (cos_ref[...], msin_ref[...], psin_ref[...])
    for hd in range(N_QHEADS):
        lanes = slice(hd * LANES, (hd + 1) * LANES)
        qn = _head_rms(y[:, lanes], ones_ref[...], gq_ref[:, lanes])
        qn_ref[0, hd] = qn
        qr_ref[0, hd] = (_rope_lanes(qn, *tabs) * (HEAD_DIM ** -0.5)).astype(BF16)
    off = QPAD_W
    kvc_ref[0] = y[:, off:off + NSA_KV_W]
    off += NSA_KV_W
    for kv_ref, k_ref, v_ref, gain_ref in ((kvs_ref, ks_ref, vs_ref, gks_ref), (kvw_ref, kw_ref, vw_ref, gkw_ref)):
        k = _rope_lanes(_head_rms(y[:, off:off + LANES], bd_ref[...], gain_ref[...]), *tabs)
        v = y[:, off + LANES:off + NSA_KV_W]
        kv_ref[0, :, :LANES] = k
        kv_ref[0, :, LANES:] = v
        k_ref[0] = k.astype(BF16)
        v_ref[0] = v.astype(BF16)
        off += NSA_KV_W
    u = jax.nn.gelu(y[:, off:off + GM_WIDTH])
    v = jax.nn.gelu(y[:, off + GM_WIDTH:off + 2 * GM_WIDTH])
    off += 2 * GM_WIDTH
    gate_ref[0] = y[:, off:]
    mu = jnp.mean(v, axis=-1, keepdims=True)
    var = jnp.mean(jnp.square(v - mu), axis=-1, keepdims=True)
    vln = (v - mu) * lax.rsqrt(var + NORM_EPS) * lng_ref[...] + lnb_ref[...]
    vln_ref[0] = vln
    lane = lax.broadcasted_iota(jnp.int32, (1, LANES), 1)
    for c in range(tm // GM_CHUNK):
        rows = slice(c * GM_CHUNK, (c + 1) * GM_CHUNK)
        mixed = []
        for p in range(GM_GROUPS // 2):
            vp = vln[rows, p * LANES:(p + 1) * LANES]
            lo = jnp.where(lane < HALF, vp, 0.0).astype(BF16)
            hi = jnp.where(lane < HALF, 0.0, vp).astype(BF16)
            mixed.append(jnp.dot(wm_ref[2 * p], lo, preferred_element_type=F32)
                         + jnp.dot(wm_ref[2 * p + 1], hi, preferred_element_type=F32))
        ob_ref[0, rows, :] = u[rows] * (jnp.concatenate(mixed, axis=-1) + bias_ref[...])


def _ab_proj(x, norm_g, scale, shift, w_in, pos, qk_gain, ln_g, ln_b, wm, mix_bias):
    B, T, D = x.shape
    tm = _row_tile(T)
    g0, g1 = AB_SPLITS[0], AB_SPLITS[0] + AB_SPLITS[1]
    w = jnp.concatenate([_pad_head_columns(w_in[:, :g0]), w_in[:, g1:], w_in[:, g0:g1]], axis=1).astype(BF16)
    N = w.shape[1]
    cos_t, msin_t, psin_t = _rope_tables(pos)
    ones_mat = jnp.full((LANES, LANES), 1.0 / HALF, BF16)
    bd = _block_diag2(jnp.full((HALF, HALF), 1.0 / HALF, F32)).astype(BF16)
    two = lambda g: jnp.concatenate([g, g])[None, :]
    const = lambda a: pl.BlockSpec(a.shape, lambda b, i: (0,) * a.ndim)
    tok = lambda n: pl.BlockSpec((1, tm, n), lambda b, i: (b, i, 0))
    heads = pl.BlockSpec((1, N_QHEADS, tm, LANES), lambda b, i: (b, 0, i, 0))
    table = pl.BlockSpec((tm, LANES), lambda b, i: (i, 0))
    consts = [_pad_head_vector(qk_gain[0]), two(qk_gain[2]), two(qk_gain[3]), ones_mat, bd,
              ln_g[None, :], ln_b[None, :], wm, mix_bias]
    sds = lambda shape, dt: jax.ShapeDtypeStruct(shape, dt)
    return pl.pallas_call(
        functools.partial(_ab_proj_kernel, tm=tm),
        out_shape=[sds((B, N_QHEADS, T, LANES), F32), sds((B, N_QHEADS, T, LANES), BF16),
                   sds((B, T, NSA_KV_W), F32),
                   sds((B, T, NSA_KV_W), F32), sds((B, T, LANES), BF16), sds((B, T, LANES), BF16),
                   sds((B, T, NSA_KV_W), F32), sds((B, T, LANES), BF16), sds((B, T, LANES), BF16),
                   sds((B, T, 3 * NSA_HEADS), F32), sds((B, T, GM_WIDTH), F32), sds((B, T, GM_WIDTH), F32)],
        grid=(B, T // tm),
        in_specs=[tok(D), pl.BlockSpec((1, D), lambda b, i: (0, 0)), _mod_spec(scale, tm), _mod_spec(shift, tm),
                  pl.BlockSpec((D, N), lambda b, i: (0, 0)), table, table, table] + [const(a) for a in consts],
        out_specs=[heads, heads, tok(NSA_KV_W), tok(NSA_KV_W), tok(LANES), tok(LANES), tok(NSA_KV_W), tok(LANES),
                   tok(LANES), tok(3 * NSA_HEADS), tok(GM_WIDTH), tok(GM_WIDTH)],
        compiler_params=pltpu.CompilerParams(
            dimension_semantics=("parallel", "parallel"), vmem_limit_bytes=VMEM_LIMIT_BYTES),
        name="ab_proj",
    )(x, norm_g.reshape(1, D), scale, shift, w, cos_t, msin_t, psin_t, *consts)


def _chunk_mix_weights(ws, bs, rows_per_seq):
    n = min(rows_per_seq, GM_CHUNK)
    wm = jnp.where(jnp.tril(jnp.ones((n, n), bool)), ws[:, :n, :n], 0.0)
    if n < GM_CHUNK:
        eye = jnp.eye(GM_CHUNK // n, dtype=F32)
        wm = jnp.einsum('ab,gij->gaibj', eye, wm).reshape(GM_GROUPS, GM_CHUNK, GM_CHUNK)
    bias = jnp.tile(jnp.transpose(bs[:, :n]), (GM_CHUNK // n, 1))
    return wm.astype(BF16), jnp.repeat(bias, GM_GROUP_W, axis=1)


C_GROUP_W = QPAD_W + 2 * DIL_KV_W


def _c_proj_kernel(x_ref, g_ref, sc_ref, sh_ref, w_ref, cos_ref, msin_ref, psin_ref, gq_ref, gk_ref,
                   ones_ref, bd_ref, *refs, tm, dils):
    out_refs, stage = refs[:-1], refs[-1]
    h = _mod_norm(x_ref[0], g_ref[...], sc_ref[0], sh_ref[0])
    y = jnp.dot(h.astype(BF16), w_ref[...], preferred_element_type=F32)
    tabs = (cos_ref[...], msin_ref[...], psin_ref[...])

    def put(dst, val, d):
        if d == 1:
            dst[0] = val.astype(dst.dtype)
            return
        stage[...] = val
        for r in range(d):
            dst[r] = stage[pl.ds(r, tm // d, stride=d), :].astype(dst.dtype)

    for g, d in enumerate(dils):
        qd_ref, kd_ref, vd_ref, kv_ref = out_refs[4 * g:4 * g + 4]
        off = g * C_GROUP_W
        for hd in range(N_QHEADS):
            lanes = slice(off + hd * LANES, off + (hd + 1) * LANES)
            q = _rope_lanes(_head_rms(y[:, lanes], ones_ref[...], gq_ref[:, hd * LANES:(hd + 1) * LANES]), *tabs)
            put(qd_ref.at[:, hd], q * (HEAD_DIM ** -0.5), d)
        k = _rope_lanes(_head_rms(y[:, off + QPAD_W:off + QPAD_W + LANES], bd_ref[...], gk_ref[...]), *tabs)
        v = y[:, off + QPAD_W + LANES:off + C_GROUP_W]
        kv_ref[0, :, :LANES] = k
        kv_ref[0, :, LANES:] = v
        put(kd_ref, k, d)
        put(vd_ref, v, d)


def _c_proj(x, norm_g, scale, shift, w_in, pos, qk_gain, dils):
    B, T, D = x.shape
    tm = _row_tile(T)
    pieces = []
    for g in range(len(dils)):
        c0 = g * (DIL_Q_W + 2 * DIL_KV_W)
        pieces += [_pad_head_columns(w_in[:, c0:c0 + DIL_Q_W]), w_in[:, c0 + DIL_Q_W:c0 + DIL_Q_W + 2 * DIL_KV_W]]
    w = jnp.concatenate(pieces, axis=1).astype(BF16)
    N = w.shape[1]
    cos_t, msin_t, psin_t = _rope_tables(pos)
    ones_mat = jnp.full((LANES, LANES), 1.0 / HALF, BF16)
    bd = _block_diag2(jnp.full((HALF, HALF), 1.0 / HALF, F32)).astype(BF16)
    consts = [_pad_head_vector(qk_gain[0]), jnp.concatenate([qk_gain[1], qk_gain[1]])[None, :], ones_mat, bd]
    const = lambda a: pl.BlockSpec(a.shape, lambda b, i: (0,) * a.ndim)
    tok = lambda n: pl.BlockSpec((1, tm, n), lambda b, i: (b, i, 0))
    table = pl.BlockSpec((tm, LANES), lambda b, i: (i, 0))
    sds = lambda shape, dt: jax.ShapeDtypeStruct(shape, dt)
    out_shape, out_specs = [], []
    for d in dils:
        out_shape += [sds((B * d, N_QHEADS, T // d, LANES), BF16), sds((B * d, T // d, LANES), BF16),
                      sds((B * d, T // d, LANES), BF16), sds((B, T, 2 * DIL_KV_W), F32)]
        seq = pl.BlockSpec((d, tm // d, LANES), lambda b, i: (b, i, 0))
        out_specs += [pl.BlockSpec((d, N_QHEADS, tm // d, LANES), lambda b, i: (b, 0, i, 0)), seq, seq,
                      tok(2 * DIL_KV_W)]
    return pl.pallas_call(
        functools.partial(_c_proj_kernel, tm=tm, dils=tuple(dils)),
        out_shape=out_shape,
        grid=(B, T // tm),
        in_specs=[tok(D), pl.BlockSpec((1, D), lambda b, i: (0, 0)), _mod_spec(scale, tm), _mod_spec(shift, tm),
                  pl.BlockSpec((D, N), lambda b, i: (0, 0)), table, table, table] + [const(a) for a in consts],
        out_specs=out_specs,
        scratch_shapes=[pltpu.VMEM((tm, LANES), F32)],
        compiler_params=pltpu.CompilerParams(
            dimension_semantics=("parallel", "parallel"), vmem_limit_bytes=VMEM_LIMIT_BYTES),
        name="c_proj",
    )(x, norm_g.reshape(1, D), scale, shift, w, cos_t, msin_t, psin_t, *consts)


def _dil_merge_proj_kernel(*refs, tm, dils):
    o_refs = refs[:len(dils)]
    w_ref, x_ref, gate_ref, out_ref, stage = refs[len(dils):]
    lane = lax.broadcasted_iota(jnp.int32, (1, LANES), 1)
    merged = []
    for hd in range(N_QHEADS):
        lower = hd < N_QHEADS // 2
        vals = []
        for o_ref, d in zip(o_refs, dils):
            if d == 1:
                vals.append(o_ref[0, hd])
            else:
                for r in range(d):
                    stage[pl.ds(r, tm // d, stride=d), :] = o_ref[r, hd]
                vals.append(stage[...])
        lses = [v[:, HALF:HALF + 1] if lower else v[:, 0:1] for v in vals]
        top = jnp.maximum(jnp.maximum(lses[0], lses[1]), lses[2])
        ws = [jnp.exp(l - top) for l in lses]
        total = ws[0] + ws[1] + ws[2]
        merged.append((ws[0] * vals[0] + ws[1] * vals[1] + ws[2] * vals[2]) / total)
    pairs = []
    for c in range(N_QHEADS // 2):
        a, b = merged[2 * c], merged[2 * c + 1]
        if 2 * c < N_QHEADS // 2:
            pairs.append(jnp.where(lane < HALF, a, pltpu.roll(b, HALF, 1)))
        else:
            pairs.append(jnp.where(lane < HALF, pltpu.roll(a, HALF, 1), b))
    o = jnp.concatenate(pairs, axis=-1)
    y = jnp.dot(o.astype(BF16), w_ref[...], preferred_element_type=F32)
    out_ref[0] = x_ref[0] + gate_ref[0] * y


def _dil_merge_proj(outs, dils, w_out, x, gate):
    B, T, D = x.shape
    tm = _row_tile(T)
    in_specs = [pl.BlockSpec((d, N_QHEADS, tm // d, LANES), lambda b, i: (b, 0, i, 0)) for d in dils]
    in_specs += [pl.BlockSpec(w_out.shape, lambda b, i: (0, 0)),
                 pl.BlockSpec((1, tm, D), lambda b, i: (b, i, 0)), _mod_spec(gate, tm)]
    return pl.pallas_call(
        functools.partial(_dil_merge_proj_kernel, tm=tm, dils=tuple(dils)),
        out_shape=jax.ShapeDtypeStruct((B, T, D), F32),
        grid=(B, T // tm),
        in_specs=in_specs,
        out_specs=pl.BlockSpec((1, tm, D), lambda b, i: (b, i, 0)),
        scratch_shapes=[pltpu.VMEM((tm, LANES), F32)],
        compiler_params=pltpu.CompilerParams(
            dimension_semantics=("parallel", "parallel"), vmem_limit_bytes=VMEM_LIMIT_BYTES),
        name="dil_merge_proj",
    )(*outs, w_out, x, gate)


def _kv5(x):
    return x.reshape(x.shape[:2] + (2, NSA_KV_HEADS, HEAD_DIM))


def _mixer_ab(x, norm_g, sc, sh, gate_m, w_in, w_out, start, pasts, weights, page_table, seq_shape):
    qk_gain, pe, w_phi, ln_g, ln_b, ws, bs = weights
    B, T = seq_shape
    pos = start + jnp.arange(x.shape[1], dtype=jnp.int32) % T
    wm, mix_bias = _chunk_mix_weights(ws, bs, T)
    qn, qr, kvc, kvs, ks, vs, kvw, kw, vw, gate, o_b, vln = _ab_proj(
        x, norm_g, sc, sh, w_in, pos, qk_gain, ln_g, ln_b, wm, mix_bias)
    if pasts is None:
        kc, vc = _cmp_finish(_block_mean(kvc), pe, w_phi, qk_gain[1])
        ocmp, sel = _cmp_select(qn, kc, vc, start)
        o_a = _nsa_prompt(qr, ks, vs, kw, vw, sel, ocmp, gate)
        kvw_all = kvw
    else:
        per_seq = lambda a: a.reshape((B, T) + a.shape[2:])
        heads = lambda a: a.reshape(N_QHEADS, B, T, LANES).transpose(1, 0, 2, 3)
        kvc, kvs, kvw, gate, vln = (per_seq(a) for a in (kvc, kvs, kvw, gate, vln))
        qn, qr = heads(qn), heads(qr.astype(F32))
        cache_cmp, cache_slc, win_past = pasts
        means = jnp.swapaxes(_page_block_mean(cache_cmp, page_table), 1, 2)
        kc, vc = _cmp_finish(means, pe, w_phi, qk_gain[1])
        ocmp, sel = _cmp_select(qn, kc, vc, start)
        kvw_all = jnp.concatenate([win_past, kvw], axis=1)
        owin = _band_tail(qr, kvw_all, NSA_WINDOW, 1, False)
        new_tile = jnp.swapaxes(jnp.pad(kvs, ((0, 0), (0, PAGE_SIZE - T), (0, 0))), 1, 2)
        o_a = _nsa_paged(page_table, qr, cache_slc, new_tile, sel, ocmp, owin, gate, start)
    keep_w = min(NSA_WINDOW, kvw_all.shape[1])
    n_cur = (T - 1) % GM_CHUNK + 1
    x_new = _proj_residual([o_a.reshape(o_b.shape), o_b], (w_out[:NSA_Q_W], w_out[NSA_Q_W:]), x, gate_m)
    return x_new, (_kv5(kvc), _kv5(kvs), _kv5(kvw_all[:, kvw_all.shape[1] - keep_w:]), vln[:, T - n_cur:])


def _mixer_c(x, norm_g, sc, sh, gate_m, w_in, w_out, start, pasts, qk_gain, seq_shape):
    B, T = seq_shape
    pos = start + jnp.arange(x.shape[1], dtype=jnp.int32) % T
    dils = tuple(d for _, d in DIL_CFG) if pasts is None else (1,) * len(DIL_CFG)
    res = _c_proj(x, norm_g, sc, sh, w_in, pos, qk_gain, dils)
    outs, new_bufs = [], []
    for g, (window, dilation) in enumerate(DIL_CFG):
        qd, kd, vd, kv_new = res[4 * g:4 * g + 4]
        if pasts is None:
            outs.append(_band_self(qd, kd, vd, window // dilation, True))
            kv_all = kv_new
        else:
            kv_all = jnp.concatenate([pasts[g], kv_new.reshape(B, T, 2 * DIL_KV_W)], axis=1)
            q = qd.reshape(N_QHEADS, B, T, LANES).transpose(1, 0, 2, 3).astype(F32)
            o = _band_tail(q, kv_all, window, dilation, True)
            outs.append(o.transpose(1, 0, 2, 3).reshape(1, N_QHEADS, B * T, LANES))
        keep = min(window, kv_all.shape[1])
        new_bufs.append(kv_all[:, kv_all.shape[1] - keep:].reshape(B, keep, 2, DIL_KV_HEADS, HEAD_DIM))
    return _dil_merge_proj(outs, dils, w_out, x, gate_m), tuple(new_bufs)


def _adaln_kernel(c_ref, w_ref, b_ref, o_ref):
    act = jax.nn.silu(c_ref[...]).astype(BF16)
    o_ref[...] = jnp.dot(act, w_ref[0].astype(BF16), preferred_element_type=F32) + b_ref[0]


def _adaln(c, layer, w, b):
    B, D = c.shape
    depth, _, N = w.shape
    tn = D
    mod = pl.pallas_call(
        _adaln_kernel,
        out_shape=jax.ShapeDtypeStruct((B, N), F32),
        grid=(N // tn,),
        in_specs=[pl.BlockSpec((B, D), lambda j: (0, 0)), pl.BlockSpec((1, D, tn), lambda j: (layer, 0, j)),
                  pl.BlockSpec((1, 1, tn), lambda j: (layer, 0, j))],
        out_specs=pl.BlockSpec((B, tn), lambda j: (0, j)),
        compiler_params=pltpu.CompilerParams(dimension_semantics=("parallel",)),
        name="adaln",
    )(c, w, b.reshape(depth, 1, N))
    return [mod[:, None, j * D:(j + 1) * D] for j in range(N // D)]


def _expand_mod(m, t):
    B, _, D = m.shape
    return jnp.broadcast_to(m, (B, t, D)).reshape(1, B * t, D)


def _slot_buffer(n_tok, d_model):
    n_tiles = -(-(n_tok * TOP_K + N_EXPERTS * (MOE_TILE - 1)) // MOE_TILE)
    return jnp.zeros((n_tiles * MOE_TILE, d_model), F32)


def _layer(groups, mixers, norm_m, norm_f, moe_w, x_sorted):
    w_router, b_router, *ffn_w = moe_w
    xs, states, routed = [], [], []
    counts = jnp.zeros((1, N_EXPERTS), F32)
    for (x, mods), mixer in zip(groups, mixers):
        sh_m, sc_m, g_m, sh_f, sc_f, g_f = mods
        B, T, D = x.shape
        x, state = mixer(x, norm_m, sc_m, sh_m, g_m)
        h, route_i, route_g, counts = _mod_norm_router(x, norm_f, sc_f, sh_f, w_router, b_router, counts)
        xs.append(x)
        states.append(state)
        routed.append((h, route_i, route_g))
    n_tiles = x_sorted.shape[0] // MOE_TILE
    pad_start, pad_end, n_used = _moe_plan(counts[0])
    dests = []
    for h, route_i, _ in routed:
        expert = route_i[..., :TOP_K]
        rank = route_i[..., ROUTE_RANK_LANE:ROUTE_RANK_LANE + TOP_K]
        first = jnp.sum(jnp.where(expert[..., None] == jnp.arange(N_EXPERTS), pad_start, 0), axis=-1)
        dests.append((first + rank).astype(jnp.int32))
        x_sorted = _moe_dispatch(dests[-1], h, x_sorted)
    y_sorted = _moe_ffn(*_moe_tiles(pad_end, n_used, n_tiles), x_sorted, *ffn_w)
    outs = [_moe_combine(dest, y_sorted, route_g, x, mods[5])
            for dest, (_, _, route_g), x, (_, mods) in zip(dests, routed, xs, groups)]
    return outs, states, x_sorted


def _stack(states, j):
    return jnp.stack([s[j] for s in states], axis=0)


def kernel(x_prompt, x_sample, cache_cmp_kv, cache_slc_kv, state_win_kv, state_dil0_kv, state_dil1_kv,
           state_dil2_kv, page_table, c_prompt, c_sample, norm_mix, norm_ffn, w_ada, b_ada, w_in_ab, w_out_ab,
           nsa_qk_gain, nsa_pe, nsa_w_phi, gm_ln_g, gm_ln_b, gm_ws, gm_bs, w_in_c, w_out_c, dil_qk_gain,
           w_router, b_router, w_up, b_up, w_down, b_down):
    depth = norm_mix.shape[0]
    past_len = page_table.shape[1] * PAGE_SIZE
    Bs, Ts, D = x_sample.shape
    x_p = x_prompt
    x_s = x_sample.reshape(1, Bs * Ts, D)
    ab_p, ab_s, dil_p, dil_s = [], [], [], []
    slots = _slot_buffer(x_prompt.shape[0] * x_prompt.shape[1] + Bs * Ts, D)
    for layer in range(depth):
        i = layer // 2
        moe_w = (w_router[layer], b_router[layer], layer, w_up, b_up, w_down, b_down)
        mods_p = _adaln(c_prompt, layer, w_ada, b_ada)
        mods_s = [_expand_mod(m, Ts) for m in _adaln(c_sample, layer, w_ada, b_ada)]
        groups = [(x_p, mods_p), (x_s, mods_s)]
        if layer % 2 == 0:
            wts = (nsa_qk_gain[i], nsa_pe[i], nsa_w_phi[i], gm_ln_g[i], gm_ln_b[i], gm_ws[i], gm_bs[i])
            flat_kv = lambda a: a.reshape(a.shape[:2] + (NSA_KV_W,))
            pasts_s = (_feature_major_pages(cache_cmp_kv[i]), _feature_major_pages(cache_slc_kv[i]),
                       flat_kv(state_win_kv[i]))

            w_io = (w_in_ab[i], w_out_ab[i].astype(BF16))

            def mix_p(x, g, sc, sh, gm, wts=wts, w_io=w_io):
                return _mixer_ab(x, g, sc, sh, gm, *w_io, 0, None, wts, None, x.shape[:2])

            def mix_s(x, g, sc, sh, gm, wts=wts, w_io=w_io, pasts_s=pasts_s):
                return _mixer_ab(x, g, sc, sh, gm, *w_io, past_len, pasts_s, wts, page_table, (Bs, Ts))

            (x_p, x_s), (st_p, st_s), slots = _layer(
                groups, (mix_p, mix_s), norm_mix[layer], norm_ffn[layer], moe_w, slots)
            ab_p.append(st_p)
            ab_s.append(st_s)
        else:
            flat_kv = lambda a: a.reshape(a.shape[:2] + (2 * DIL_KV_W,))
            pasts_s = (flat_kv(state_dil0_kv[i]), flat_kv(state_dil1_kv[i]), flat_kv(state_dil2_kv[i]))

            w_io = (w_in_c[i], w_out_c[i].astype(BF16))

            def mix_p(x, g, sc, sh, gm, gain=dil_qk_gain[i], w_io=w_io):
                return _mixer_c(x, g, sc, sh, gm, *w_io, 0, None, gain, x.shape[:2])

            def mix_s(x, g, sc, sh, gm, pasts_s=pasts_s, gain=dil_qk_gain[i], w_io=w_io):
                return _mixer_c(x, g, sc, sh, gm, *w_io, past_len, pasts_s, gain, (Bs, Ts))

            (x_p, x_s), (st_p, st_s), slots = _layer(
                groups, (mix_p, mix_s), norm_mix[layer], norm_ffn[layer], moe_w, slots)
            dil_p.append(st_p)
            dil_s.append(st_s)
    x_s = x_s.reshape(Bs, Ts, D)
    cmp_p, cmp_s = _stack(ab_p, 0), _stack(ab_s, 0)
    slc_p, slc_s = _stack(ab_p, 1), _stack(ab_s, 1)
    win_p, win_s = _stack(ab_p, 2), _stack(ab_s, 2)
    gmv_p, gmv_s = _stack(ab_p, 3), _stack(ab_s, 3)
    d0_p, d0_s = _stack(dil_p, 0), _stack(dil_s, 0)
    d1_p, d1_s = _stack(dil_p, 1), _stack(dil_s, 1)
    d2_p, d2_s = _stack(dil_p, 2), _stack(dil_s, 2)
    return (x_p, x_s, cmp_p, cmp_s, slc_p, slc_s, win_p, win_s, gmv_p, gmv_s, d0_p, d0_s, d1_p, d1_s, d2_p, d2_s)
```

```python
import functools
import math

import jax
import jax.numpy as jnp
from jax import lax
from jax.experimental import pallas as pl
from jax.experimental.pallas import tpu as pltpu

F32 = jnp.float32
BF16 = jnp.bfloat16

D_MODEL = 1024
HEAD_DIM = 64
ROT_DIM = HEAD_DIM // 4
ROPE_THETA = 500000.0
NORM_EPS = 1e-6
NEG_INF = -1e30
Q_BLOCK = 128
ATTN_Q_TILE = 256
PAGE_SIZE = 128

NSA_HEADS = D_MODEL // (2 * HEAD_DIM)
NSA_KV_HEADS = 2
NSA_GROUP = NSA_HEADS // NSA_KV_HEADS
NSA_BLOCK = 64
NSA_N_SEL = 16
NSA_WINDOW = 512
NSA_Q_W = NSA_HEADS * HEAD_DIM
NSA_KV_W = 2 * NSA_KV_HEADS * HEAD_DIM

GM_GROUPS = 8
GM_WIDTH = D_MODEL // 2
GM_GROUP_W = GM_WIDTH // GM_GROUPS
GM_CHUNK = 128

AB_SPLITS = (NSA_Q_W, 3 * NSA_HEADS, NSA_KV_W, NSA_KV_W, NSA_KV_W, GM_WIDTH, GM_WIDTH)

DIL_CFG = ((128, 1), (512, 4), (2048, 16))
DIL_HEADS = 8
DIL_KV_HEADS = 2
DIL_Q_W = DIL_HEADS * HEAD_DIM
DIL_KV_W = DIL_KV_HEADS * HEAD_DIM

N_EXPERTS = 32
TOP_K = 4
D_FF = D_MODEL
SWIGLU_ALPHA = 1.702
SWIGLU_LIMIT = 7.0

VMEM_LIMIT_BYTES = 56 * 1024 * 1024
ROW_TILE = 512
MOE_TILE = 512


def _row_tile(t):
    return t if t <= ROW_TILE else ROW_TILE


def _mod_spec(mod, tm):
    if mod.shape[1] == 1:
        return pl.BlockSpec((1, 1, mod.shape[2]), lambda b, i: (b, 0, 0))
    return pl.BlockSpec((1, tm, mod.shape[2]), lambda b, i: (b, i, 0))


def _split_bf16(v):
    hi = v.astype(BF16)
    lo = (v - hi.astype(F32)).astype(BF16)
    return hi, lo


def _mod_norm(x, g, sc, sh):
    h = x * lax.rsqrt(jnp.mean(x * x, axis=-1, keepdims=True) + NORM_EPS) * g
    return h * (1.0 + sc) + sh


def _proj_residual_kernel(*refs, n_parts):
    a_refs = refs[:n_parts]
    w_refs = refs[n_parts:2 * n_parts]
    x_ref, g_ref, o_ref = refs[2 * n_parts:]
    y = None
    for a_ref, w_ref in zip(a_refs, w_refs):
        p = jnp.dot(a_ref[0].astype(BF16), w_ref[...], preferred_element_type=F32)
        y = p if y is None else y + p
    o_ref[0] = x_ref[0] + g_ref[0] * y


def _proj_residual(a_parts, w_parts_bf16, x, gate):
    B, T, D = x.shape
    tm = _row_tile(T)
    n_parts = len(a_parts)
    in_specs = [pl.BlockSpec((1, tm, a.shape[2]), lambda b, i: (b, i, 0)) for a in a_parts]
    in_specs += [pl.BlockSpec(w.shape, lambda b, i: (0, 0)) for w in w_parts_bf16]
    in_specs += [pl.BlockSpec((1, tm, D), lambda b, i: (b, i, 0)), _mod_spec(gate, tm)]
    return pl.pallas_call(
        functools.partial(_proj_residual_kernel, n_parts=n_parts),
        out_shape=jax.ShapeDtypeStruct((B, T, D), F32),
        grid=(B, T // tm),
        in_specs=in_specs,
        out_specs=pl.BlockSpec((1, tm, D), lambda b, i: (b, i, 0)),
        compiler_params=pltpu.CompilerParams(
            dimension_semantics=("parallel", "parallel"), vmem_limit_bytes=VMEM_LIMIT_BYTES),
        name="proj_residual",
    )(*a_parts, *w_parts_bf16, x, gate)


ROUTE_RANK_LANE = TOP_K


def _mod_norm_router_kernel(x_ref, g_ref, sc_ref, sh_ref, whi_ref, wlo_ref, b_ref, cnt_in_ref,
                            h_ref, ri_ref, rg_ref, cnt_ref, *, tm):
    @pl.when((pl.program_id(0) == 0) & (pl.program_id(1) == 0))
    def _():
        cnt_ref[...] = cnt_in_ref[...]

    h = _mod_norm(x_ref[0], g_ref[...], sc_ref[0], sh_ref[0])
    h_ref[0] = h
    h_hi, h_lo = _split_bf16(h)
    w_hi = whi_ref[...]
    logits = (jnp.dot(h_hi, w_hi, preferred_element_type=F32)
              + jnp.dot(h_lo, w_hi, preferred_element_type=F32)
              + jnp.dot(h_hi, wlo_ref[...], preferred_element_type=F32)) + b_ref[...]
    n_exp = logits.shape[-1]
    lane = lax.broadcasted_iota(jnp.int32, (1, n_exp), 1)
    work = logits
    vals, ids, hits = [], [], []
    for _ in range(TOP_K):
        m = jnp.max(work, axis=-1, keepdims=True)
        idx = jnp.min(jnp.where(work == m, lane, n_exp), axis=-1, keepdims=True)
        hit = lane == idx
        vals.append(m)
        ids.append(idx)
        hits.append(hit)
        work = jnp.where(hit, NEG_INF, work)
    exps = [jnp.exp(v - vals[0]) for v in vals]
    denom = exps[0] + exps[1] + exps[2] + exps[3]
    onehot = jnp.zeros(logits.shape, F32)
    for hit in hits:
        onehot = onehot + jnp.where(hit, 1.0, 0.0)
    row = lax.broadcasted_iota(jnp.int32, (tm, tm), 0)
    col = lax.broadcasted_iota(jnp.int32, (tm, tm), 1)
    earlier = jnp.where(row > col, 1.0, 0.0).astype(BF16)
    before = jnp.dot(earlier, onehot.astype(BF16), preferred_element_type=F32) + cnt_ref[...]
    cnt_ref[...] = cnt_ref[...] + jnp.sum(onehot, axis=0, keepdims=True)
    out_lane = lax.broadcasted_iota(jnp.int32, (1, LANES), 1)
    ri = jnp.zeros((tm, LANES), jnp.int32)
    rg = jnp.zeros((tm, LANES), F32)
    for k in range(TOP_K):
        rank = jnp.sum(jnp.where(hits[k], before, 0.0), axis=-1, keepdims=True).astype(jnp.int32)
        ri = jnp.where(out_lane == k, ids[k], ri)
        ri = jnp.where(out_lane == ROUTE_RANK_LANE + k, rank, ri)
        rg = jnp.where(out_lane == k, exps[k] / denom, rg)
    ri_ref[0] = ri
    rg_ref[0] = rg


def _mod_norm_router(x, norm_g, scale, shift, w_router, b_router, counts_in):
    B, T, D = x.shape
    tm = _row_tile(T)
    E = w_router.shape[1]
    w_hi, w_lo = _split_bf16(w_router)
    tok = lambda n: pl.BlockSpec((1, tm, n), lambda b, i: (b, i, 0))
    return pl.pallas_call(
        functools.partial(_mod_norm_router_kernel, tm=tm),
        out_shape=[jax.ShapeDtypeStruct((B, T, D), F32), jax.ShapeDtypeStruct((B, T, LANES), jnp.int32),
                   jax.ShapeDtypeStruct((B, T, LANES), F32), jax.ShapeDtypeStruct((1, E), F32)],
        grid=(B, T // tm),
        in_specs=[
            tok(D),
            pl.BlockSpec((1, D), lambda b, i: (0, 0)),
            _mod_spec(scale, tm),
            _mod_spec(shift, tm),
            pl.BlockSpec((D, E), lambda b, i: (0, 0)),
            pl.BlockSpec((D, E), lambda b, i: (0, 0)),
            pl.BlockSpec((1, E), lambda b, i: (0, 0)),
            pl.BlockSpec((1, E), lambda b, i: (0, 0)),
        ],
        out_specs=[tok(D), tok(LANES), tok(LANES), pl.BlockSpec((1, E), lambda b, i: (0, 0))],
        compiler_params=pltpu.CompilerParams(
            dimension_semantics=("arbitrary", "arbitrary"), vmem_limit_bytes=VMEM_LIMIT_BYTES),
        name="mod_norm_router",
    )(x, norm_g.reshape(1, D), scale, shift, w_hi, w_lo, b_router.reshape(1, E), counts_in)


def _moe_ffn_kernel(te_ref, tx_ref, tv_ref, x_ref, wu_ref, bu_ref, wd_ref, bd_ref, o_ref):
    i = pl.program_id(0)

    @pl.when(tv_ref[i] != 0)
    def _():
        hu = jnp.dot(x_ref[...].astype(BF16), wu_ref[0, 0].astype(BF16), preferred_element_type=F32) + bu_ref[0, 0]
        glu = jnp.minimum(hu[:, :D_FF], SWIGLU_LIMIT)
        lin = jnp.clip(hu[:, D_FF:], -SWIGLU_LIMIT, SWIGLU_LIMIT)
        act = glu * jax.nn.sigmoid(SWIGLU_ALPHA * glu) * (lin + 1.0)
        o_ref[...] = (jnp.dot(act.astype(BF16), wd_ref[0, 0].astype(BF16), preferred_element_type=F32)
                      + bd_ref[0, 0])

    @pl.when(tv_ref[i] == 0)
    def _():
        o_ref[...] = jnp.zeros_like(o_ref)


def _moe_ffn(tile_exp, tile_x, tile_valid, x_sorted, layer, w_up, b_up, w_down, b_down):
    n_slots, D = x_sorted.shape
    n_tiles = n_slots // MOE_TILE
    depth, E, _, F2 = w_up.shape
    expert = lambda i, te, tx, tv: (layer, te[i], 0, 0)
    grid_spec = pltpu.PrefetchScalarGridSpec(
        num_scalar_prefetch=3,
        grid=(n_tiles,),
        in_specs=[
            pl.BlockSpec((MOE_TILE, D), lambda i, te, tx, tv: (tx[i], 0)),
            pl.BlockSpec((1, 1, D, F2), expert),
            pl.BlockSpec((1, 1, 1, F2), expert),
            pl.BlockSpec((1, 1, F2 // 2, D), expert),
            pl.BlockSpec((1, 1, 1, D), expert),
        ],
        out_specs=pl.BlockSpec((MOE_TILE, D), lambda i, te, tx, tv: (i, 0)),
    )
    return pl.pallas_call(
        _moe_ffn_kernel,
        out_shape=jax.ShapeDtypeStruct((n_slots, D), F32),
        grid_spec=grid_spec,
        compiler_params=pltpu.CompilerParams(
            dimension_semantics=("arbitrary",), vmem_limit_bytes=VMEM_LIMIT_BYTES),
        name="moe_ffn",
    )(tile_exp, tile_x, tile_valid, x_sorted, w_up, b_up.reshape(depth, E, 1, F2), w_down,
      b_down.reshape(depth, E, 1, D))


DISPATCH_TILE = 512
COMBINE_TILE = 512


def _row_copy(src_ref, src_row, dst_ref, dst_row, sem):
    return pltpu.make_async_copy(src_ref.at[pl.ds(src_row, 1)], dst_ref.at[pl.ds(dst_row, 1)], sem)


def _moe_dispatch_kernel(dest_ref, h_ref, xs_in_ref, xs_ref, sem, *, tm):
    del xs_in_ref
    h_rows = h_ref.at[0]

    def issue(t, carry):
        for k in range(TOP_K):
            _row_copy(h_rows, t, xs_ref, dest_ref[0, 0, 0, t * TOP_K + k], sem).start()
        return carry

    lax.fori_loop(0, tm, issue, 0, unroll=4)
    for k in range(TOP_K):
        pltpu.make_async_copy(h_rows, xs_ref.at[pl.ds(0, tm)], sem).wait()


def _dest_spec(tm):
    return pl.BlockSpec((1, 1, 1, tm * TOP_K), lambda b, i: (b, i, 0, 0), memory_space=pltpu.SMEM)


def _moe_dispatch(dest, h, x_sorted):
    B, T, D = h.shape
    tm = min(T, DISPATCH_TILE)
    return pl.pallas_call(
        functools.partial(_moe_dispatch_kernel, tm=tm),
        out_shape=jax.ShapeDtypeStruct(x_sorted.shape, x_sorted.dtype),
        grid=(B, T // tm),
        in_specs=[_dest_spec(tm),
                  pl.BlockSpec((1, tm, D), lambda b, i: (b, i, 0)),
                  pl.BlockSpec(memory_space=pl.ANY)],
        out_specs=pl.BlockSpec(memory_space=pl.ANY),
        scratch_shapes=[pltpu.SemaphoreType.DMA(())],
        input_output_aliases={2: 0},
        compiler_params=pltpu.CompilerParams(dimension_semantics=("arbitrary", "arbitrary")),
        name="moe_dispatch",
    )(dest.reshape(B, T // tm, 1, tm * TOP_K), h, x_sorted)


def _moe_combine_kernel(dest_ref, ys_ref, rg_ref, x_ref, gf_ref, o_ref, buf, sem, *, tm):
    def issue(t, carry):
        for k in range(TOP_K):
            _row_copy(ys_ref, dest_ref[0, 0, 0, t * TOP_K + k], buf.at[k], t, sem).start()
        return carry

    lax.fori_loop(0, tm, issue, 0, unroll=4)
    for k in range(TOP_K):
        pltpu.make_async_copy(ys_ref.at[pl.ds(0, tm)], buf.at[k], sem).wait()
    gates = rg_ref[0]
    y = gates[:, 0:1] * buf[0]
    for k in range(1, TOP_K):
        y = y + gates[:, k:k + 1] * buf[k]
    o_ref[0] = x_ref[0] + gf_ref[0] * y


def _moe_combine(dest, y_sorted, route_g, x, gate_f):
    B, T, D = x.shape
    tm = min(T, COMBINE_TILE)
    tok = lambda n: pl.BlockSpec((1, tm, n), lambda b, i: (b, i, 0))
    return pl.pallas_call(
        functools.partial(_moe_combine_kernel, tm=tm),
        out_shape=jax.ShapeDtypeStruct((B, T, D), F32),
        grid=(B, T // tm),
        in_specs=[_dest_spec(tm), pl.BlockSpec(memory_space=pl.ANY), tok(LANES), tok(D), _mod_spec(gate_f, tm)],
        out_specs=tok(D),
        scratch_shapes=[pltpu.VMEM((TOP_K, tm, D), F32), pltpu.SemaphoreType.DMA(())],
        compiler_params=pltpu.CompilerParams(
            dimension_semantics=("parallel", "parallel"), vmem_limit_bytes=VMEM_LIMIT_BYTES),
        name="moe_combine",
    )(dest.reshape(B, T // tm, 1, tm * TOP_K), y_sorted, route_g, x, gate_f)


def _moe_plan(counts):
    counts = counts.astype(jnp.int32)
    padded = (counts + MOE_TILE - 1) // MOE_TILE * MOE_TILE
    pad_end = jnp.cumsum(padded)
    pad_start = pad_end - padded
    n_used = pad_end[-1] // MOE_TILE
    return pad_start, pad_end, n_used


def _moe_tiles(pad_end, n_used, n_tiles):
    tile = jnp.arange(n_tiles, dtype=jnp.int32)
    tile_valid = (tile < n_used).astype(jnp.int32)
    tile_x = jnp.minimum(tile, n_used - 1)
    tile_exp = jnp.sum((tile_x[:, None] * MOE_TILE >= pad_end[None, :]).astype(jnp.int32), axis=1)
    return jnp.minimum(tile_exp, N_EXPERTS - 1), tile_x, tile_valid


N_QHEADS = 8
LANES = 128
HALF = HEAD_DIM


def _nt_dot(a, b):
    return lax.dot_general(a, b, (((1,), (1,)), ((), ())), preferred_element_type=F32)


def _band_mask(qpos, kpos, window, dilation):
    delta = qpos - kpos
    valid = (delta >= 0) & (delta <= window)
    if dilation > 1:
        valid = valid & ((delta & (dilation - 1)) == 0)
    return valid


def _softmax_pv(s, valid, v, tq):
    n = s.shape[-1]
    s = jnp.where(valid[None], s.reshape(N_QHEADS, tq, n), NEG_INF)
    m = jnp.max(s, axis=-1, keepdims=True)
    p = jnp.exp(s - m)
    l = jnp.sum(p, axis=-1, keepdims=True)
    o = jnp.dot(p.reshape(N_QHEADS * tq, n).astype(BF16), v, preferred_element_type=F32)
    return o.reshape(N_QHEADS, tq, LANES) / l, m + jnp.log(l)


def _store_heads(o_ref, o, lse):
    if lse is None:
        o_ref[0] = o
        return
    lane = lax.broadcasted_iota(jnp.int32, (1, LANES), 1)
    for h in range(N_QHEADS):
        own = (lane < HALF) if h < N_QHEADS // 2 else (lane >= HALF)
        o_ref[0, h] = jnp.where(own, o[h], lse[h])


def _band_self_kernel(q_ref, k_ref, v_ref, o_ref, *, window, tq, span, want_lse):
    T = k_ref.shape[1]
    q0 = pl.program_id(1) * tq
    if span == T:
        start = 0
        k = k_ref[0]
        v = v_ref[0]
    else:
        start = pl.multiple_of(jnp.maximum(q0 - window, 0), LANES)
        k = k_ref[0, pl.ds(start, span), :]
        v = v_ref[0, pl.ds(start, span), :]
    q = q_ref[0].reshape(N_QHEADS * tq, LANES).astype(BF16)
    qpos = q0 + lax.broadcasted_iota(jnp.int32, (tq, 1), 0)
    kpos = start + lax.broadcasted_iota(jnp.int32, (1, span), 1)
    o, lse = _softmax_pv(_nt_dot(q, k), _band_mask(qpos, kpos, window, 1), v, tq)
    _store_heads(o_ref, o, lse if want_lse else None)


def _band_self(q, k, v, window, want_lse):
    B, _, T, _ = q.shape
    tq = min(T, ATTN_Q_TILE)
    span = min(T, window + tq)
    return pl.pallas_call(
        functools.partial(_band_self_kernel, window=window, tq=tq, span=span, want_lse=want_lse),
        out_shape=jax.ShapeDtypeStruct((B, N_QHEADS, T, LANES), F32),
        grid=(B, T // tq),
        in_specs=[
            pl.BlockSpec((1, N_QHEADS, tq, LANES), lambda b, i: (b, 0, i, 0)),
            pl.BlockSpec((1, T, LANES), lambda b, i: (b, 0, 0)),
            pl.BlockSpec((1, T, LANES), lambda b, i: (b, 0, 0)),
        ],
        out_specs=pl.BlockSpec((1, N_QHEADS, tq, LANES), lambda b, i: (b, 0, i, 0)),
        compiler_params=pltpu.CompilerParams(
            dimension_semantics=("parallel", "parallel"), vmem_limit_bytes=VMEM_LIMIT_BYTES),
        name="band_self",
    )(q, k, v)


def _band_tail_kernel(q_ref, kv_ref, o_ref, *, window, dilation, tq, want_lse):
    L = kv_ref.shape[1]
    kv = kv_ref[0]
    k = kv[:, :LANES].astype(BF16)
    v = kv[:, LANES:].astype(BF16)
    q = q_ref[0].reshape(N_QHEADS * tq, LANES).astype(BF16)
    qpos = (L - tq) + lax.broadcasted_iota(jnp.int32, (tq, 1), 0)
    kpos = lax.broadcasted_iota(jnp.int32, (1, L), 1)
    o, lse = _softmax_pv(_nt_dot(q, k), _band_mask(qpos, kpos, window, dilation), v, tq)
    _store_heads(o_ref, o, lse if want_lse else None)


def _band_tail(q, kv_all, window, dilation, want_lse):
    B, _, T, _ = q.shape
    L = kv_all.shape[1]
    return pl.pallas_call(
        functools.partial(_band_tail_kernel, window=window, dilation=dilation, tq=T, want_lse=want_lse),
        out_shape=jax.ShapeDtypeStruct((B, N_QHEADS, T, LANES), F32),
        grid=(B,),
        in_specs=[
            pl.BlockSpec((1, N_QHEADS, T, LANES), lambda b: (b, 0, 0, 0)),
            pl.BlockSpec((1, L, 2 * LANES), lambda b: (b, 0, 0)),
        ],
        out_specs=pl.BlockSpec((1, N_QHEADS, T, LANES), lambda b: (b, 0, 0, 0)),
        compiler_params=pltpu.CompilerParams(
            dimension_semantics=("parallel",), vmem_limit_bytes=VMEM_LIMIT_BYTES),
        name="band_tail",
    )(q, kv_all)


def _block_mean_kernel(x_ref, o_ref):
    rows = x_ref.shape[1]
    x = x_ref[0].reshape(rows // NSA_BLOCK, NSA_BLOCK, x_ref.shape[2])
    o_ref[0] = jnp.sum(x, axis=1) * (1.0 / NSA_BLOCK)


def _block_mean(kvc):
    B, T, W = kvc.shape
    tm = _row_tile(T)
    return pl.pallas_call(
        _block_mean_kernel,
        out_shape=jax.ShapeDtypeStruct((B, T // NSA_BLOCK, W), F32),
        grid=(B, T // tm),
        in_specs=[pl.BlockSpec((1, tm, W), lambda b, i: (b, i, 0))],
        out_specs=pl.BlockSpec((1, tm // NSA_BLOCK, W), lambda b, i: (b, i, 0)),
        compiler_params=pltpu.CompilerParams(dimension_semantics=("parallel", "parallel")),
        name="block_mean",
    )(kvc)


PAGES_PER_STEP = 16


def _feature_major_pages(cache):
    n_phys, rows = cache.shape[:2]
    return cache.transpose(0, 2, 3, 4, 1).reshape(n_phys, -1, rows)


def _page_specs(n_feat):
    return [pl.BlockSpec((1, n_feat, PAGE_SIZE), functools.partial(
        lambda b, s, pt, pg: (pt[b, s * PAGES_PER_STEP + pg], 0, 0), pg=pg)) for pg in range(PAGES_PER_STEP)]


def _page_block_mean_kernel(pt_ref, *refs):
    page_refs, o_ref = refs[:-1], refs[-1]
    step = pl.program_id(1)
    n_blocks = o_ref.shape[2]
    per_page = PAGE_SIZE // NSA_BLOCK

    @pl.when(step == 0)
    def _():
        o_ref[...] = jnp.zeros(o_ref.shape, F32)

    row_blk = lax.broadcasted_iota(jnp.int32, (PAGE_SIZE, 1), 0) // NSA_BLOCK
    col = lax.broadcasted_iota(jnp.int32, (1, n_blocks), 1)
    acc = o_ref[0]
    for pg, page_ref in enumerate(page_refs):
        first = (step * PAGES_PER_STEP + pg) * per_page
        avg = jnp.where(col == first + row_blk, 1.0 / NSA_BLOCK, 0.0).astype(BF16)
        hi, lo = _split_bf16(page_ref[0])
        acc = acc + jnp.dot(hi, avg, preferred_element_type=F32) + jnp.dot(lo, avg, preferred_element_type=F32)
    o_ref[0] = acc


def _page_block_mean(pages, page_table):
    B, n_pages = page_table.shape
    F = pages.shape[1]
    n_blocks = n_pages * PAGE_SIZE // NSA_BLOCK
    grid_spec = pltpu.PrefetchScalarGridSpec(
        num_scalar_prefetch=1,
        grid=(B, n_pages // PAGES_PER_STEP),
        in_specs=_page_specs(F),
        out_specs=pl.BlockSpec((1, F, n_blocks), lambda b, s, pt: (b, 0, 0)),
    )
    return pl.pallas_call(
        _page_block_mean_kernel,
        out_shape=jax.ShapeDtypeStruct((B, F, n_blocks), F32),
        grid_spec=grid_spec,
        compiler_params=pltpu.CompilerParams(dimension_semantics=("parallel", "arbitrary")),
        name="page_block_mean",
    )(page_table, *([pages] * PAGES_PER_STEP))


def _head_mean_sq(x, ones_bd):
    return jnp.dot((x * x).astype(BF16), ones_bd, preferred_element_type=F32)


def _cmp_finish_kernel(mean_ref, pe_ref, wk_ref, wv_ref, gain_ref, bd_ref, kc_ref, vc_ref):
    s = mean_ref[0] + pe_ref[...]
    sk = jnp.dot(s[:, :LANES].astype(BF16), wk_ref[...], preferred_element_type=F32)
    vc_ref[0] = jnp.dot(s[:, LANES:].astype(BF16), wv_ref[...], preferred_element_type=F32)
    kc_ref[0] = sk * lax.rsqrt(_head_mean_sq(sk, bd_ref[...]) + NORM_EPS) * gain_ref[...]


def _block_diag2(w):
    z = jnp.zeros_like(w)
    return jnp.concatenate([jnp.concatenate([w, z], axis=1), jnp.concatenate([z, w], axis=1)], axis=0)


def _cmp_finish(mean, pe, w_phi, kc_gain):
    B, n_cb, W = mean.shape
    pe_mean = jnp.mean(pe, axis=0)
    pe_row = jnp.concatenate([pe_mean[0], pe_mean[0], pe_mean[1], pe_mean[1]])[None, :]
    wk = _block_diag2(w_phi[0]).astype(BF16)
    wv = _block_diag2(w_phi[1]).astype(BF16)
    bd = _block_diag2(jnp.full((HALF, HALF), 1.0 / HALF, F32)).astype(BF16)
    gain = jnp.concatenate([kc_gain, kc_gain])[None, :]
    full = lambda shape: pl.BlockSpec(shape, lambda b: (0,) * len(shape))
    return pl.pallas_call(
        _cmp_finish_kernel,
        out_shape=[jax.ShapeDtypeStruct((B, n_cb, LANES), F32)] * 2,
        grid=(B,),
        in_specs=[pl.BlockSpec((1, n_cb, W), lambda b: (b, 0, 0)), full((1, W)), full((LANES, LANES)),
                  full((LANES, LANES)), full((1, LANES)), full((LANES, LANES))],
        out_specs=[pl.BlockSpec((1, n_cb, LANES), lambda b: (b, 0, 0))] * 2,
        compiler_params=pltpu.CompilerParams(dimension_semantics=("parallel",)),
        name="cmp_finish",
    )(mean, pe_row, wk, wv, gain, bd)


def _cmp_select_kernel(q_ref, kc_ref, vc_ref, ocmp_ref, sel_ref, *, tq, pos0):
    n_cb = kc_ref.shape[1]
    q0 = pl.program_id(1) * tq
    qh, ql = _split_bf16(q_ref[0].reshape(N_QHEADS * tq, LANES))
    kh, kl = _split_bf16(kc_ref[0])
    s = (_nt_dot(qh, kh) + _nt_dot(ql, kh) + _nt_dot(qh, kl)) * (HEAD_DIM ** -0.5)
    cur = (pos0 + q0 + lax.broadcasted_iota(jnp.int32, (tq, 1), 0)) // NSA_BLOCK
    blk = lax.broadcasted_iota(jnp.int32, (1, n_cb), 1)
    vis = blk < cur
    s = jnp.where(vis[None], s.reshape(N_QHEADS, tq, n_cb), NEG_INF)
    p = jnp.exp(s - jnp.max(s, axis=-1, keepdims=True))
    p = jnp.where(vis[None], p / jnp.sum(p, axis=-1, keepdims=True), 0.0)
    o = jnp.dot(p.reshape(N_QHEADS * tq, n_cb).astype(BF16), vc_ref[0].astype(BF16), preferred_element_type=F32)
    ocmp_ref[0] = o.reshape(N_QHEADS, tq, LANES)
    for kvh in range(NSA_KV_HEADS):
        g0 = kvh * NSA_GROUP
        imp = jnp.where(vis, p[g0] + p[g0 + 1] + p[g0 + 2] + p[g0 + 3], -1.0)
        if tq % LANES == 0:
            imp_t = imp.T
            blk_t = lax.broadcasted_iota(jnp.int32, (n_cb, tq), 0)
            cur_t = (pos0 + q0 + lax.broadcasted_iota(jnp.int32, (1, tq), 1)) // NSA_BLOCK
            rank = jnp.zeros((n_cb, tq), jnp.int32)
            for i in range(n_cb):
                row = imp_t[i:i + 1, :]
                wins_tie = jnp.where(blk_t > i, 1, 0)
                rank = rank + jnp.where(row > imp_t, 1, jnp.where(row == imp_t, wins_tie, 0))
            chosen = jnp.where(blk_t < cur_t, jnp.where(rank < NSA_N_SEL - 1, 1.0, 0.0),
                               jnp.where(blk_t == cur_t, 1.0, 0.0)).T
        else:
            rank = jnp.zeros((tq, n_cb), jnp.int32)
            for i in range(n_cb):
                col = imp[:, i:i + 1]
                wins_tie = jnp.where(blk > i, 1, 0)
                rank = rank + jnp.where(col > imp, 1, jnp.where(col == imp, wins_tie, 0))
            chosen = jnp.where(vis, jnp.where(rank < NSA_N_SEL - 1, 1.0, 0.0), jnp.where(blk == cur, 1.0, 0.0))
        sel_ref[0, kvh] = chosen.astype(BF16)


def _cmp_select(qn, kc, vc, pos0):
    B, _, T, _ = qn.shape
    n_cb = kc.shape[1]
    tq = min(T, Q_BLOCK)
    return pl.pallas_call(
        functools.partial(_cmp_select_kernel, tq=tq, pos0=pos0),
        out_shape=[jax.ShapeDtypeStruct((B, N_QHEADS, T, LANES), F32),
                   jax.ShapeDtypeStruct((B, NSA_KV_HEADS, T, n_cb), BF16)],
        grid=(B, T // tq),
        in_specs=[
            pl.BlockSpec((1, N_QHEADS, tq, LANES), lambda b, i: (b, 0, i, 0)),
            pl.BlockSpec((1, n_cb, LANES), lambda b, i: (b, 0, 0)),
            pl.BlockSpec((1, n_cb, LANES), lambda b, i: (b, 0, 0)),
        ],
        out_specs=[pl.BlockSpec((1, N_QHEADS, tq, LANES), lambda b, i: (b, 0, i, 0)),
                   pl.BlockSpec((1, NSA_KV_HEADS, tq, n_cb), lambda b, i: (b, 0, i, 0))],
        compiler_params=pltpu.CompilerParams(
            dimension_semantics=("parallel", "parallel"), vmem_limit_bytes=VMEM_LIMIT_BYTES),
        name="cmp_select",
    )(qn, kc, vc)


def _selection_bias_kv(sel, first_key, n_keys, causal):
    n_cb = sel.shape[-1]
    key_blk = (first_key + lax.broadcasted_iota(jnp.int32, (1, n_keys), 1)) // NSA_BLOCK
    expand = jnp.where(lax.broadcasted_iota(jnp.int32, (n_cb, 1), 0) == key_blk, 1.0, 0.0).astype(BF16)
    out = []
    for kvh in range(NSA_KV_HEADS):
        picked = jnp.dot(sel[kvh], expand, preferred_element_type=F32)
        bias = (picked - 1.0) * 1e30
        if causal is not None:
            bias = jnp.where(causal, bias, NEG_INF)
        out.append(bias)
    return out


def _selection_bias(sel, first_key, n_keys, causal):
    return jnp.concatenate([jnp.broadcast_to(b[None], (NSA_GROUP,) + b.shape)
                            for b in _selection_bias_kv(sel, first_key, n_keys, causal)], axis=0)


def _online_softmax_step(s, v, m_ref, l_ref, acc_ref, tq, v_feature_major=False):
    n = s.shape[-1]
    m_old = m_ref[...]
    m_new = jnp.maximum(m_old, jnp.max(s, axis=-1, keepdims=True))
    alpha = jnp.exp(m_old - m_new)
    p = jnp.exp(s - m_new[..., :1])
    l_ref[...] = alpha * l_ref[...] + jnp.sum(p, axis=-1, keepdims=True)
    p2 = p.reshape(N_QHEADS * tq, n).astype(BF16)
    pv = _nt_dot(p2, v) if v_feature_major else jnp.dot(p2, v, preferred_element_type=F32)
    acc_ref[...] = alpha * acc_ref[...] + pv.reshape(N_QHEADS, tq, LANES)
    m_ref[...] = m_new


def _init_softmax_state(m_ref, l_ref, acc_ref):
    m_ref[...] = jnp.full(m_ref.shape, NEG_INF, F32)
    l_ref[...] = jnp.zeros(l_ref.shape, F32)
    acc_ref[...] = jnp.zeros(acc_ref.shape, F32)


def _gated_heads(gate_pre, o_cmp, o_slc, o_win):
    g = jax.nn.sigmoid(gate_pre)
    lane = lax.broadcasted_iota(jnp.int32, (1, LANES), 1)
    pairs = []
    for c in range(N_QHEADS // 2):
        mixed = []
        for h in (2 * c, 2 * c + 1):
            mixed.append(g[:, 3 * h:3 * h + 1] * o_cmp[h] + g[:, 3 * h + 1:3 * h + 2] * o_slc[h]
                         + g[:, 3 * h + 2:3 * h + 3] * o_win[h])
        if 2 * c < N_QHEADS // 2:
            pairs.append(jnp.where(lane < HALF, mixed[0], pltpu.roll(mixed[1], HALF, 1)))
        else:
            pairs.append(jnp.where(lane < HALF, pltpu.roll(mixed[0], HALF, 1), mixed[1]))
    return jnp.concatenate(pairs, axis=-1)


SLC_KEY_TILE = 512


def _nsa_prompt_kernel(q_ref, ks_ref, vs_ref, kw_ref, vw_ref, sel_ref, ocmp_ref, g_ref, o_ref,
                       m_ref, l_ref, acc_ref, *, tq, tk, span):
    T = ks_ref.shape[1]
    q0 = pl.program_id(1) * tq
    q = q_ref[0].reshape(N_QHEADS * tq, LANES)
    qpos = q0 + lax.broadcasted_iota(jnp.int32, (tq, 1), 0)
    sel = sel_ref[0]
    _init_softmax_state(m_ref, l_ref, acc_ref)

    def key_tile(j, carry):
        k0 = pl.multiple_of(j * tk, tk)
        kpos = k0 + lax.broadcasted_iota(jnp.int32, (1, tk), 1)
        s = _nt_dot(q, ks_ref[0, pl.ds(k0, tk), :]).reshape(N_QHEADS, tq, tk)
        s = s + _selection_bias(sel, k0, tk, kpos <= qpos)
        _online_softmax_step(s, vs_ref[0, pl.ds(k0, tk), :], m_ref, l_ref, acc_ref, tq)
        return carry

    lax.fori_loop(0, (q0 + tq + tk - 1) // tk, key_tile, 0)
    o_slc = acc_ref[...] / l_ref[...]

    if span == T:
        start = 0
        kw, vw = kw_ref[0], vw_ref[0]
    else:
        start = pl.multiple_of(jnp.maximum(q0 - NSA_WINDOW, 0), LANES)
        kw, vw = kw_ref[0, pl.ds(start, span), :], vw_ref[0, pl.ds(start, span), :]
    kpos = start + lax.broadcasted_iota(jnp.int32, (1, span), 1)
    o_win, _ = _softmax_pv(_nt_dot(q, kw), _band_mask(qpos, kpos, NSA_WINDOW, 1), vw, tq)
    o_ref[0] = _gated_heads(g_ref[0], ocmp_ref[0], o_slc, o_win)


def _nsa_prompt(qr, ks, vs, kw, vw, sel, ocmp, gate_pre):
    B, _, T, _ = qr.shape
    n_cb = sel.shape[-1]
    tq = min(T, Q_BLOCK)
    tk = min(T, SLC_KEY_TILE)
    span = min(T, NSA_WINDOW + tq)
    seq = pl.BlockSpec((1, T, LANES), lambda b, i: (b, 0, 0))
    heads = pl.BlockSpec((1, N_QHEADS, tq, LANES), lambda b, i: (b, 0, i, 0))
    stat = pltpu.VMEM((N_QHEADS, tq, LANES), F32)
    return pl.pallas_call(
        functools.partial(_nsa_prompt_kernel, tq=tq, tk=tk, span=span),
        out_shape=jax.ShapeDtypeStruct((B, T, NSA_Q_W), F32),
        grid=(B, T // tq),
        in_specs=[heads, seq, seq, seq, seq,
                  pl.BlockSpec((1, NSA_KV_HEADS, tq, n_cb), lambda b, i: (b, 0, i, 0)),
                  heads,
                  pl.BlockSpec((1, tq, gate_pre.shape[2]), lambda b, i: (b, i, 0))],
        out_specs=pl.BlockSpec((1, tq, NSA_Q_W), lambda b, i: (b, i, 0)),
        scratch_shapes=[stat, stat, stat],
        compiler_params=pltpu.CompilerParams(
            dimension_semantics=("parallel", "arbitrary"), vmem_limit_bytes=VMEM_LIMIT_BYTES),
        name="nsa_prompt",
    )(qr, ks, vs, kw, vw, sel, ocmp, gate_pre)


def _nsa_paged_kernel(pt_ref, q_ref, *refs, tq, pos0):
    page_refs = refs[:PAGES_PER_STEP]
    new_ref, sel_ref, ocmp_ref, owin_ref, g_ref, o_ref, m_ref, l_ref, acc_ref = refs[PAGES_PER_STEP:]
    step = pl.program_id(1)
    q = q_ref[0].reshape(N_QHEADS * tq, LANES).astype(BF16)
    sel = sel_ref[0]

    @pl.when(step == 0)
    def _():
        _init_softmax_state(m_ref, l_ref, acc_ref)

    def attend(kt, vt, first_key, causal):
        n = kt.shape[1]
        s = jnp.dot(q, kt, preferred_element_type=F32).reshape(N_QHEADS, tq, n)
        if causal is None:
            s = s + _selection_bias(sel, first_key, n, None)
        else:
            s = jnp.where(causal[None], s, NEG_INF)
        _online_softmax_step(s, vt, m_ref, l_ref, acc_ref, tq, v_feature_major=True)

    attend(jnp.concatenate([r[0, :LANES, :].astype(BF16) for r in page_refs], axis=1),
           jnp.concatenate([r[0, LANES:, :].astype(BF16) for r in page_refs], axis=1),
           step * (PAGES_PER_STEP * PAGE_SIZE), None)

    @pl.when(step == pl.num_programs(1) - 1)
    def _():
        qpos = lax.broadcasted_iota(jnp.int32, (tq, 1), 0)
        kpos = lax.broadcasted_iota(jnp.int32, (1, PAGE_SIZE), 1)
        new = new_ref[0]
        attend(new[:LANES, :].astype(BF16), new[LANES:, :].astype(BF16), pos0, kpos <= qpos)
        o_ref[0] = _gated_heads(g_ref[0], ocmp_ref[0], acc_ref[...] / l_ref[...], owin_ref[0])


def _nsa_paged(page_table, qr, cache, new_tile, sel, ocmp, owin, gate_pre, pos0):
    B, _, T, _ = qr.shape
    n_pages = page_table.shape[1]
    n_cb = sel.shape[-1]
    W = cache.shape[1]
    heads = pl.BlockSpec((1, N_QHEADS, T, LANES), lambda b, s, pt: (b, 0, 0, 0))
    stat = pltpu.VMEM((N_QHEADS, T, LANES), F32)
    grid_spec = pltpu.PrefetchScalarGridSpec(
        num_scalar_prefetch=1,
        grid=(B, n_pages // PAGES_PER_STEP),
        in_specs=[heads] + _page_specs(W) + [
            pl.BlockSpec((1, W, PAGE_SIZE), lambda b, s, pt: (b, 0, 0)),
            pl.BlockSpec((1, NSA_KV_HEADS, T, n_cb), lambda b, s, pt: (b, 0, 0, 0)),
            heads, heads,
            pl.BlockSpec((1, T, gate_pre.shape[2]), lambda b, s, pt: (b, 0, 0))],
        out_specs=pl.BlockSpec((1, T, NSA_Q_W), lambda b, s, pt: (b, 0, 0)),
        scratch_shapes=[stat, stat, stat],
    )
    return pl.pallas_call(
        functools.partial(_nsa_paged_kernel, tq=T, pos0=pos0),
        out_shape=jax.ShapeDtypeStruct((B, T, NSA_Q_W), F32),
        grid_spec=grid_spec,
        compiler_params=pltpu.CompilerParams(
            dimension_semantics=("parallel", "arbitrary"), vmem_limit_bytes=VMEM_LIMIT_BYTES),
        name="nsa_paged",
    )(page_table, qr, *([cache] * PAGES_PER_STEP), new_tile, sel, ocmp, owin, gate_pre)


def _rope_tables(pos):
    half = ROT_DIM // 2
    inv = jnp.exp(-math.log(ROPE_THETA) * jnp.arange(half, dtype=F32) * (2.0 / ROT_DIM))
    ang = pos.astype(F32)[:, None] * inv[None, :]
    cos, sin = jnp.cos(ang), jnp.sin(ang)
    T = pos.shape[0]
    zeros = lambda n: jnp.zeros((T, n), F32)
    cos_t = jnp.concatenate([cos, cos, jnp.ones((T, HALF - ROT_DIM), F32)], axis=-1)
    msin_t = jnp.concatenate([-sin, zeros(HALF - half)], axis=-1)
    psin_t = jnp.concatenate([zeros(half), sin, zeros(HALF - ROT_DIM)], axis=-1)
    twice = lambda a: jnp.concatenate([a, a], axis=-1)
    return twice(cos_t), twice(msin_t), twice(psin_t)


def _rope_lanes(x, cos_t, msin_t, psin_t):
    half = ROT_DIM // 2
    return x * cos_t + pltpu.roll(x, LANES - half, 1) * msin_t + pltpu.roll(x, half, 1) * psin_t


def _head_rms(x, ones_mat, gain):
    return x * lax.rsqrt(_head_mean_sq(x, ones_mat) + NORM_EPS) * gain


def _store_head_pair(heads_ref, pair, val, put=None):
    lower = lax.broadcasted_iota(jnp.int32, (1, LANES), 1) < HALF
    swapped = pltpu.roll(val, HALF, 1)
    if 2 * pair < N_QHEADS // 2:
        first, second = jnp.where(lower, val, 0.0), jnp.where(lower, swapped, 0.0)
    else:
        first, second = jnp.where(lower, 0.0, swapped), jnp.where(lower, 0.0, val)
    for hd, head_val in ((2 * pair, first), (2 * pair + 1, second)):
        if put is None:
            heads_ref[hd] = head_val.astype(heads_ref.dtype)
        else:
            put(hd, head_val)


def _ab_proj_kernel(x_ref, g_ref, sc_ref, sh_ref, w_ref, cos_ref, msin_ref, psin_ref, gq_ref, gks_ref, gkw_ref,
                    bd_ref, lng_ref, lnb_ref, wm_ref, bias_ref,
                    qn_ref, qr_ref, kvc_ref, kvs_ref, ks_ref, vs_ref, kvw_ref, kw_ref, vw_ref, gate_ref,
                    ob_ref, vln_ref, *, tm):
    h = _mod_norm(x_ref[0], g_ref[...], sc_ref[0], sh_ref[0])
    y = jnp.dot(h.astype(BF16), w_ref[...], preferred_element_type=F32)
    tabs = (cos_ref[...], msin_ref[...], psin_ref[...])
    for pair in range(N_QHEADS // 2):
        qn = _head_rms(y[:, pair * LANES:(pair + 1) * LANES], bd_ref[...], gq_ref[...])
        _store_head_pair(qn_ref.at[0], pair, qn)
        _store_head_pair(qr_ref.at[0], pair, _rope_lanes(qn, *tabs) * (HEAD_DIM ** -0.5))
    off = NSA_Q_W
    kvc_ref[0] = y[:, off:off + NSA_KV_W]
    off += NSA_KV_W
    for kv_ref, k_ref, v_ref, gain_ref in ((kvs_ref, ks_ref, vs_ref, gks_ref), (kvw_ref, kw_ref, vw_ref, gkw_ref)):
        k = _rope_lanes(_head_rms(y[:, off:off + LANES], bd_ref[...], gain_ref[...]), *tabs)
        v = y[:, off + LANES:off + NSA_KV_W]
        kv_ref[0, :, :LANES] = k
        kv_ref[0, :, LANES:] = v
        k_ref[0] = k.astype(BF16)
        v_ref[0] = v.astype(BF16)
        off += NSA_KV_W
    u = jax.nn.gelu(y[:, off:off + GM_WIDTH])
    v = jax.nn.gelu(y[:, off + GM_WIDTH:off + 2 * GM_WIDTH])
    off += 2 * GM_WIDTH
    gate_ref[0] = y[:, off:]
    mu = jnp.mean(v, axis=-1, keepdims=True)
    var = jnp.mean(jnp.square(v - mu), axis=-1, keepdims=True)
    vln = (v - mu) * lax.rsqrt(var + NORM_EPS) * lng_ref[...] + lnb_ref[...]
    vln_ref[0] = vln
    lane = lax.broadcasted_iota(jnp.int32, (1, LANES), 1)
    for c in range(tm // GM_CHUNK):
        rows = slice(c * GM_CHUNK, (c + 1) * GM_CHUNK)
        mixed = []
        for p in range(GM_GROUPS // 2):
            vp = vln[rows, p * LANES:(p + 1) * LANES]
            lo = jnp.where(lane < HALF, vp, 0.0).astype(BF16)
            hi = jnp.where(lane < HALF, 0.0, vp).astype(BF16)
            mixed.append(jnp.dot(wm_ref[2 * p], lo, preferred_element_type=F32)
                         + jnp.dot(wm_ref[2 * p + 1], hi, preferred_element_type=F32))
        ob_ref[0, rows, :] = u[rows] * (jnp.concatenate(mixed, axis=-1) + bias_ref[...])


def _ab_proj(x, norm_g, scale, shift, w_in, pos, qk_gain, ln_g, ln_b, wm, mix_bias):
    B, T, D = x.shape
    tm = _row_tile(T)
    g0, g1 = AB_SPLITS[0], AB_SPLITS[0] + AB_SPLITS[1]
    w = jnp.concatenate([w_in[:, :g0], w_in[:, g1:], w_in[:, g0:g1]], axis=1).astype(BF16)
    N = w.shape[1]
    cos_t, msin_t, psin_t = _rope_tables(pos)
    bd = _block_diag2(jnp.full((HALF, HALF), 1.0 / HALF, F32)).astype(BF16)
    two = lambda g: jnp.concatenate([g, g])[None, :]
    const = lambda a: pl.BlockSpec(a.shape, lambda b, i: (0,) * a.ndim)
    tok = lambda n: pl.BlockSpec((1, tm, n), lambda b, i: (b, i, 0))
    heads = pl.BlockSpec((1, N_QHEADS, tm, LANES), lambda b, i: (b, 0, i, 0))
    table = pl.BlockSpec((tm, LANES), lambda b, i: (i, 0))
    consts = [two(qk_gain[0]), two(qk_gain[2]), two(qk_gain[3]), bd, ln_g[None, :], ln_b[None, :], wm, mix_bias]
    sds = lambda shape, dt: jax.ShapeDtypeStruct(shape, dt)
    return pl.pallas_call(
        functools.partial(_ab_proj_kernel, tm=tm),
        out_shape=[sds((B, N_QHEADS, T, LANES), F32), sds((B, N_QHEADS, T, LANES), BF16),
                   sds((B, T, NSA_KV_W), F32),
                   sds((B, T, NSA_KV_W), F32), sds((B, T, LANES), BF16), sds((B, T, LANES), BF16),
                   sds((B, T, NSA_KV_W), F32), sds((B, T, LANES), BF16), sds((B, T, LANES), BF16),
                   sds((B, T, 3 * NSA_HEADS), F32), sds((B, T, GM_WIDTH), F32), sds((B, T, GM_WIDTH), F32)],
        grid=(B, T // tm),
        in_specs=[tok(D), pl.BlockSpec((1, D), lambda b, i: (0, 0)), _mod_spec(scale, tm), _mod_spec(shift, tm),
                  pl.BlockSpec((D, N), lambda b, i: (0, 0)), table, table, table] + [const(a) for a in consts],
        out_specs=[heads, heads, tok(NSA_KV_W), tok(NSA_KV_W), tok(LANES), tok(LANES), tok(NSA_KV_W), tok(LANES),
                   tok(LANES), tok(3 * NSA_HEADS), tok(GM_WIDTH), tok(GM_WIDTH)],
        compiler_params=pltpu.CompilerParams(
            dimension_semantics=("parallel", "parallel"), vmem_limit_bytes=VMEM_LIMIT_BYTES),
        name="ab_proj",
    )(x, norm_g.reshape(1, D), scale, shift, w, cos_t, msin_t, psin_t, *consts)


def _chunk_mix_weights(ws, bs, rows_per_seq):
    n = min(rows_per_seq, GM_CHUNK)
    wm = jnp.where(jnp.tril(jnp.ones((n, n), bool)), ws[:, :n, :n], 0.0)
    if n < GM_CHUNK:
        eye = jnp.eye(GM_CHUNK // n, dtype=F32)
        wm = jnp.einsum('ab,gij->gaibj', eye, wm).reshape(GM_GROUPS, GM_CHUNK, GM_CHUNK)
    bias = jnp.tile(jnp.transpose(bs[:, :n]), (GM_CHUNK // n, 1))
    return wm.astype(BF16), jnp.repeat(bias, GM_GROUP_W, axis=1)


C_GROUP_W = DIL_Q_W + 2 * DIL_KV_W


def _c_proj_kernel(x_ref, g_ref, sc_ref, sh_ref, w_ref, cos_ref, msin_ref, psin_ref, gq_ref, gk_ref,
                   bd_ref, *refs, tm, dils):
    out_refs, stage = refs[:-1], refs[-1]
    h = _mod_norm(x_ref[0], g_ref[...], sc_ref[0], sh_ref[0])
    y = jnp.dot(h.astype(BF16), w_ref[...], preferred_element_type=F32)
    tabs = (cos_ref[...], msin_ref[...], psin_ref[...])

    def put(dst, val, d):
        if d == 1:
            dst[0] = val.astype(dst.dtype)
            return
        stage[...] = val
        for r in range(d):
            dst[r] = stage[pl.ds(r, tm // d, stride=d), :].astype(dst.dtype)

    for g, d in enumerate(dils):
        qd_ref, kd_ref, vd_ref, kv_ref = out_refs[4 * g:4 * g + 4]
        off = g * C_GROUP_W
        for pair in range(N_QHEADS // 2):
            lanes = slice(off + pair * LANES, off + (pair + 1) * LANES)
            q = _rope_lanes(_head_rms(y[:, lanes], bd_ref[...], gq_ref[...]), *tabs) * (HEAD_DIM ** -0.5)
            _store_head_pair(None, pair, q, put=lambda hd, val, qd_ref=qd_ref, d=d: put(qd_ref.at[:, hd], val, d))
        k = _rope_lanes(_head_rms(y[:, off + DIL_Q_W:off + DIL_Q_W + LANES], bd_ref[...], gk_ref[...]), *tabs)
        v = y[:, off + DIL_Q_W + LANES:off + C_GROUP_W]
        kv_ref[0, :, :LANES] = k
        kv_ref[0, :, LANES:] = v
        put(kd_ref, k, d)
        put(vd_ref, v, d)


def _c_proj(x, norm_g, scale, shift, w_in, pos, qk_gain, dils):
    B, T, D = x.shape
    tm = _row_tile(T)
    w = w_in.astype(BF16)
    N = w.shape[1]
    cos_t, msin_t, psin_t = _rope_tables(pos)
    bd = _block_diag2(jnp.full((HALF, HALF), 1.0 / HALF, F32)).astype(BF16)
    two = lambda g: jnp.concatenate([g, g])[None, :]
    consts = [two(qk_gain[0]), two(qk_gain[1]), bd]
    const = lambda a: pl.BlockSpec(a.shape, lambda b, i: (0,) * a.ndim)
    tok = lambda n: pl.BlockSpec((1, tm, n), lambda b, i: (b, i, 0))
    table = pl.BlockSpec((tm, LANES), lambda b, i: (i, 0))
    sds = lambda shape, dt: jax.ShapeDtypeStruct(shape, dt)
    out_shape, out_specs = [], []
    for d in dils:
        out_shape += [sds((B * d, N_QHEADS, T // d, LANES), BF16), sds((B * d, T // d, LANES), BF16),
                      sds((B * d, T // d, LANES), BF16), sds((B, T, 2 * DIL_KV_W), F32)]
        seq = pl.BlockSpec((d, tm // d, LANES), lambda b, i: (b, i, 0))
        out_specs += [pl.BlockSpec((d, N_QHEADS, tm // d, LANES), lambda b, i: (b, 0, i, 0)), seq, seq,
                      tok(2 * DIL_KV_W)]
    return pl.pallas_call(
        functools.partial(_c_proj_kernel, tm=tm, dils=tuple(dils)),
        out_shape=out_shape,
        grid=(B, T // tm),
        in_specs=[tok(D), pl.BlockSpec((1, D), lambda b, i: (0, 0)), _mod_spec(scale, tm), _mod_spec(shift, tm),
                  pl.BlockSpec((D, N), lambda b, i: (0, 0)), table, table, table] + [const(a) for a in consts],
        out_specs=out_specs,
        scratch_shapes=[pltpu.VMEM((tm, LANES), F32)],
        compiler_params=pltpu.CompilerParams(
            dimension_semantics=("parallel", "parallel"), vmem_limit_bytes=VMEM_LIMIT_BYTES),
        name="c_proj",
    )(x, norm_g.reshape(1, D), scale, shift, w, cos_t, msin_t, psin_t, *consts)


def _dil_merge_proj_kernel(*refs, tm, dils):
    o_refs = refs[:len(dils)]
    w_ref, x_ref, gate_ref, out_ref, stage = refs[len(dils):]
    lane = lax.broadcasted_iota(jnp.int32, (1, LANES), 1)
    merged = []
    for hd in range(N_QHEADS):
        lower = hd < N_QHEADS // 2
        vals = []
        for o_ref, d in zip(o_refs, dils):
            if d == 1:
                vals.append(o_ref[0, hd])
            else:
                for r in range(d):
                    stage[pl.ds(r, tm // d, stride=d), :] = o_ref[r, hd]
                vals.append(stage[...])
        lses = [v[:, HALF:HALF + 1] if lower else v[:, 0:1] for v in vals]
        top = jnp.maximum(jnp.maximum(lses[0], lses[1]), lses[2])
        ws = [jnp.exp(l - top) for l in lses]
        total = ws[0] + ws[1] + ws[2]
        merged.append((ws[0] * vals[0] + ws[1] * vals[1] + ws[2] * vals[2]) / total)
    pairs = []
    for c in range(N_QHEADS // 2):
        a, b = merged[2 * c], merged[2 * c + 1]
        if 2 * c < N_QHEADS // 2:
            pairs.append(jnp.where(lane < HALF, a, pltpu.roll(b, HALF, 1)))
        else:
            pairs.append(jnp.where(lane < HALF, pltpu.roll(a, HALF, 1), b))
    o = jnp.concatenate(pairs, axis=-1)
    y = jnp.dot(o.astype(BF16), w_ref[...], preferred_element_type=F32)
    out_ref[0] = x_ref[0] + gate_ref[0] * y


def _dil_merge_proj(outs, dils, w_out, x, gate):
    B, T, D = x.shape
    tm = _row_tile(T)
    in_specs = [pl.BlockSpec((d, N_QHEADS, tm // d, LANES), lambda b, i: (b, 0, i, 0)) for d in dils]
    in_specs += [pl.BlockSpec(w_out.shape, lambda b, i: (0, 0)),
                 pl.BlockSpec((1, tm, D), lambda b, i: (b, i, 0)), _mod_spec(gate, tm)]
    return pl.pallas_call(
        functools.partial(_dil_merge_proj_kernel, tm=tm, dils=tuple(dils)),
        out_shape=jax.ShapeDtypeStruct((B, T, D), F32),
        grid=(B, T // tm),
        in_specs=in_specs,
        out_specs=pl.BlockSpec((1, tm, D), lambda b, i: (b, i, 0)),
        scratch_shapes=[pltpu.VMEM((tm, LANES), F32)],
        compiler_params=pltpu.CompilerParams(
            dimension_semantics=("parallel", "parallel"), vmem_limit_bytes=VMEM_LIMIT_BYTES),
        name="dil_merge_proj",
    )(*outs, w_out, x, gate)


def _kv5(x):
    return x.reshape(x.shape[:2] + (2, NSA_KV_HEADS, HEAD_DIM))


def _mixer_ab(x, norm_g, sc, sh, gate_m, w_in, w_out, start, pasts, weights, page_table, seq_shape):
    qk_gain, pe, w_phi, ln_g, ln_b, ws, bs = weights
    B, T = seq_shape
    pos = start + jnp.arange(x.shape[1], dtype=jnp.int32) % T
    wm, mix_bias = _chunk_mix_weights(ws, bs, T)
    qn, qr, kvc, kvs, ks, vs, kvw, kw, vw, gate, o_b, vln = _ab_proj(
        x, norm_g, sc, sh, w_in, pos, qk_gain, ln_g, ln_b, wm, mix_bias)
    if pasts is None:
        kc, vc = _cmp_finish(_block_mean(kvc), pe, w_phi, qk_gain[1])
        ocmp, sel = _cmp_select(qn, kc, vc, start)
        o_a = _nsa_prompt(qr, ks, vs, kw, vw, sel, ocmp, gate)
        kvw_all = kvw
    else:
        per_seq = lambda a: a.reshape((B, T) + a.shape[2:])
        heads = lambda a: a.reshape(N_QHEADS, B, T, LANES).transpose(1, 0, 2, 3)
        kvc, kvs, kvw, gate, vln = (per_seq(a) for a in (kvc, kvs, kvw, gate, vln))
        qn, qr = heads(qn), heads(qr.astype(F32))
        cache_cmp, cache_slc, win_past = pasts
        means = jnp.swapaxes(_page_block_mean(cache_cmp, page_table), 1, 2)
        kc, vc = _cmp_finish(means, pe, w_phi, qk_gain[1])
        ocmp, sel = _cmp_select(qn, kc, vc, start)
        kvw_all = jnp.concatenate([win_past, kvw], axis=1)
        owin = _band_tail(qr, kvw_all, NSA_WINDOW, 1, False)
        new_tile = jnp.swapaxes(jnp.pad(kvs, ((0, 0), (0, PAGE_SIZE - T), (0, 0))), 1, 2)
        o_a = _nsa_paged(page_table, qr, cache_slc, new_tile, sel, ocmp, owin, gate, start)
    keep_w = min(NSA_WINDOW, kvw_all.shape[1])
    n_cur = (T - 1) % GM_CHUNK + 1
    x_new = _proj_residual([o_a.reshape(o_b.shape), o_b], (w_out[:NSA_Q_W], w_out[NSA_Q_W:]), x, gate_m)
    return x_new, (_kv5(kvc), _kv5(kvs), _kv5(kvw_all[:, kvw_all.shape[1] - keep_w:]), vln[:, T - n_cur:])


def _mixer_c(x, norm_g, sc, sh, gate_m, w_in, w_out, start, pasts, qk_gain, seq_shape):
    B, T = seq_shape
    pos = start + jnp.arange(x.shape[1], dtype=jnp.int32) % T
    dils = tuple(d for _, d in DIL_CFG) if pasts is None else (1,) * len(DIL_CFG)
    res = _c_proj(x, norm_g, sc, sh, w_in, pos, qk_gain, dils)
    outs, new_bufs = [], []
    for g, (window, dilation) in enumerate(DIL_CFG):
        qd, kd, vd, kv_new = res[4 * g:4 * g + 4]
        if pasts is None:
            outs.append(_band_self(qd, kd, vd, window // dilation, True))
            kv_all = kv_new
        else:
            kv_all = jnp.concatenate([pasts[g], kv_new.reshape(B, T, 2 * DIL_KV_W)], axis=1)
            q = qd.reshape(N_QHEADS, B, T, LANES).transpose(1, 0, 2, 3).astype(F32)
            o = _band_tail(q, kv_all, window, dilation, True)
            outs.append(o.transpose(1, 0, 2, 3).reshape(1, N_QHEADS, B * T, LANES))
        keep = min(window, kv_all.shape[1])
        new_bufs.append(kv_all[:, kv_all.shape[1] - keep:].reshape(B, keep, 2, DIL_KV_HEADS, HEAD_DIM))
    return _dil_merge_proj(outs, dils, w_out, x, gate_m), tuple(new_bufs)


def _adaln_kernel(c_ref, w_ref, b_ref, o_ref):
    act = jax.nn.silu(c_ref[...]).astype(BF16)
    o_ref[...] = jnp.dot(act, w_ref[0].astype(BF16), preferred_element_type=F32) + b_ref[0]


def _adaln(c, layer, w, b):
    B, D = c.shape
    depth, _, N = w.shape
    tn = D
    mod = pl.pallas_call(
        _adaln_kernel,
        out_shape=jax.ShapeDtypeStruct((B, N), F32),
        grid=(N // tn,),
        in_specs=[pl.BlockSpec((B, D), lambda j: (0, 0)), pl.BlockSpec((1, D, tn), lambda j: (layer, 0, j)),
                  pl.BlockSpec((1, 1, tn), lambda j: (layer, 0, j))],
        out_specs=pl.BlockSpec((B, tn), lambda j: (0, j)),
        compiler_params=pltpu.CompilerParams(dimension_semantics=("parallel",)),
        name="adaln",
    )(c, w, b.reshape(depth, 1, N))
    return [mod[:, None, j * D:(j + 1) * D] for j in range(N // D)]


def _expand_mod(m, t):
    B, _, D = m.shape
    return jnp.broadcast_to(m, (B, t, D)).reshape(1, B * t, D)


def _slot_buffer(n_tok, d_model):
    n_tiles = -(-(n_tok * TOP_K + N_EXPERTS * (MOE_TILE - 1)) // MOE_TILE)
    return jnp.zeros((n_tiles * MOE_TILE, d_model), F32)


def _layer(groups, mixers, norm_m, norm_f, moe_w, x_sorted):
    w_router, b_router, *ffn_w = moe_w
    xs, states, routed = [], [], []
    counts = jnp.zeros((1, N_EXPERTS), F32)
    for (x, mods), mixer in zip(groups, mixers):
        sh_m, sc_m, g_m, sh_f, sc_f, g_f = mods
        B, T, D = x.shape
        x, state = mixer(x, norm_m, sc_m, sh_m, g_m)
        h, route_i, route_g, counts = _mod_norm_router(x, norm_f, sc_f, sh_f, w_router, b_router, counts)
        xs.append(x)
        states.append(state)
        routed.append((h, route_i, route_g))
    n_tiles = x_sorted.shape[0] // MOE_TILE
    pad_start, pad_end, n_used = _moe_plan(counts[0])
    dests = []
    for h, route_i, _ in routed:
        expert = route_i[..., :TOP_K]
        rank = route_i[..., ROUTE_RANK_LANE:ROUTE_RANK_LANE + TOP_K]
        first = jnp.sum(jnp.where(expert[..., None] == jnp.arange(N_EXPERTS), pad_start, 0), axis=-1)
        dests.append((first + rank).astype(jnp.int32))
        x_sorted = _moe_dispatch(dests[-1], h, x_sorted)
    y_sorted = _moe_ffn(*_moe_tiles(pad_end, n_used, n_tiles), x_sorted, *ffn_w)
    outs = [_moe_combine(dest, y_sorted, route_g, x, mods[5])
            for dest, (_, _, route_g), x, (_, mods) in zip(dests, routed, xs, groups)]
    return outs, states, x_sorted


def _stack(states, j):
    return jnp.stack([s[j] for s in states], axis=0)


def kernel(x_prompt, x_sample, cache_cmp_kv, cache_slc_kv, state_win_kv, state_dil0_kv, state_dil1_kv,
           state_dil2_kv, page_table, c_prompt, c_sample, norm_mix, norm_ffn, w_ada, b_ada, w_in_ab, w_out_ab,
           nsa_qk_gain, nsa_pe, nsa_w_phi, gm_ln_g, gm_ln_b, gm_ws, gm_bs, w_in_c, w_out_c, dil_qk_gain,
           w_router, b_router, w_up, b_up, w_down, b_down):
    depth = norm_mix.shape[0]
    past_len = page_table.shape[1] * PAGE_SIZE
    Bs, Ts, D = x_sample.shape
    x_p = x_prompt
    x_s = x_sample.reshape(1, Bs * Ts, D)
    ab_p, ab_s, dil_p, dil_s = [], [], [], []
    slots = _slot_buffer(x_prompt.shape[0] * x_prompt.shape[1] + Bs * Ts, D)
    for layer in range(depth):
        i = layer // 2
        moe_w = (w_router[layer], b_router[layer], layer, w_up, b_up, w_down, b_down)
        mods_p = _adaln(c_prompt, layer, w_ada, b_ada)
        mods_s = [_expand_mod(m, Ts) for m in _adaln(c_sample, layer, w_ada, b_ada)]
        groups = [(x_p, mods_p), (x_s, mods_s)]
        if layer % 2 == 0:
            wts = (nsa_qk_gain[i], nsa_pe[i], nsa_w_phi[i], gm_ln_g[i], gm_ln_b[i], gm_ws[i], gm_bs[i])
            flat_kv = lambda a: a.reshape(a.shape[:2] + (NSA_KV_W,))
            pasts_s = (_feature_major_pages(cache_cmp_kv[i]), _feature_major_pages(cache_slc_kv[i]),
                       flat_kv(state_win_kv[i]))

            w_io = (w_in_ab[i], w_out_ab[i].astype(BF16))

            def mix_p(x, g, sc, sh, gm, wts=wts, w_io=w_io):
                return _mixer_ab(x, g, sc, sh, gm, *w_io, 0, None, wts, None, x.shape[:2])

            def mix_s(x, g, sc, sh, gm, wts=wts, w_io=w_io, pasts_s=pasts_s):
                return _mixer_ab(x, g, sc, sh, gm, *w_io, past_len, pasts_s, wts, page_table, (Bs, Ts))

            (x_p, x_s), (st_p, st_s), slots = _layer(
                groups, (mix_p, mix_s), norm_mix[layer], norm_ffn[layer], moe_w, slots)
            ab_p.append(st_p)
            ab_s.append(st_s)
        else:
            flat_kv = lambda a: a.reshape(a.shape[:2] + (2 * DIL_KV_W,))
            pasts_s = (flat_kv(state_dil0_kv[i]), flat_kv(state_dil1_kv[i]), flat_kv(state_dil2_kv[i]))

            w_io = (w_in_c[i], w_out_c[i].astype(BF16))

            def mix_p(x, g, sc, sh, gm, gain=dil_qk_gain[i], w_io=w_io):
                return _mixer_c(x, g, sc, sh, gm, *w_io, 0, None, gain, x.shape[:2])

            def mix_s(x, g, sc, sh, gm, pasts_s=pasts_s, gain=dil_qk_gain[i], w_io=w_io):
                return _mixer_c(x, g, sc, sh, gm, *w_io, past_len, pasts_s, gain, (Bs, Ts))

            (x_p, x_s), (st_p, st_s), slots = _layer(
                groups, (mix_p, mix_s), norm_mix[layer], norm_ffn[layer], moe_w, slots)
            dil_p.append(st_p)
            dil_s.append(st_s)
    x_s = x_s.reshape(Bs, Ts, D)
    cmp_p, cmp_s = _stack(ab_p, 0), _stack(ab_s, 0)
    slc_p, slc_s = _stack(ab_p, 1), _stack(ab_s, 1)
    win_p, win_s = _stack(ab_p, 2), _stack(ab_s, 2)
    gmv_p, gmv_s = _stack(ab_p, 3), _stack(ab_s, 3)
    d0_p, d0_s = _stack(dil_p, 0), _stack(dil_s, 0)
    d1_p, d1_s = _stack(dil_p, 1), _stack(dil_s, 1)
    d2_p, d2_s = _stack(dil_p, 2), _stack(dil_s, 2)
    return (x_p, x_s, cmp_p, cmp_s, slc_p, slc_s, win_p, win_s, gmv_p, gmv_s, d0_p, d0_s, d1_p, d1_s, d2_p, d2_s)
```

```python
import functools
import math

import jax
import jax.numpy as jnp
from jax import lax
from jax.experimental import pallas as pl
from jax.experimental.pallas import tpu as pltpu

F32 = jnp.float32
BF16 = jnp.bfloat16

D_MODEL = 1024
HEAD_DIM = 64
ROT_DIM = HEAD_DIM // 4
ROPE_THETA = 500000.0
NORM_EPS = 1e-6
NEG_INF = -1e30
Q_BLOCK = 128
ATTN_Q_TILE = 256
PAGE_SIZE = 128

NSA_HEADS = D_MODEL // (2 * HEAD_DIM)
NSA_KV_HEADS = 2
NSA_GROUP = NSA_HEADS // NSA_KV_HEADS
NSA_BLOCK = 64
NSA_N_SEL = 16
NSA_WINDOW = 512
NSA_Q_W = NSA_HEADS * HEAD_DIM
NSA_KV_W = 2 * NSA_KV_HEADS * HEAD_DIM

GM_GROUPS = 8
GM_WIDTH = D_MODEL // 2
GM_GROUP_W = GM_WIDTH // GM_GROUPS
GM_CHUNK = 128

AB_SPLITS = (NSA_Q_W, 3 * NSA_HEADS, NSA_KV_W, NSA_KV_W, NSA_KV_W, GM_WIDTH, GM_WIDTH)

DIL_CFG = ((128, 1), (512, 4), (2048, 16))
DIL_HEADS = 8
DIL_KV_HEADS = 2
DIL_Q_W = DIL_HEADS * HEAD_DIM
DIL_KV_W = DIL_KV_HEADS * HEAD_DIM

N_EXPERTS = 32
TOP_K = 4
D_FF = D_MODEL
SWIGLU_ALPHA = 1.702
SWIGLU_LIMIT = 7.0

VMEM_LIMIT_BYTES = 56 * 1024 * 1024
ROW_TILE = 512
MOE_TILE = 512


def _row_tile(t):
    return t if t <= ROW_TILE else ROW_TILE


def _mod_spec(mod, tm):
    if mod.shape[1] == 1:
        return pl.BlockSpec((1, 1, mod.shape[2]), lambda b, i: (b, 0, 0))
    return pl.BlockSpec((1, tm, mod.shape[2]), lambda b, i: (b, i, 0))


def _split_bf16(v):
    hi = v.astype(BF16)
    lo = (v - hi.astype(F32)).astype(BF16)
    return hi, lo


def _mod_norm(x, g, sc, sh):
    h = x * lax.rsqrt(jnp.mean(x * x, axis=-1, keepdims=True) + NORM_EPS) * g
    return h * (1.0 + sc) + sh


def _proj_residual_kernel(*refs, n_parts):
    a_refs = refs[:n_parts]
    w_refs = refs[n_parts:2 * n_parts]
    x_ref, g_ref, o_ref = refs[2 * n_parts:]
    y = None
    for a_ref, w_ref in zip(a_refs, w_refs):
        p = jnp.dot(a_ref[0].astype(BF16), w_ref[...], preferred_element_type=F32)
        y = p if y is None else y + p
    o_ref[0] = x_ref[0] + g_ref[0] * y


def _proj_residual(a_parts, w_parts_bf16, x, gate):
    B, T, D = x.shape
    tm = _row_tile(T)
    n_parts = len(a_parts)
    in_specs = [pl.BlockSpec((1, tm, a.shape[2]), lambda b, i: (b, i, 0)) for a in a_parts]
    in_specs += [pl.BlockSpec(w.shape, lambda b, i: (0, 0)) for w in w_parts_bf16]
    in_specs += [pl.BlockSpec((1, tm, D), lambda b, i: (b, i, 0)), _mod_spec(gate, tm)]
    return pl.pallas_call(
        functools.partial(_proj_residual_kernel, n_parts=n_parts),
        out_shape=jax.ShapeDtypeStruct((B, T, D), F32),
        grid=(B, T // tm),
        in_specs=in_specs,
        out_specs=pl.BlockSpec((1, tm, D), lambda b, i: (b, i, 0)),
        compiler_params=pltpu.CompilerParams(
            dimension_semantics=("parallel", "parallel"), vmem_limit_bytes=VMEM_LIMIT_BYTES),
        name="proj_residual",
    )(*a_parts, *w_parts_bf16, x, gate)


ROUTE_RANK_LANE = TOP_K


def _mod_norm_router_kernel(x_ref, g_ref, sc_ref, sh_ref, whi_ref, wlo_ref, b_ref, cnt_in_ref,
                            h_ref, ri_ref, rg_ref, cnt_ref, *, tm):
    @pl.when((pl.program_id(0) == 0) & (pl.program_id(1) == 0))
    def _():
        cnt_ref[...] = cnt_in_ref[...]

    h = _mod_norm(x_ref[0], g_ref[...], sc_ref[0], sh_ref[0])
    h_ref[0] = h
    h_hi, h_lo = _split_bf16(h)
    w_hi = whi_ref[...]
    logits = (jnp.dot(h_hi, w_hi, preferred_element_type=F32)
              + jnp.dot(h_lo, w_hi, preferred_element_type=F32)
              + jnp.dot(h_hi, wlo_ref[...], preferred_element_type=F32)) + b_ref[...]
    n_exp = logits.shape[-1]
    lane = lax.broadcasted_iota(jnp.int32, (1, n_exp), 1)
    work = logits
    vals, ids, hits = [], [], []
    for _ in range(TOP_K):
        m = jnp.max(work, axis=-1, keepdims=True)
        idx = jnp.min(jnp.where(work == m, lane, n_exp), axis=-1, keepdims=True)
        hit = lane == idx
        vals.append(m)
        ids.append(idx)
        hits.append(hit)
        work = jnp.where(hit, NEG_INF, work)
    exps = [jnp.exp(v - vals[0]) for v in vals]
    denom = exps[0] + exps[1] + exps[2] + exps[3]
    onehot = jnp.zeros(logits.shape, F32)
    for hit in hits:
        onehot = onehot + jnp.where(hit, 1.0, 0.0)
    row = lax.broadcasted_iota(jnp.int32, (tm, tm), 0)
    col = lax.broadcasted_iota(jnp.int32, (tm, tm), 1)
    earlier = jnp.where(row > col, 1.0, 0.0).astype(BF16)
    before = jnp.dot(earlier, onehot.astype(BF16), preferred_element_type=F32) + cnt_ref[...]
    cnt_ref[...] = cnt_ref[...] + jnp.sum(onehot, axis=0, keepdims=True)
    out_lane = lax.broadcasted_iota(jnp.int32, (1, LANES), 1)
    ri = jnp.zeros((tm, LANES), jnp.int32)
    rg = jnp.zeros((tm, LANES), F32)
    for k in range(TOP_K):
        rank = jnp.sum(jnp.where(hits[k], before, 0.0), axis=-1, keepdims=True).astype(jnp.int32)
        ri = jnp.where(out_lane == k, ids[k], ri)
        ri = jnp.where(out_lane == ROUTE_RANK_LANE + k, rank, ri)
        rg = jnp.where(out_lane == k, exps[k] / denom, rg)
    ri_ref[0] = ri
    rg_ref[0] = rg


def _mod_norm_router(x, norm_g, scale, shift, w_router, b_router, counts_in):
    B, T, D = x.shape
    tm = _row_tile(T)
    E = w_router.shape[1]
    w_hi, w_lo = _split_bf16(w_router)
    tok = lambda n: pl.BlockSpec((1, tm, n), lambda b, i: (b, i, 0))
    return pl.pallas_call(
        functools.partial(_mod_norm_router_kernel, tm=tm),
        out_shape=[jax.ShapeDtypeStruct((B, T, D), F32), jax.ShapeDtypeStruct((B, T, LANES), jnp.int32),
                   jax.ShapeDtypeStruct((B, T, LANES), F32), jax.ShapeDtypeStruct((1, E), F32)],
        grid=(B, T // tm),
        in_specs=[
            tok(D),
            pl.BlockSpec((1, D), lambda b, i: (0, 0)),
            _mod_spec(scale, tm),
            _mod_spec(shift, tm),
            pl.BlockSpec((D, E), lambda b, i: (0, 0)),
            pl.BlockSpec((D, E), lambda b, i: (0, 0)),
            pl.BlockSpec((1, E), lambda b, i: (0, 0)),
            pl.BlockSpec((1, E), lambda b, i: (0, 0)),
        ],
        out_specs=[tok(D), tok(LANES), tok(LANES), pl.BlockSpec((1, E), lambda b, i: (0, 0))],
        compiler_params=pltpu.CompilerParams(
            dimension_semantics=("arbitrary", "arbitrary"), vmem_limit_bytes=VMEM_LIMIT_BYTES),
        name="mod_norm_router",
    )(x, norm_g.reshape(1, D), scale, shift, w_hi, w_lo, b_router.reshape(1, E), counts_in)


def _moe_ffn_kernel(te_ref, tx_ref, tv_ref, x_ref, wu_ref, bu_ref, wd_ref, bd_ref, o_ref):
    i = pl.program_id(0)

    @pl.when(tv_ref[i] != 0)
    def _():
        hu = jnp.dot(x_ref[...].astype(BF16), wu_ref[0, 0].astype(BF16), preferred_element_type=F32) + bu_ref[0, 0]
        glu = jnp.minimum(hu[:, :D_FF], SWIGLU_LIMIT)
        lin = jnp.clip(hu[:, D_FF:], -SWIGLU_LIMIT, SWIGLU_LIMIT)
        act = glu * jax.nn.sigmoid(SWIGLU_ALPHA * glu) * (lin + 1.0)
        o_ref[...] = (jnp.dot(act.astype(BF16), wd_ref[0, 0].astype(BF16), preferred_element_type=F32)
                      + bd_ref[0, 0])

    @pl.when(tv_ref[i] == 0)
    def _():
        o_ref[...] = jnp.zeros_like(o_ref)


def _moe_ffn(tile_exp, tile_x, tile_valid, x_sorted, layer, w_up, b_up, w_down, b_down):
    n_slots, D = x_sorted.shape
    n_tiles = n_slots // MOE_TILE
    depth, E, _, F2 = w_up.shape
    expert = lambda i, te, tx, tv: (layer, te[i], 0, 0)
    grid_spec = pltpu.PrefetchScalarGridSpec(
        num_scalar_prefetch=3,
        grid=(n_tiles,),
        in_specs=[
            pl.BlockSpec((MOE_TILE, D), lambda i, te, tx, tv: (tx[i], 0)),
            pl.BlockSpec((1, 1, D, F2), expert),
            pl.BlockSpec((1, 1, 1, F2), expert),
            pl.BlockSpec((1, 1, F2 // 2, D), expert),
            pl.BlockSpec((1, 1, 1, D), expert),
        ],
        out_specs=pl.BlockSpec((MOE_TILE, D), lambda i, te, tx, tv: (i, 0)),
    )
    return pl.pallas_call(
        _moe_ffn_kernel,
        out_shape=jax.ShapeDtypeStruct((n_slots, D), F32),
        grid_spec=grid_spec,
        compiler_params=pltpu.CompilerParams(
            dimension_semantics=("arbitrary",), vmem_limit_bytes=VMEM_LIMIT_BYTES),
        name="moe_ffn",
    )(tile_exp, tile_x, tile_valid, x_sorted, w_up, b_up.reshape(depth, E, 1, F2), w_down,
      b_down.reshape(depth, E, 1, D))


DISPATCH_TILE = 512
COMBINE_TILE = 512


def _row_copy(src_ref, src_row, dst_ref, dst_row, sem):
    return pltpu.make_async_copy(src_ref.at[pl.ds(src_row, 1)], dst_ref.at[pl.ds(dst_row, 1)], sem)


def _moe_dispatch_kernel(dest_ref, h_ref, xs_in_ref, xs_ref, sem, *, tm):
    del xs_in_ref
    h_rows = h_ref.at[0]

    def issue(t, carry):
        for k in range(TOP_K):
            _row_copy(h_rows, t, xs_ref, dest_ref[0, 0, 0, t * TOP_K + k], sem).start()
        return carry

    lax.fori_loop(0, tm, issue, 0, unroll=4)
    for k in range(TOP_K):
        pltpu.make_async_copy(h_rows, xs_ref.at[pl.ds(0, tm)], sem).wait()


def _dest_spec(tm):
    return pl.BlockSpec((1, 1, 1, tm * TOP_K), lambda b, i: (b, i, 0, 0), memory_space=pltpu.SMEM)


def _moe_dispatch(dest, h, x_sorted):
    B, T, D = h.shape
    tm = min(T, DISPATCH_TILE)
    return pl.pallas_call(
        functools.partial(_moe_dispatch_kernel, tm=tm),
        out_shape=jax.ShapeDtypeStruct(x_sorted.shape, x_sorted.dtype),
        grid=(B, T // tm),
        in_specs=[_dest_spec(tm),
                  pl.BlockSpec((1, tm, D), lambda b, i: (b, i, 0)),
                  pl.BlockSpec(memory_space=pl.ANY)],
        out_specs=pl.BlockSpec(memory_space=pl.ANY),
        scratch_shapes=[pltpu.SemaphoreType.DMA(())],
        input_output_aliases={2: 0},
        compiler_params=pltpu.CompilerParams(dimension_semantics=("arbitrary", "arbitrary")),
        name="moe_dispatch",
    )(dest.reshape(B, T // tm, 1, tm * TOP_K), h, x_sorted)


def _moe_combine_kernel(dest_ref, ys_ref, rg_ref, x_ref, gf_ref, o_ref, buf, sem, *, tm):
    def issue(t, carry):
        for k in range(TOP_K):
            _row_copy(ys_ref, dest_ref[0, 0, 0, t * TOP_K + k], buf.at[k], t, sem).start()
        return carry

    lax.fori_loop(0, tm, issue, 0, unroll=4)
    for k in range(TOP_K):
        pltpu.make_async_copy(ys_ref.at[pl.ds(0, tm)], buf.at[k], sem).wait()
    gates = rg_ref[0]
    y = gates[:, 0:1] * buf[0]
    for k in range(1, TOP_K):
        y = y + gates[:, k:k + 1] * buf[k]
    o_ref[0] = x_ref[0] + gf_ref[0] * y


def _moe_combine(dest, y_sorted, route_g, x, gate_f):
    B, T, D = x.shape
    tm = min(T, COMBINE_TILE)
    tok = lambda n: pl.BlockSpec((1, tm, n), lambda b, i: (b, i, 0))
    return pl.pallas_call(
        functools.partial(_moe_combine_kernel, tm=tm),
        out_shape=jax.ShapeDtypeStruct((B, T, D), F32),
        grid=(B, T // tm),
        in_specs=[_dest_spec(tm), pl.BlockSpec(memory_space=pl.ANY), tok(LANES), tok(D), _mod_spec(gate_f, tm)],
        out_specs=tok(D),
        scratch_shapes=[pltpu.VMEM((TOP_K, tm, D), F32), pltpu.SemaphoreType.DMA(())],
        compiler_params=pltpu.CompilerParams(
            dimension_semantics=("parallel", "parallel"), vmem_limit_bytes=VMEM_LIMIT_BYTES),
        name="moe_combine",
    )(dest.reshape(B, T // tm, 1, tm * TOP_K), y_sorted, route_g, x, gate_f)


def _moe_plan(counts):
    counts = counts.astype(jnp.int32)
    padded = (counts + MOE_TILE - 1) // MOE_TILE * MOE_TILE
    pad_end = jnp.cumsum(padded)
    pad_start = pad_end - padded
    n_used = pad_end[-1] // MOE_TILE
    return pad_start, pad_end, n_used


def _moe_tiles(pad_end, n_used, n_tiles):
    tile = jnp.arange(n_tiles, dtype=jnp.int32)
    tile_valid = (tile < n_used).astype(jnp.int32)
    tile_x = jnp.minimum(tile, n_used - 1)
    tile_exp = jnp.sum((tile_x[:, None] * MOE_TILE >= pad_end[None, :]).astype(jnp.int32), axis=1)
    return jnp.minimum(tile_exp, N_EXPERTS - 1), tile_x, tile_valid


N_QHEADS = 8
LANES = 128
HALF = HEAD_DIM


def _nt_dot(a, b):
    return lax.dot_general(a, b, (((1,), (1,)), ((), ())), preferred_element_type=F32)


def _band_mask(qpos, kpos, window, dilation):
    delta = qpos - kpos
    valid = (delta >= 0) & (delta <= window)
    if dilation > 1:
        valid = valid & ((delta & (dilation - 1)) == 0)
    return valid


def _softmax_pv(s, valid, v, tq):
    n = s.shape[-1]
    s = jnp.where(valid[None], s.reshape(N_QHEADS, tq, n), NEG_INF)
    m = jnp.max(s, axis=-1, keepdims=True)
    p = jnp.exp(s - m)
    l = jnp.sum(p, axis=-1, keepdims=True)
    o = jnp.dot(p.reshape(N_QHEADS * tq, n).astype(BF16), v, preferred_element_type=F32)
    return o.reshape(N_QHEADS, tq, LANES) / l, m + jnp.log(l)


def _store_heads(o_ref, o, lse):
    if lse is None:
        o_ref[0] = o
        return
    lane = lax.broadcasted_iota(jnp.int32, (1, LANES), 1)
    for h in range(N_QHEADS):
        own = (lane < HALF) if h < N_QHEADS // 2 else (lane >= HALF)
        o_ref[0, h] = jnp.where(own, o[h], lse[h])


def _band_self_kernel(q_ref, k_ref, v_ref, o_ref, *, window, tq, span, want_lse):
    T = k_ref.shape[1]
    q0 = pl.program_id(1) * tq
    if span == T:
        start = 0
        k = k_ref[0]
        v = v_ref[0]
    else:
        start = pl.multiple_of(jnp.maximum(q0 - window, 0), LANES)
        k = k_ref[0, pl.ds(start, span), :]
        v = v_ref[0, pl.ds(start, span), :]
    q = q_ref[0].reshape(N_QHEADS * tq, LANES).astype(BF16)
    qpos = q0 + lax.broadcasted_iota(jnp.int32, (tq, 1), 0)
    kpos = start + lax.broadcasted_iota(jnp.int32, (1, span), 1)
    o, lse = _softmax_pv(_nt_dot(q, k), _band_mask(qpos, kpos, window, 1), v, tq)
    _store_heads(o_ref, o, lse if want_lse else None)


def _band_self(q, k, v, window, want_lse):
    B, _, T, _ = q.shape
    tq = min(T, ATTN_Q_TILE)
    span = min(T, window + tq)
    return pl.pallas_call(
        functools.partial(_band_self_kernel, window=window, tq=tq, span=span, want_lse=want_lse),
        out_shape=jax.ShapeDtypeStruct((B, N_QHEADS, T, LANES), F32),
        grid=(B, T // tq),
        in_specs=[
            pl.BlockSpec((1, N_QHEADS, tq, LANES), lambda b, i: (b, 0, i, 0)),
            pl.BlockSpec((1, T, LANES), lambda b, i: (b, 0, 0)),
            pl.BlockSpec((1, T, LANES), lambda b, i: (b, 0, 0)),
        ],
        out_specs=pl.BlockSpec((1, N_QHEADS, tq, LANES), lambda b, i: (b, 0, i, 0)),
        compiler_params=pltpu.CompilerParams(
            dimension_semantics=("parallel", "parallel"), vmem_limit_bytes=VMEM_LIMIT_BYTES),
        name="band_self",
    )(q, k, v)


def _band_tail_kernel(q_ref, kv_ref, o_ref, *, window, dilation, tq, want_lse):
    L = kv_ref.shape[1]
    kv = kv_ref[0]
    k = kv[:, :LANES].astype(BF16)
    v = kv[:, LANES:].astype(BF16)
    q = q_ref[0].reshape(N_QHEADS * tq, LANES).astype(BF16)
    qpos = (L - tq) + lax.broadcasted_iota(jnp.int32, (tq, 1), 0)
    kpos = lax.broadcasted_iota(jnp.int32, (1, L), 1)
    o, lse = _softmax_pv(_nt_dot(q, k), _band_mask(qpos, kpos, window, dilation), v, tq)
    _store_heads(o_ref, o, lse if want_lse else None)


def _band_tail(q, kv_all, window, dilation, want_lse):
    B, _, T, _ = q.shape
    L = kv_all.shape[1]
    return pl.pallas_call(
        functools.partial(_band_tail_kernel, window=window, dilation=dilation, tq=T, want_lse=want_lse),
        out_shape=jax.ShapeDtypeStruct((B, N_QHEADS, T, LANES), F32),
        grid=(B,),
        in_specs=[
            pl.BlockSpec((1, N_QHEADS, T, LANES), lambda b: (b, 0, 0, 0)),
            pl.BlockSpec((1, L, 2 * LANES), lambda b: (b, 0, 0)),
        ],
        out_specs=pl.BlockSpec((1, N_QHEADS, T, LANES), lambda b: (b, 0, 0, 0)),
        compiler_params=pltpu.CompilerParams(
            dimension_semantics=("parallel",), vmem_limit_bytes=VMEM_LIMIT_BYTES),
        name="band_tail",
    )(q, kv_all)


def _block_mean_kernel(x_ref, o_ref):
    rows = x_ref.shape[1]
    x = x_ref[0].reshape(rows // NSA_BLOCK, NSA_BLOCK, x_ref.shape[2])
    o_ref[0] = jnp.sum(x, axis=1) * (1.0 / NSA_BLOCK)


def _block_mean(kvc):
    B, T, W = kvc.shape
    tm = _row_tile(T)
    return pl.pallas_call(
        _block_mean_kernel,
        out_shape=jax.ShapeDtypeStruct((B, T // NSA_BLOCK, W), F32),
        grid=(B, T // tm),
        in_specs=[pl.BlockSpec((1, tm, W), lambda b, i: (b, i, 0))],
        out_specs=pl.BlockSpec((1, tm // NSA_BLOCK, W), lambda b, i: (b, i, 0)),
        compiler_params=pltpu.CompilerParams(dimension_semantics=("parallel", "parallel")),
        name="block_mean",
    )(kvc)


PAGES_PER_STEP = 32


def _feature_major_pages(cache):
    n_phys, rows = cache.shape[:2]
    return cache.transpose(0, 2, 3, 4, 1).reshape(n_phys, -1, rows)


def _page_specs(n_feat):
    return [pl.BlockSpec((1, n_feat, PAGE_SIZE), functools.partial(
        lambda b, s, pt, pg: (pt[b, s * PAGES_PER_STEP + pg], 0, 0), pg=pg)) for pg in range(PAGES_PER_STEP)]


def _page_block_mean_kernel(pt_ref, *refs):
    page_refs, o_ref = refs[:-1], refs[-1]
    step = pl.program_id(1)
    n_blocks = o_ref.shape[2]
    per_page = PAGE_SIZE // NSA_BLOCK

    @pl.when(step == 0)
    def _():
        o_ref[...] = jnp.zeros(o_ref.shape, F32)

    row_blk = lax.broadcasted_iota(jnp.int32, (PAGE_SIZE, 1), 0) // NSA_BLOCK
    col = lax.broadcasted_iota(jnp.int32, (1, n_blocks), 1)
    acc = o_ref[0]
    for pg, page_ref in enumerate(page_refs):
        first = (step * PAGES_PER_STEP + pg) * per_page
        avg = jnp.where(col == first + row_blk, 1.0 / NSA_BLOCK, 0.0).astype(BF16)
        hi, lo = _split_bf16(page_ref[0])
        acc = acc + jnp.dot(hi, avg, preferred_element_type=F32) + jnp.dot(lo, avg, preferred_element_type=F32)
    o_ref[0] = acc


def _page_block_mean(pages, page_table):
    B, n_pages = page_table.shape
    F = pages.shape[1]
    n_blocks = n_pages * PAGE_SIZE // NSA_BLOCK
    grid_spec = pltpu.PrefetchScalarGridSpec(
        num_scalar_prefetch=1,
        grid=(B, n_pages // PAGES_PER_STEP),
        in_specs=_page_specs(F),
        out_specs=pl.BlockSpec((1, F, n_blocks), lambda b, s, pt: (b, 0, 0)),
    )
    return pl.pallas_call(
        _page_block_mean_kernel,
        out_shape=jax.ShapeDtypeStruct((B, F, n_blocks), F32),
        grid_spec=grid_spec,
        compiler_params=pltpu.CompilerParams(dimension_semantics=("parallel", "arbitrary")),
        name="page_block_mean",
    )(page_table, *([pages] * PAGES_PER_STEP))


def _head_mean_sq(x, ones_bd):
    return jnp.dot((x * x).astype(BF16), ones_bd, preferred_element_type=F32)


def _cmp_finish_kernel(mean_ref, pe_ref, wk_ref, wv_ref, gain_ref, bd_ref, kc_ref, vc_ref):
    s = mean_ref[0] + pe_ref[...]
    sk = jnp.dot(s[:, :LANES].astype(BF16), wk_ref[...], preferred_element_type=F32)
    vc_ref[0] = jnp.dot(s[:, LANES:].astype(BF16), wv_ref[...], preferred_element_type=F32)
    kc_ref[0] = sk * lax.rsqrt(_head_mean_sq(sk, bd_ref[...]) + NORM_EPS) * gain_ref[...]


def _block_diag2(w):
    z = jnp.zeros_like(w)
    return jnp.concatenate([jnp.concatenate([w, z], axis=1), jnp.concatenate([z, w], axis=1)], axis=0)


def _cmp_finish(mean, pe, w_phi, kc_gain):
    B, n_cb, W = mean.shape
    pe_mean = jnp.mean(pe, axis=0)
    pe_row = jnp.concatenate([pe_mean[0], pe_mean[0], pe_mean[1], pe_mean[1]])[None, :]
    wk = _block_diag2(w_phi[0]).astype(BF16)
    wv = _block_diag2(w_phi[1]).astype(BF16)
    bd = _block_diag2(jnp.full((HALF, HALF), 1.0 / HALF, F32)).astype(BF16)
    gain = jnp.concatenate([kc_gain, kc_gain])[None, :]
    full = lambda shape: pl.BlockSpec(shape, lambda b: (0,) * len(shape))
    return pl.pallas_call(
        _cmp_finish_kernel,
        out_shape=[jax.ShapeDtypeStruct((B, n_cb, LANES), F32)] * 2,
        grid=(B,),
        in_specs=[pl.BlockSpec((1, n_cb, W), lambda b: (b, 0, 0)), full((1, W)), full((LANES, LANES)),
                  full((LANES, LANES)), full((1, LANES)), full((LANES, LANES))],
        out_specs=[pl.BlockSpec((1, n_cb, LANES), lambda b: (b, 0, 0))] * 2,
        compiler_params=pltpu.CompilerParams(dimension_semantics=("parallel",)),
        name="cmp_finish",
    )(mean, pe_row, wk, wv, gain, bd)


def _cmp_select_kernel(q_ref, kc_ref, vc_ref, ocmp_ref, sel_ref, *, tq, pos0):
    n_cb = kc_ref.shape[1]
    q0 = pl.program_id(1) * tq
    qh, ql = _split_bf16(q_ref[0].reshape(N_QHEADS * tq, LANES))
    kh, kl = _split_bf16(kc_ref[0])
    s = (_nt_dot(qh, kh) + _nt_dot(ql, kh) + _nt_dot(qh, kl)) * (HEAD_DIM ** -0.5)
    cur = (pos0 + q0 + lax.broadcasted_iota(jnp.int32, (tq, 1), 0)) // NSA_BLOCK
    blk = lax.broadcasted_iota(jnp.int32, (1, n_cb), 1)
    vis = blk < cur
    s = jnp.where(vis[None], s.reshape(N_QHEADS, tq, n_cb), NEG_INF)
    p = jnp.exp(s - jnp.max(s, axis=-1, keepdims=True))
    p = jnp.where(vis[None], p / jnp.sum(p, axis=-1, keepdims=True), 0.0)
    o = jnp.dot(p.reshape(N_QHEADS * tq, n_cb).astype(BF16), vc_ref[0].astype(BF16), preferred_element_type=F32)
    ocmp_ref[0] = o.reshape(N_QHEADS, tq, LANES)
    for kvh in range(NSA_KV_HEADS):
        g0 = kvh * NSA_GROUP
        imp = jnp.where(vis, p[g0] + p[g0 + 1] + p[g0 + 2] + p[g0 + 3], -1.0)
        if tq % LANES == 0:
            imp_t = imp.T
            blk_t = lax.broadcasted_iota(jnp.int32, (n_cb, tq), 0)
            cur_t = (pos0 + q0 + lax.broadcasted_iota(jnp.int32, (1, tq), 1)) // NSA_BLOCK
            rank = jnp.zeros((n_cb, tq), jnp.int32)
            for i in range(n_cb):
                row = imp_t[i:i + 1, :]
                wins_tie = jnp.where(blk_t > i, 1, 0)
                rank = rank + jnp.where(row > imp_t, 1, jnp.where(row == imp_t, wins_tie, 0))
            chosen = jnp.where(blk_t < cur_t, jnp.where(rank < NSA_N_SEL - 1, 1.0, 0.0),
                               jnp.where(blk_t == cur_t, 1.0, 0.0)).T
        else:
            rank = jnp.zeros((tq, n_cb), jnp.int32)
            for i in range(n_cb):
                col = imp[:, i:i + 1]
                wins_tie = jnp.where(blk > i, 1, 0)
                rank = rank + jnp.where(col > imp, 1, jnp.where(col == imp, wins_tie, 0))
            chosen = jnp.where(vis, jnp.where(rank < NSA_N_SEL - 1, 1.0, 0.0), jnp.where(blk == cur, 1.0, 0.0))
        sel_ref[0, kvh] = chosen.astype(BF16)


def _cmp_select(qn, kc, vc, pos0):
    B, _, T, _ = qn.shape
    n_cb = kc.shape[1]
    tq = min(T, Q_BLOCK)
    return pl.pallas_call(
        functools.partial(_cmp_select_kernel, tq=tq, pos0=pos0),
        out_shape=[jax.ShapeDtypeStruct((B, N_QHEADS, T, LANES), F32),
                   jax.ShapeDtypeStruct((B, NSA_KV_HEADS, T, n_cb), BF16)],
        grid=(B, T // tq),
        in_specs=[
            pl.BlockSpec((1, N_QHEADS, tq, LANES), lambda b, i: (b, 0, i, 0)),
            pl.BlockSpec((1, n_cb, LANES), lambda b, i: (b, 0, 0)),
            pl.BlockSpec((1, n_cb, LANES), lambda b, i: (b, 0, 0)),
        ],
        out_specs=[pl.BlockSpec((1, N_QHEADS, tq, LANES), lambda b, i: (b, 0, i, 0)),
                   pl.BlockSpec((1, NSA_KV_HEADS, tq, n_cb), lambda b, i: (b, 0, i, 0))],
        compiler_params=pltpu.CompilerParams(
            dimension_semantics=("parallel", "parallel"), vmem_limit_bytes=VMEM_LIMIT_BYTES),
        name="cmp_select",
    )(qn, kc, vc)


def _selection_bias_kv(sel, first_key, n_keys, causal):
    n_cb = sel.shape[-1]
    key_blk = (first_key + lax.broadcasted_iota(jnp.int32, (1, n_keys), 1)) // NSA_BLOCK
    expand = jnp.where(lax.broadcasted_iota(jnp.int32, (n_cb, 1), 0) == key_blk, 1.0, 0.0).astype(BF16)
    out = []
    for kvh in range(NSA_KV_HEADS):
        picked = jnp.dot(sel[kvh], expand, preferred_element_type=F32)
        bias = (picked - 1.0) * 1e30
        if causal is not None:
            bias = jnp.where(causal, bias, NEG_INF)
        out.append(bias)
    return out


def _selection_bias(sel, first_key, n_keys, causal):
    return jnp.concatenate([jnp.broadcast_to(b[None], (NSA_GROUP,) + b.shape)
                            for b in _selection_bias_kv(sel, first_key, n_keys, causal)], axis=0)


def _online_softmax_step(s, v, m_ref, l_ref, acc_ref, tq, v_feature_major=False):
    n = s.shape[-1]
    m_old = m_ref[...]
    m_new = jnp.maximum(m_old, jnp.max(s, axis=-1, keepdims=True))
    alpha = jnp.exp(m_old - m_new)
    p = jnp.exp(s - m_new[..., :1])
    l_ref[...] = alpha * l_ref[...] + jnp.sum(p, axis=-1, keepdims=True)
    p2 = p.reshape(N_QHEADS * tq, n).astype(BF16)
    pv = _nt_dot(p2, v) if v_feature_major else jnp.dot(p2, v, preferred_element_type=F32)
    acc_ref[...] = alpha * acc_ref[...] + pv.reshape(N_QHEADS, tq, LANES)
    m_ref[...] = m_new


def _init_softmax_state(m_ref, l_ref, acc_ref):
    m_ref[...] = jnp.full(m_ref.shape, NEG_INF, F32)
    l_ref[...] = jnp.zeros(l_ref.shape, F32)
    acc_ref[...] = jnp.zeros(acc_ref.shape, F32)


def _gated_heads(gate_pre, o_cmp, o_slc, o_win):
    g = jax.nn.sigmoid(gate_pre)
    lane = lax.broadcasted_iota(jnp.int32, (1, LANES), 1)
    pairs = []
    for c in range(N_QHEADS // 2):
        mixed = []
        for h in (2 * c, 2 * c + 1):
            mixed.append(g[:, 3 * h:3 * h + 1] * o_cmp[h] + g[:, 3 * h + 1:3 * h + 2] * o_slc[h]
                         + g[:, 3 * h + 2:3 * h + 3] * o_win[h])
        if 2 * c < N_QHEADS // 2:
            pairs.append(jnp.where(lane < HALF, mixed[0], pltpu.roll(mixed[1], HALF, 1)))
        else:
            pairs.append(jnp.where(lane < HALF, pltpu.roll(mixed[0], HALF, 1), mixed[1]))
    return jnp.concatenate(pairs, axis=-1)


SLC_KEY_TILE = 512


def _nsa_prompt_kernel(q_ref, ks_ref, vs_ref, kw_ref, vw_ref, sel_ref, ocmp_ref, g_ref, o_ref,
                       m_ref, l_ref, acc_ref, *, tq, tk, span):
    T = ks_ref.shape[1]
    q0 = pl.program_id(1) * tq
    q = q_ref[0].reshape(N_QHEADS * tq, LANES)
    qpos = q0 + lax.broadcasted_iota(jnp.int32, (tq, 1), 0)
    sel = sel_ref[0]
    _init_softmax_state(m_ref, l_ref, acc_ref)

    def key_tile(j, carry):
        k0 = pl.multiple_of(j * tk, tk)
        kpos = k0 + lax.broadcasted_iota(jnp.int32, (1, tk), 1)
        s = _nt_dot(q, ks_ref[0, pl.ds(k0, tk), :]).reshape(N_QHEADS, tq, tk)
        s = s + _selection_bias(sel, k0, tk, kpos <= qpos)
        _online_softmax_step(s, vs_ref[0, pl.ds(k0, tk), :], m_ref, l_ref, acc_ref, tq)
        return carry

    lax.fori_loop(0, (q0 + tq + tk - 1) // tk, key_tile, 0)
    o_slc = acc_ref[...] / l_ref[...]

    if span == T:
        start = 0
        kw, vw = kw_ref[0], vw_ref[0]
    else:
        start = pl.multiple_of(jnp.maximum(q0 - NSA_WINDOW, 0), LANES)
        kw, vw = kw_ref[0, pl.ds(start, span), :], vw_ref[0, pl.ds(start, span), :]
    kpos = start + lax.broadcasted_iota(jnp.int32, (1, span), 1)
    o_win, _ = _softmax_pv(_nt_dot(q, kw), _band_mask(qpos, kpos, NSA_WINDOW, 1), vw, tq)
    o_ref[0] = _gated_heads(g_ref[0], ocmp_ref[0], o_slc, o_win)


def _nsa_prompt(qr, ks, vs, kw, vw, sel, ocmp, gate_pre):
    B, _, T, _ = qr.shape
    n_cb = sel.shape[-1]
    tq = min(T, Q_BLOCK)
    tk = min(T, SLC_KEY_TILE)
    span = min(T, NSA_WINDOW + tq)
    seq = pl.BlockSpec((1, T, LANES), lambda b, i: (b, 0, 0))
    heads = pl.BlockSpec((1, N_QHEADS, tq, LANES), lambda b, i: (b, 0, i, 0))
    stat = pltpu.VMEM((N_QHEADS, tq, LANES), F32)
    return pl.pallas_call(
        functools.partial(_nsa_prompt_kernel, tq=tq, tk=tk, span=span),
        out_shape=jax.ShapeDtypeStruct((B, T, NSA_Q_W), F32),
        grid=(B, T // tq),
        in_specs=[heads, seq, seq, seq, seq,
                  pl.BlockSpec((1, NSA_KV_HEADS, tq, n_cb), lambda b, i: (b, 0, i, 0)),
                  heads,
                  pl.BlockSpec((1, tq, gate_pre.shape[2]), lambda b, i: (b, i, 0))],
        out_specs=pl.BlockSpec((1, tq, NSA_Q_W), lambda b, i: (b, i, 0)),
        scratch_shapes=[stat, stat, stat],
        compiler_params=pltpu.CompilerParams(
            dimension_semantics=("parallel", "arbitrary"), vmem_limit_bytes=VMEM_LIMIT_BYTES),
        name="nsa_prompt",
    )(qr, ks, vs, kw, vw, sel, ocmp, gate_pre)


def _nsa_paged_kernel(pt_ref, q_ref, *refs, tq, pos0):
    page_refs = refs[:PAGES_PER_STEP]
    new_ref, sel_ref, ocmp_ref, owin_ref, g_ref, o_ref, m_ref, l_ref, acc_ref = refs[PAGES_PER_STEP:]
    step = pl.program_id(1)
    q = q_ref[0].reshape(N_QHEADS * tq, LANES).astype(BF16)
    sel = sel_ref[0]

    @pl.when(step == 0)
    def _():
        _init_softmax_state(m_ref, l_ref, acc_ref)

    def attend(kt, vt, first_key, causal):
        n = kt.shape[1]
        s = jnp.dot(q, kt, preferred_element_type=F32).reshape(N_QHEADS, tq, n)
        if causal is None:
            s = s + _selection_bias(sel, first_key, n, None)
        else:
            s = jnp.where(causal[None], s, NEG_INF)
        _online_softmax_step(s, vt, m_ref, l_ref, acc_ref, tq, v_feature_major=True)

    attend(jnp.concatenate([r[0, :LANES, :].astype(BF16) for r in page_refs], axis=1),
           jnp.concatenate([r[0, LANES:, :].astype(BF16) for r in page_refs], axis=1),
           step * (PAGES_PER_STEP * PAGE_SIZE), None)

    @pl.when(step == pl.num_programs(1) - 1)
    def _():
        qpos = lax.broadcasted_iota(jnp.int32, (tq, 1), 0)
        kpos = lax.broadcasted_iota(jnp.int32, (1, PAGE_SIZE), 1)
        new = new_ref[0]
        attend(new[:LANES, :].astype(BF16), new[LANES:, :].astype(BF16), pos0, kpos <= qpos)
        o_ref[0] = _gated_heads(g_ref[0], ocmp_ref[0], acc_ref[...] / l_ref[...], owin_ref[0])


def _nsa_paged(page_table, qr, cache, new_tile, sel, ocmp, owin, gate_pre, pos0):
    B, _, T, _ = qr.shape
    n_pages = page_table.shape[1]
    n_cb = sel.shape[-1]
    W = cache.shape[1]
    heads = pl.BlockSpec((1, N_QHEADS, T, LANES), lambda b, s, pt: (b, 0, 0, 0))
    stat = pltpu.VMEM((N_QHEADS, T, LANES), F32)
    grid_spec = pltpu.PrefetchScalarGridSpec(
        num_scalar_prefetch=1,
        grid=(B, n_pages // PAGES_PER_STEP),
        in_specs=[heads] + _page_specs(W) + [
            pl.BlockSpec((1, W, PAGE_SIZE), lambda b, s, pt: (b, 0, 0)),
            pl.BlockSpec((1, NSA_KV_HEADS, T, n_cb), lambda b, s, pt: (b, 0, 0, 0)),
            heads, heads,
            pl.BlockSpec((1, T, gate_pre.shape[2]), lambda b, s, pt: (b, 0, 0))],
        out_specs=pl.BlockSpec((1, T, NSA_Q_W), lambda b, s, pt: (b, 0, 0)),
        scratch_shapes=[stat, stat, stat],
    )
    return pl.pallas_call(
        functools.partial(_nsa_paged_kernel, tq=T, pos0=pos0),
        out_shape=jax.ShapeDtypeStruct((B, T, NSA_Q_W), F32),
        grid_spec=grid_spec,
        compiler_params=pltpu.CompilerParams(
            dimension_semantics=("parallel", "arbitrary"), vmem_limit_bytes=VMEM_LIMIT_BYTES),
        name="nsa_paged",
    )(page_table, qr, *([cache] * PAGES_PER_STEP), new_tile, sel, ocmp, owin, gate_pre)


def _rope_tables(pos):
    half = ROT_DIM // 2
    inv = jnp.exp(-math.log(ROPE_THETA) * jnp.arange(half, dtype=F32) * (2.0 / ROT_DIM))
    ang = pos.astype(F32)[:, None] * inv[None, :]
    cos, sin = jnp.cos(ang), jnp.sin(ang)
    T = pos.shape[0]
    zeros = lambda n: jnp.zeros((T, n), F32)
    cos_t = jnp.concatenate([cos, cos, jnp.ones((T, HALF - ROT_DIM), F32)], axis=-1)
    msin_t = jnp.concatenate([-sin, zeros(HALF - half)], axis=-1)
    psin_t = jnp.concatenate([zeros(half), sin, zeros(HALF - ROT_DIM)], axis=-1)
    twice = lambda a: jnp.concatenate([a, a], axis=-1)
    return twice(cos_t), twice(msin_t), twice(psin_t)


def _rope_lanes(x, cos_t, msin_t, psin_t):
    half = ROT_DIM // 2
    return x * cos_t + pltpu.roll(x, LANES - half, 1) * msin_t + pltpu.roll(x, half, 1) * psin_t


def _head_rms(x, ones_mat, gain):
    return x * lax.rsqrt(_head_mean_sq(x, ones_mat) + NORM_EPS) * gain


def _store_head_pair(heads_ref, pair, val, put=None):
    lower = lax.broadcasted_iota(jnp.int32, (1, LANES), 1) < HALF
    swapped = pltpu.roll(val, HALF, 1)
    if 2 * pair < N_QHEADS // 2:
        first, second = jnp.where(lower, val, 0.0), jnp.where(lower, swapped, 0.0)
    else:
        first, second = jnp.where(lower, 0.0, swapped), jnp.where(lower, 0.0, val)
    for hd, head_val in ((2 * pair, first), (2 * pair + 1, second)):
        if put is None:
            heads_ref[hd] = head_val.astype(heads_ref.dtype)
        else:
            put(hd, head_val)


def _ab_proj_kernel(x_ref, g_ref, sc_ref, sh_ref, w_ref, cos_ref, msin_ref, psin_ref, gq_ref, gks_ref, gkw_ref,
                    bd_ref, lng_ref, lnb_ref, wm_ref, bias_ref,
                    qn_ref, qr_ref, kvc_ref, kvs_ref, ks_ref, vs_ref, kvw_ref, kw_ref, vw_ref, gate_ref,
                    ob_ref, vln_ref, *, tm):
    h = _mod_norm(x_ref[0], g_ref[...], sc_ref[0], sh_ref[0])
    y = jnp.dot(h.astype(BF16), w_ref[...], preferred_element_type=F32)
    tabs = (cos_ref[...], msin_ref[...], psin_ref[...])
    for pair in range(N_QHEADS // 2):
        qn = _head_rms(y[:, pair * LANES:(pair + 1) * LANES], bd_ref[...], gq_ref[...])
        _store_head_pair(qn_ref.at[0], pair, qn)
        _store_head_pair(qr_ref.at[0], pair, _rope_lanes(qn, *tabs) * (HEAD_DIM ** -0.5))
    off = NSA_Q_W
    kvc_ref[0] = y[:, off:off + NSA_KV_W]
    off += NSA_KV_W
    for kv_ref, k_ref, v_ref, gain_ref in ((kvs_ref, ks_ref, vs_ref, gks_ref), (kvw_ref, kw_ref, vw_ref, gkw_ref)):
        k = _rope_lanes(_head_rms(y[:, off:off + LANES], bd_ref[...], gain_ref[...]), *tabs)
        v = y[:, off + LANES:off + NSA_KV_W]
        kv_ref[0, :, :LANES] = k
        kv_ref[0, :, LANES:] = v
        k_ref[0] = k.astype(BF16)
        v_ref[0] = v.astype(BF16)
        off += NSA_KV_W
    u = jax.nn.gelu(y[:, off:off + GM_WIDTH])
    v = jax.nn.gelu(y[:, off + GM_WIDTH:off + 2 * GM_WIDTH])
    off += 2 * GM_WIDTH
    gate_ref[0] = y[:, off:]
    mu = jnp.mean(v, axis=-1, keepdims=True)
    var = jnp.mean(jnp.square(v - mu), axis=-1, keepdims=True)
    vln = (v - mu) * lax.rsqrt(var + NORM_EPS) * lng_ref[...] + lnb_ref[...]
    vln_ref[0] = vln
    lane = lax.broadcasted_iota(jnp.int32, (1, LANES), 1)
    for c in range(tm // GM_CHUNK):
        rows = slice(c * GM_CHUNK, (c + 1) * GM_CHUNK)
        mixed = []
        for p in range(GM_GROUPS // 2):
            vp = vln[rows, p * LANES:(p + 1) * LANES]
            lo = jnp.where(lane < HALF, vp, 0.0).astype(BF16)
            hi = jnp.where(lane < HALF, 0.0, vp).astype(BF16)
            mixed.append(jnp.dot(wm_ref[2 * p], lo, preferred_element_type=F32)
                         + jnp.dot(wm_ref[2 * p + 1], hi, preferred_element_type=F32))
        ob_ref[0, rows, :] = u[rows] * (jnp.concatenate(mixed, axis=-1) + bias_ref[...])


def _ab_proj(x, norm_g, scale, shift, w_in, pos, qk_gain, ln_g, ln_b, wm, mix_bias):
    B, T, D = x.shape
    tm = _row_tile(T)
    g0, g1 = AB_SPLITS[0], AB_SPLITS[0] + AB_SPLITS[1]
    w = jnp.concatenate([w_in[:, :g0], w_in[:, g1:], w_in[:, g0:g1]], axis=1).astype(BF16)
    N = w.shape[1]
    cos_t, msin_t, psin_t = _rope_tables(pos)
    bd = _block_diag2(jnp.full((HALF, HALF), 1.0 / HALF, F32)).astype(BF16)
    two = lambda g: jnp.concatenate([g, g])[None, :]
    const = lambda a: pl.BlockSpec(a.shape, lambda b, i: (0,) * a.ndim)
    tok = lambda n: pl.BlockSpec((1, tm, n), lambda b, i: (b, i, 0))
    heads = pl.BlockSpec((1, N_QHEADS, tm, LANES), lambda b, i: (b, 0, i, 0))
    table = pl.BlockSpec((tm, LANES), lambda b, i: (i, 0))
    consts = [two(qk_gain[0]), two(qk_gain[2]), two(qk_gain[3]), bd, ln_g[None, :], ln_b[None, :], wm, mix_bias]
    sds = lambda shape, dt: jax.ShapeDtypeStruct(shape, dt)
    return pl.pallas_call(
        functools.partial(_ab_proj_kernel, tm=tm),
        out_shape=[sds((B, N_QHEADS, T, LANES), F32), sds((B, N_QHEADS, T, LANES), BF16),
                   sds((B, T, NSA_KV_W), F32),
                   sds((B, T, NSA_KV_W), F32), sds((B, T, LANES), BF16), sds((B, T, LANES), BF16),
                   sds((B, T, NSA_KV_W), F32), sds((B, T, LANES), BF16), sds((B, T, LANES), BF16),
                   sds((B, T, 3 * NSA_HEADS), F32), sds((B, T, GM_WIDTH), F32), sds((B, T, GM_WIDTH), F32)],
        grid=(B, T // tm),
        in_specs=[tok(D), pl.BlockSpec((1, D), lambda b, i: (0, 0)), _mod_spec(scale, tm), _mod_spec(shift, tm),
                  pl.BlockSpec((D, N), lambda b, i: (0, 0)), table, table, table] + [const(a) for a in consts],
        out_specs=[heads, heads, tok(NSA_KV_W), tok(NSA_KV_W), tok(LANES), tok(LANES), tok(NSA_KV_W), tok(LANES),
                   tok(LANES), tok(3 * NSA_HEADS), tok(GM_WIDTH), tok(GM_WIDTH)],
        compiler_params=pltpu.CompilerParams(
            dimension_semantics=("parallel", "parallel"), vmem_limit_bytes=VMEM_LIMIT_BYTES),
        name="ab_proj",
    )(x, norm_g.reshape(1, D), scale, shift, w, cos_t, msin_t, psin_t, *consts)


def _chunk_mix_weights(ws, bs, rows_per_seq):
    n = min(rows_per_seq, GM_CHUNK)
    wm = jnp.where(jnp.tril(jnp.ones((n, n), bool)), ws[:, :n, :n], 0.0)
    if n < GM_CHUNK:
        eye = jnp.eye(GM_CHUNK // n, dtype=F32)
        wm = jnp.einsum('ab,gij->gaibj', eye, wm).reshape(GM_GROUPS, GM_CHUNK, GM_CHUNK)
    bias = jnp.tile(jnp.transpose(bs[:, :n]), (GM_CHUNK // n, 1))
    return wm.astype(BF16), jnp.repeat(bias, GM_GROUP_W, axis=1)


C_GROUP_W = DIL_Q_W + 2 * DIL_KV_W


def _c_proj_kernel(x_ref, g_ref, sc_ref, sh_ref, w_ref, cos_ref, msin_ref, psin_ref, gq_ref, gk_ref,
                   bd_ref, *refs, tm, dils):
    out_refs, stage = refs[:-1], refs[-1]
    h = _mod_norm(x_ref[0], g_ref[...], sc_ref[0], sh_ref[0])
    y = jnp.dot(h.astype(BF16), w_ref[...], preferred_element_type=F32)
    tabs = (cos_ref[...], msin_ref[...], psin_ref[...])

    def put(dst, val, d):
        if d == 1:
            dst[0] = val.astype(dst.dtype)
            return
        stage[...] = val
        for r in range(d):
            dst[r] = stage[pl.ds(r, tm // d, stride=d), :].astype(dst.dtype)

    for g, d in enumerate(dils):
        qd_ref, kd_ref, vd_ref, kv_ref = out_refs[4 * g:4 * g + 4]
        off = g * C_GROUP_W
        for pair in range(N_QHEADS // 2):
            lanes = slice(off + pair * LANES, off + (pair + 1) * LANES)
            q = _rope_lanes(_head_rms(y[:, lanes], bd_ref[...], gq_ref[...]), *tabs) * (HEAD_DIM ** -0.5)
            _store_head_pair(None, pair, q, put=lambda hd, val, qd_ref=qd_ref, d=d: put(qd_ref.at[:, hd], val, d))
        k = _rope_lanes(_head_rms(y[:, off + DIL_Q_W:off + DIL_Q_W + LANES], bd_ref[...], gk_ref[...]), *tabs)
        v = y[:, off + DIL_Q_W + LANES:off + C_GROUP_W]
        kv_ref[0, :, :LANES] = k
        kv_ref[0, :, LANES:] = v
        put(kd_ref, k, d)
        put(vd_ref, v, d)


def _c_proj(x, norm_g, scale, shift, w_in, pos, qk_gain, dils):
    B, T, D = x.shape
    tm = _row_tile(T)
    w = w_in.astype(BF16)
    N = w.shape[1]
    cos_t, msin_t, psin_t = _rope_tables(pos)
    bd = _block_diag2(jnp.full((HALF, HALF), 1.0 / HALF, F32)).astype(BF16)
    two = lambda g: jnp.concatenate([g, g])[None, :]
    consts = [two(qk_gain[0]), two(qk_gain[1]), bd]
    const = lambda a: pl.BlockSpec(a.shape, lambda b, i: (0,) * a.ndim)
    tok = lambda n: pl.BlockSpec((1, tm, n), lambda b, i: (b, i, 0))
    table = pl.BlockSpec((tm, LANES), lambda b, i: (i, 0))
    sds = lambda shape, dt: jax.ShapeDtypeStruct(shape, dt)
    out_shape, out_specs = [], []
    for d in dils:
        out_shape += [sds((B * d, N_QHEADS, T // d, LANES), BF16), sds((B * d, T // d, LANES), BF16),
                      sds((B * d, T // d, LANES), BF16), sds((B, T, 2 * DIL_KV_W), F32)]
        seq = pl.BlockSpec((d, tm // d, LANES), lambda b, i: (b, i, 0))
        out_specs += [pl.BlockSpec((d, N_QHEADS, tm // d, LANES), lambda b, i: (b, 0, i, 0)), seq, seq,
                      tok(2 * DIL_KV_W)]
    return pl.pallas_call(
        functools.partial(_c_proj_kernel, tm=tm, dils=tuple(dils)),
        out_shape=out_shape,
        grid=(B, T // tm),
        in_specs=[tok(D), pl.BlockSpec((1, D), lambda b, i: (0, 0)), _mod_spec(scale, tm), _mod_spec(shift, tm),
                  pl.BlockSpec((D, N), lambda b, i: (0, 0)), table, table, table] + [const(a) for a in consts],
        out_specs=out_specs,
        scratch_shapes=[pltpu.VMEM((tm, LANES), F32)],
        compiler_params=pltpu.CompilerParams(
            dimension_semantics=("parallel", "parallel"), vmem_limit_bytes=VMEM_LIMIT_BYTES),
        name="c_proj",
    )(x, norm_g.reshape(1, D), scale, shift, w, cos_t, msin_t, psin_t, *consts)


def _dil_merge_proj_kernel(*refs, tm, dils):
    o_refs = refs[:len(dils)]
    w_ref, x_ref, gate_ref, out_ref, stage = refs[len(dils):]
    lane = lax.broadcasted_iota(jnp.int32, (1, LANES), 1)
    merged = []
    for hd in range(N_QHEADS):
        lower = hd < N_QHEADS // 2
        vals = []
        for o_ref, d in zip(o_refs, dils):
            if d == 1:
                vals.append(o_ref[0, hd])
            else:
                for r in range(d):
                    stage[pl.ds(r, tm // d, stride=d), :] = o_ref[r, hd]
                vals.append(stage[...])
        lses = [v[:, HALF:HALF + 1] if lower else v[:, 0:1] for v in vals]
        top = jnp.maximum(jnp.maximum(lses[0], lses[1]), lses[2])
        ws = [jnp.exp(l - top) for l in lses]
        total = ws[0] + ws[1] + ws[2]
        merged.append((ws[0] * vals[0] + ws[1] * vals[1] + ws[2] * vals[2]) / total)
    pairs = []
    for c in range(N_QHEADS // 2):
        a, b = merged[2 * c], merged[2 * c + 1]
        if 2 * c < N_QHEADS // 2:
            pairs.append(jnp.where(lane < HALF, a, pltpu.roll(b, HALF, 1)))
        else:
            pairs.append(jnp.where(lane < HALF, pltpu.roll(a, HALF, 1), b))
    o = jnp.concatenate(pairs, axis=-1)
    y = jnp.dot(o.astype(BF16), w_ref[...], preferred_element_type=F32)
    out_ref[0] = x_ref[0] + gate_ref[0] * y


def _dil_merge_proj(outs, dils, w_out, x, gate):
    B, T, D = x.shape
    tm = _row_tile(T)
    in_specs = [pl.BlockSpec((d, N_QHEADS, tm // d, LANES), lambda b, i: (b, 0, i, 0)) for d in dils]
    in_specs += [pl.BlockSpec(w_out.shape, lambda b, i: (0, 0)),
                 pl.BlockSpec((1, tm, D), lambda b, i: (b, i, 0)), _mod_spec(gate, tm)]
    return pl.pallas_call(
        functools.partial(_dil_merge_proj_kernel, tm=tm, dils=tuple(dils)),
        out_shape=jax.ShapeDtypeStruct((B, T, D), F32),
        grid=(B, T // tm),
        in_specs=in_specs,
        out_specs=pl.BlockSpec((1, tm, D), lambda b, i: (b, i, 0)),
        scratch_shapes=[pltpu.VMEM((tm, LANES), F32)],
        compiler_params=pltpu.CompilerParams(
            dimension_semantics=("parallel", "parallel"), vmem_limit_bytes=VMEM_LIMIT_BYTES),
        name="dil_merge_proj",
    )(*outs, w_out, x, gate)


def _kv5(x):
    return x.reshape(x.shape[:2] + (2, NSA_KV_HEADS, HEAD_DIM))


def _mixer_ab(x, norm_g, sc, sh, gate_m, w_in, w_out, start, pasts, weights, page_table, seq_shape):
    qk_gain, pe, w_phi, ln_g, ln_b, ws, bs = weights
    B, T = seq_shape
    pos = start + jnp.arange(x.shape[1], dtype=jnp.int32) % T
    wm, mix_bias = _chunk_mix_weights(ws, bs, T)
    qn, qr, kvc, kvs, ks, vs, kvw, kw, vw, gate, o_b, vln = _ab_proj(
        x, norm_g, sc, sh, w_in, pos, qk_gain, ln_g, ln_b, wm, mix_bias)
    if pasts is None:
        kc, vc = _cmp_finish(_block_mean(kvc), pe, w_phi, qk_gain[1])
        ocmp, sel = _cmp_select(qn, kc, vc, start)
        o_a = _nsa_prompt(qr, ks, vs, kw, vw, sel, ocmp, gate)
        kvw_all = kvw
    else:
        per_seq = lambda a: a.reshape((B, T) + a.shape[2:])
        heads = lambda a: a.reshape(N_QHEADS, B, T, LANES).transpose(1, 0, 2, 3)
        kvc, kvs, kvw, gate, vln = (per_seq(a) for a in (kvc, kvs, kvw, gate, vln))
        qn, qr = heads(qn), heads(qr.astype(F32))
        cache_cmp, cache_slc, win_past = pasts
        means = jnp.swapaxes(_page_block_mean(cache_cmp, page_table), 1, 2)
        kc, vc = _cmp_finish(means, pe, w_phi, qk_gain[1])
        ocmp, sel = _cmp_select(qn, kc, vc, start)
        kvw_all = jnp.concatenate([win_past, kvw], axis=1)
        owin = _band_tail(qr, kvw_all, NSA_WINDOW, 1, False)
        new_tile = jnp.swapaxes(jnp.pad(kvs, ((0, 0), (0, PAGE_SIZE - T), (0, 0))), 1, 2)
        o_a = _nsa_paged(page_table, qr, cache_slc, new_tile, sel, ocmp, owin, gate, start)
    keep_w = min(NSA_WINDOW, kvw_all.shape[1])
    n_cur = (T - 1) % GM_CHUNK + 1
    x_new = _proj_residual([o_a.reshape(o_b.shape), o_b], (w_out[:NSA_Q_W], w_out[NSA_Q_W:]), x, gate_m)
    return x_new, (_kv5(kvc), _kv5(kvs), _kv5(kvw_all[:, kvw_all.shape[1] - keep_w:]), vln[:, T - n_cur:])


def _mixer_c(x, norm_g, sc, sh, gate_m, w_in, w_out, start, pasts, qk_gain, seq_shape):
    B, T = seq_shape
    pos = start + jnp.arange(x.shape[1], dtype=jnp.int32) % T
    dils = tuple(d for _, d in DIL_CFG) if pasts is None else (1,) * len(DIL_CFG)
    res = _c_proj(x, norm_g, sc, sh, w_in, pos, qk_gain, dils)
    outs, new_bufs = [], []
    for g, (window, dilation) in enumerate(DIL_CFG):
        qd, kd, vd, kv_new = res[4 * g:4 * g + 4]
        if pasts is None:
            outs.append(_band_self(qd, kd, vd, window // dilation, True))
            kv_all = kv_new
        else:
            kv_all = jnp.concatenate([pasts[g], kv_new.reshape(B, T, 2 * DIL_KV_W)], axis=1)
            q = qd.reshape(N_QHEADS, B, T, LANES).transpose(1, 0, 2, 3).astype(F32)
            o = _band_tail(q, kv_all, window, dilation, True)
            outs.append(o.transpose(1, 0, 2, 3).reshape(1, N_QHEADS, B * T, LANES))
        keep = min(window, kv_all.shape[1])
        new_bufs.append(kv_all[:, kv_all.shape[1] - keep:].reshape(B, keep, 2, DIL_KV_HEADS, HEAD_DIM))
    return _dil_merge_proj(outs, dils, w_out, x, gate_m), tuple(new_bufs)


def _adaln_kernel(c_ref, w_ref, b_ref, o_ref):
    act = jax.nn.silu(c_ref[...]).astype(BF16)
    o_ref[...] = jnp.dot(act, w_ref[0].astype(BF16), preferred_element_type=F32) + b_ref[0]


def _adaln(c, layer, w, b):
    B, D = c.shape
    depth, _, N = w.shape
    tn = D
    mod = pl.pallas_call(
        _adaln_kernel,
        out_shape=jax.ShapeDtypeStruct((B, N), F32),
        grid=(N // tn,),
        in_specs=[pl.BlockSpec((B, D), lambda j: (0, 0)), pl.BlockSpec((1, D, tn), lambda j: (layer, 0, j)),
                  pl.BlockSpec((1, 1, tn), lambda j: (layer, 0, j))],
        out_specs=pl.BlockSpec((B, tn), lambda j: (0, j)),
        compiler_params=pltpu.CompilerParams(dimension_semantics=("parallel",)),
        name="adaln",
    )(c, w, b.reshape(depth, 1, N))
    return [mod[:, None, j * D:(j + 1) * D] for j in range(N // D)]


def _expand_mod(m, t):
    B, _, D = m.shape
    return jnp.broadcast_to(m, (B, t, D)).reshape(1, B * t, D)


def _slot_buffer(n_tok, d_model):
    n_tiles = -(-(n_tok * TOP_K + N_EXPERTS * (MOE_TILE - 1)) // MOE_TILE)
    return jnp.zeros((n_tiles * MOE_TILE, d_model), F32)


def _layer(groups, mixers, norm_m, norm_f, moe_w, x_sorted):
    w_router, b_router, *ffn_w = moe_w
    xs, states, routed = [], [], []
    counts = jnp.zeros((1, N_EXPERTS), F32)
    for (x, mods), mixer in zip(groups, mixers):
        sh_m, sc_m, g_m, sh_f, sc_f, g_f = mods
        B, T, D = x.shape
        x, state = mixer(x, norm_m, sc_m, sh_m, g_m)
        h, route_i, route_g, counts = _mod_norm_router(x, norm_f, sc_f, sh_f, w_router, b_router, counts)
        xs.append(x)
        states.append(state)
        routed.append((h, route_i, route_g))
    n_tiles = x_sorted.shape[0] // MOE_TILE
    pad_start, pad_end, n_used = _moe_plan(counts[0])
    dests = []
    for h, route_i, _ in routed:
        expert = route_i[..., :TOP_K]
        rank = route_i[..., ROUTE_RANK_LANE:ROUTE_RANK_LANE + TOP_K]
        first = jnp.sum(jnp.where(expert[..., None] == jnp.arange(N_EXPERTS), pad_start, 0), axis=-1)
        dests.append((first + rank).astype(jnp.int32))
        x_sorted = _moe_dispatch(dests[-1], h, x_sorted)
    y_sorted = _moe_ffn(*_moe_tiles(pad_end, n_used, n_tiles), x_sorted, *ffn_w)
    outs = [_moe_combine(dest, y_sorted, route_g, x, mods[5])
            for dest, (_, _, route_g), x, (_, mods) in zip(dests, routed, xs, groups)]
    return outs, states, x_sorted


def _stack(states, j):
    return jnp.stack([s[j] for s in states], axis=0)


def kernel(x_prompt, x_sample, cache_cmp_kv, cache_slc_kv, state_win_kv, state_dil0_kv, state_dil1_kv,
           state_dil2_kv, page_table, c_prompt, c_sample, norm_mix, norm_ffn, w_ada, b_ada, w_in_ab, w_out_ab,
           nsa_qk_gain, nsa_pe, nsa_w_phi, gm_ln_g, gm_ln_b, gm_ws, gm_bs, w_in_c, w_out_c, dil_qk_gain,
           w_router, b_router, w_up, b_up, w_down, b_down):
    depth = norm_mix.shape[0]
    past_len = page_table.shape[1] * PAGE_SIZE
    Bs, Ts, D = x_sample.shape
    x_p = x_prompt
    x_s = x_sample.reshape(1, Bs * Ts, D)
    ab_p, ab_s, dil_p, dil_s = [], [], [], []
    slots = _slot_buffer(x_prompt.shape[0] * x_prompt.shape[1] + Bs * Ts, D)
    for layer in range(depth):
        i = layer // 2
        moe_w = (w_router[layer], b_router[layer], layer, w_up, b_up, w_down, b_down)
        mods_p = _adaln(c_prompt, layer, w_ada, b_ada)
        mods_s = [_expand_mod(m, Ts) for m in _adaln(c_sample, layer, w_ada, b_ada)]
        groups = [(x_p, mods_p), (x_s, mods_s)]
        if layer % 2 == 0:
            wts = (nsa_qk_gain[i], nsa_pe[i], nsa_w_phi[i], gm_ln_g[i], gm_ln_b[i], gm_ws[i], gm_bs[i])
            flat_kv = lambda a: a.reshape(a.shape[:2] + (NSA_KV_W,))
            pasts_s = (_feature_major_pages(cache_cmp_kv[i]), _feature_major_pages(cache_slc_kv[i]),
                       flat_kv(state_win_kv[i]))

            w_io = (w_in_ab[i], w_out_ab[i].astype(BF16))

            def mix_p(x, g, sc, sh, gm, wts=wts, w_io=w_io):
                return _mixer_ab(x, g, sc, sh, gm, *w_io, 0, None, wts, None, x.shape[:2])

            def mix_s(x, g, sc, sh, gm, wts=wts, w_io=w_io, pasts_s=pasts_s):
                return _mixer_ab(x, g, sc, sh, gm, *w_io, past_len, pasts_s, wts, page_table, (Bs, Ts))

            (x_p, x_s), (st_p, st_s), slots = _layer(
                groups, (mix_p, mix_s), norm_mix[layer], norm_ffn[layer], moe_w, slots)
            ab_p.append(st_p)
            ab_s.append(st_s)
        else:
            flat_kv = lambda a: a.reshape(a.shape[:2] + (2 * DIL_KV_W,))
            pasts_s = (flat_kv(state_dil0_kv[i]), flat_kv(state_dil1_kv[i]), flat_kv(state_dil2_kv[i]))

            w_io = (w_in_c[i], w_out_c[i].astype(BF16))

            def mix_p(x, g, sc, sh, gm, gain=dil_qk_gain[i], w_io=w_io):
                return _mixer_c(x, g, sc, sh, gm, *w_io, 0, None, gain, x.shape[:2])

            def mix_s(x, g, sc, sh, gm, pasts_s=pasts_s, gain=dil_qk_gain[i], w_io=w_io):
                return _mixer_c(x, g, sc, sh, gm, *w_io, past_len, pasts_s, gain, (Bs, Ts))

            (x_p, x_s), (st_p, st_s), slots = _layer(
                groups, (mix_p, mix_s), norm_mix[layer], norm_ffn[layer], moe_w, slots)
            dil_p.append(st_p)
            dil_s.append(st_s)
    x_s = x_s.reshape(Bs, Ts, D)
    cmp_p, cmp_s = _stack(ab_p, 0), _stack(ab_s, 0)
    slc_p, slc_s = _stack(ab_p, 1), _stack(ab_s, 1)
    win_p, win_s = _stack(ab_p, 2), _stack(ab_s, 2)
    gmv_p, gmv_s = _stack(ab_p, 3), _stack(ab_s, 3)
    d0_p, d0_s = _stack(dil_p, 0), _stack(dil_s, 0)
    d1_p, d1_s = _stack(dil_p, 1), _stack(dil_s, 1)
    d2_p, d2_s = _stack(dil_p, 2), _stack(dil_s, 2)
    return (x_p, x_s, cmp_p, cmp_s, slc_p, slc_s, win_p, win_s, gmv_p, gmv_s, d0_p, d0_s, d1_p, d1_s, d2_p, d2_s)
```
